```python
import jax, jax.numpy as jnp
from jax import lax
import numpy as np

D_MODEL = 4096
BATCH = 2
SEQ = 4096
DEPTH = 2

NORM_EPS = 1e-6
LN_EPS = 1e-5
D_FF = 11008
BRANCH_WIDTH = D_MODEL // 4
N_BRANCH = 4
GM_WIDTH = BRANCH_WIDTH
GM_GROUPS = 8
GM_CHUNK = 128
HG_HEADS = 8
HG_DK = 128
HG_DV = BRANCH_WIDTH // HG_HEADS
HG_CHUNK = 64
SSM_INNER = BRANCH_WIDTH
SSM_HEADDIM = 64
SSM_HEADS = SSM_INNER // SSM_HEADDIM
SSM_GROUPS = 4
SSM_STATE = 128
SSM_CONV = 4
SSM_CHUNK = 128
SSM_CONV_DIM = SSM_INNER + 2 * SSM_GROUPS * SSM_STATE
ATT_HEAD_DIM = 64
ATT_Q_HEADS = BRANCH_WIDTH // ATT_HEAD_DIM
ATT_KV_HEADS = ATT_Q_HEADS // 8
ATT_WINDOW = 128
ATT_BLOCK = 128
IN_SPLITS = (
    2 * GM_WIDTH,
    HG_HEADS * HG_DK,
    HG_HEADS * HG_DK,
    HG_HEADS * HG_DV,
    HG_HEADS * HG_DV,
    SSM_INNER,
    SSM_CONV_DIM,
    SSM_HEADS,
    ATT_Q_HEADS * ATT_HEAD_DIM,
    ATT_KV_HEADS * ATT_HEAD_DIM,
    ATT_KV_HEADS * ATT_HEAD_DIM,
    N_BRANCH * D_MODEL,
)
D_IN = (2 * GM_WIDTH + 2 * HG_HEADS * HG_DK + 2 * HG_HEADS * HG_DV + SSM_INNER + SSM_CONV_DIM
        + SSM_HEADS + (ATT_Q_HEADS + 2 * ATT_KV_HEADS) * ATT_HEAD_DIM + N_BRANCH * D_MODEL)

kernel_name = "hybrid_gated_gmlp_hgrn2_ssd_swa_macaron"


def rms_norm(x, w, eps=NORM_EPS):
    xf = x.astype(jnp.float32)
    y = xf * lax.rsqrt(jnp.mean(xf * xf, axis=-1, keepdims=True) + eps)
    return (y * w.astype(jnp.float32)).astype(x.dtype)


def swiglu_ffn(h, w_up, w_down):
    a, b = jnp.split(h @ w_up, 2, axis=-1)
    return (jax.nn.silu(a) * b) @ w_down


def gmlp_spatial_gating(p, ln_w, ln_b, w_s, b_s):
    B, S, _ = p.shape
    u, v = jnp.split(jax.nn.gelu(p, approximate=False), 2, axis=-1)
    vf = v.astype(jnp.float32)
    mu = jnp.mean(vf, axis=-1, keepdims=True)
    var = jnp.mean(jnp.square(vf - mu), axis=-1, keepdims=True)
    vn = (vf - mu) * lax.rsqrt(var + LN_EPS) * ln_w.astype(jnp.float32) + ln_b.astype(jnp.float32)
    n = S // GM_CHUNK
    vc = vn.reshape(B, n, GM_CHUNK, GM_GROUPS, GM_WIDTH // GM_GROUPS)
    w = w_s.astype(jnp.float32) * jnp.tril(jnp.ones((GM_CHUNK, GM_CHUNK), jnp.float32))
    mixed = jnp.einsum('gts,bnsgc->bntgc', w, vc) + b_s.astype(jnp.float32).T[:, :, None]
    return (u * mixed.reshape(B, S, GM_WIDTH).astype(u.dtype)).astype(p.dtype)


def hgrn2_chunk_scan(q, k, v, log_f):
    B, S, H, DK = q.shape
    DV = v.shape[-1]
    C = HG_CHUNK
    n = S // C

    def chunks(t):
        return t.reshape(B, n, C, H, t.shape[-1]).transpose(1, 0, 3, 2, 4)

    qc, kc, vc = chunks(q), chunks(k), chunks(v)
    gc = jnp.cumsum(chunks(log_f), axis=3)
    causal = jnp.tril(jnp.ones((C, C), bool))[:, :, None]

    def step(state, inp):
        q_, k_, v_, g_ = inp
        o_inter = jnp.einsum('bhtk,bhkv->bhtv', q_ * jnp.exp(g_), state)
        rel = jnp.where(causal, g_[:, :, :, None, :] - g_[:, :, None, :, :], -jnp.inf)
        scores = jnp.einsum('bhtk,bhsk,bhtsk->bhts', q_, k_, jnp.exp(rel))
        o_intra = jnp.einsum('bhts,bhsv->bhtv', scores, v_)
        g_last = g_[:, :, -1:, :]
        state = (jnp.exp(g_last[:, :, 0, :, None]) * state
                 + jnp.einsum('bhsk,bhsv->bhkv', k_ * jnp.exp(g_last - g_), v_))
        return state, o_inter + o_intra

    state0 = jnp.zeros((B, H, DK, DV), jnp.float32)
    _, o = lax.scan(step, state0, (qc, kc, vc, gc))
    return o.transpose(1, 0, 3, 2, 4).reshape(B, S, H, DV)


def hgrn2_mixer(q_raw, f_raw, i_raw, g_raw, lb, norm_w):
    B, S, _ = q_raw.shape
    f32 = jnp.float32
    q = jax.nn.silu(q_raw.astype(f32)).reshape(B, S, HG_HEADS, HG_DK)
    lbf = lb.astype(f32)
    f = lbf + (1.0 - lbf) * jax.nn.sigmoid(f_raw.astype(f32))
    log_f = jnp.log(f).reshape(B, S, HG_HEADS, HG_DK)
    k = (1.0 - f).reshape(B, S, HG_HEADS, HG_DK)
    v = i_raw.astype(f32).reshape(B, S, HG_HEADS, HG_DV)
    o = hgrn2_chunk_scan(q, k, v, log_f)
    o = rms_norm(o, norm_w.reshape(HG_HEADS, HG_DV))
    o = o.reshape(B, S, HG_HEADS * HG_DV) * jax.nn.silu(g_raw.astype(f32))
    return o.astype(q_raw.dtype)


def segsum(a):
    T = a.shape[-1]
    cs = jnp.cumsum(a, axis=-1)
    diff = cs[..., :, None] - cs[..., None, :]
    return jnp.where(jnp.tril(jnp.ones((T, T), bool)), diff, -jnp.inf)


def mamba2_mixer(z, xbc, dt_raw, conv_w, conv_b, dt_bias, a_log, d_skip, norm_w):
    B, S, _ = xbc.shape
    f32 = jnp.float32
    G, R, P, N, L = SSM_GROUPS, SSM_HEADS // SSM_GROUPS, SSM_HEADDIM, SSM_STATE, SSM_CHUNK
    n = S // L
    xbc = lax.conv_general_dilated(
        xbc.astype(f32), conv_w.astype(f32)[:, None, :], window_strides=(1,),
        padding=[(SSM_CONV - 1, 0)], dimension_numbers=('NWC', 'WIO', 'NWC'),
        feature_group_count=SSM_CONV_DIM) + conv_b.astype(f32)
    xbc = jax.nn.silu(xbc)
    x_c, b_c, c_c = jnp.split(xbc, [SSM_INNER, SSM_INNER + G * N], axis=-1)
    x_c = x_c.reshape(B, S, SSM_HEADS, P)
    dt = jax.nn.softplus(dt_raw.astype(f32) + dt_bias.astype(f32))
    A = -jnp.exp(a_log.astype(f32))
    xs = (x_c * dt[..., None]).reshape(B, n, L, G, R, P)
    a = (dt * A).reshape(B, n, L, G, R).transpose(0, 3, 4, 1, 2)
    Bc = b_c.reshape(B, n, L, G, N)
    Cc = c_c.reshape(B, n, L, G, N)
    a_cs = jnp.cumsum(a, axis=-1)
    decay_in = jnp.exp(segsum(a))
    cb = jnp.einsum('bclgn,bcsgn->bcgls', Cc, Bc)
    y_diag = jnp.einsum('bcgls,bgrcls,bcsgrp->bclgrp', cb, decay_in, xs)
    decay_states = jnp.exp(a_cs[..., -1:] - a_cs)
    states = jnp.einsum('bcsgn,bgrcs,bcsgrp->bcgrpn', Bc, decay_states, xs)
    chunk_decay = jnp.exp(a_cs[..., -1])

    def step(carry, inp):
        s_c, d_c = inp
        return d_c[..., None, None] * carry + s_c, carry

    init = jnp.zeros((B, G, R, P, N), f32)
    _, prev = lax.scan(step, init, (jnp.moveaxis(states, 1, 0), jnp.moveaxis(chunk_decay, -1, 0)))
    prev = jnp.moveaxis(prev, 0, 1)
    y_off = jnp.einsum('bclgn,bcgrpn,bgrcl->bclgrp', Cc, prev, jnp.exp(a_cs))
    y = (y_diag + y_off).reshape(B, S, SSM_HEADS, P) + x_c * d_skip.astype(f32)[:, None]
    y = y.reshape(B, S, SSM_INNER) * jax.nn.silu(z.astype(f32))
    y = rms_norm(y.reshape(B, S, G, SSM_INNER // G), norm_w.reshape(G, SSM_INNER // G))
    return y.reshape(B, S, SSM_INNER).astype(z.dtype)


def swa_sink_attention(q_raw, k_raw, v_raw, q_norm_w, k_norm_w, sinks):
    B, S, _ = q_raw.shape
    T, KVH, HD = ATT_BLOCK, ATT_KV_HEADS, ATT_HEAD_DIM
    R = ATT_Q_HEADS // KVH
    n = S // T
    q = rms_norm(q_raw.reshape(B, S, ATT_Q_HEADS, HD), q_norm_w).reshape(B, n, T, KVH, R, HD)
    k = rms_norm(k_raw.reshape(B, S, KVH, HD), k_norm_w).reshape(B, n, T, KVH, HD)
    v = v_raw.reshape(B, n, T, KVH, HD)

    def with_prev(t):
        prev = jnp.concatenate([jnp.zeros_like(t[:, :1]), t[:, :-1]], axis=1)
        return jnp.concatenate([prev, t], axis=2)

    kk, vv = with_prev(k), with_prev(v)
    scores = jnp.einsum('bnqhrd,bnkhd->bnhrqk', q, kk).astype(jnp.float32) * (HD ** -0.5)
    blk = jnp.arange(n)[:, None, None]
    qpos = blk * T + jnp.arange(T)[None, :, None]
    kpos = blk * T - T + jnp.arange(2 * T)[None, None, :]
    rel = qpos - kpos
    valid = (rel >= 0) & (rel < ATT_WINDOW) & (kpos >= 0)
    scores = jnp.where(valid[None, :, None, None], scores, -jnp.inf)
    sink = sinks.astype(jnp.float32).reshape(KVH, R)[None, None, :, :, None, None]
    m = jnp.maximum(jnp.max(scores, axis=-1, keepdims=True), sink)
    p = jnp.exp(scores - m)
    probs = p / (jnp.sum(p, axis=-1, keepdims=True) + jnp.exp(sink - m))
    out = jnp.einsum('bnhrqk,bnkhd->bnqhrd', probs.astype(v.dtype), vv)
    return out.reshape(B, S, ATT_Q_HEADS * HD)


def hybrid_mixer(h, w_in, gm_ln_w, gm_ln_b, gm_w_s, gm_b_s, hg_lb, hg_norm,
                 ssm_conv_w, ssm_conv_b, ssm_dt_bias, ssm_a_log, ssm_d, ssm_norm,
                 att_q_norm, att_k_norm, att_sinks, w_branch, w_out):
    p = h @ w_in
    split_points = np.cumsum(IN_SPLITS)[:-1].tolist()
    (gm_in, hg_q, hg_f, hg_i, hg_g, ssm_z, ssm_xbc, ssm_dt,
     att_q, att_k, att_v, gate_logits) = jnp.split(p, split_points, axis=-1)
    branches = (
        gmlp_spatial_gating(gm_in, gm_ln_w, gm_ln_b, gm_w_s, gm_b_s),
        hgrn2_mixer(hg_q, hg_f, hg_i, hg_g, hg_lb, hg_norm),
        mamba2_mixer(ssm_z, ssm_xbc, ssm_dt, ssm_conv_w, ssm_conv_b, ssm_dt_bias, ssm_a_log, ssm_d, ssm_norm),
        swa_sink_attention(att_q, att_k, att_v, att_q_norm, att_k_norm, att_sinks),
    )
    gates = jnp.split(gate_logits, N_BRANCH, axis=-1)
    merged = jax.nn.sigmoid(gates[0]) * (branches[0].astype(h.dtype) @ w_branch[0])
    for b in range(1, N_BRANCH):
        merged = merged + jax.nn.sigmoid(gates[b]) * (branches[b].astype(h.dtype) @ w_branch[b])
    return merged @ w_out


def setup_inputs(seed: int = 0) -> dict:
    key = jax.random.key(seed)
    ks = iter(jax.random.split(key, 40))
    L, D = DEPTH, D_MODEL
    f32 = jnp.float32

    def normal(shape, scale):
        return jax.random.normal(next(ks), shape, f32) * scale

    def gain(shape):
        return 1.0 + normal(shape, 0.02)

    x = normal((BATCH, SEQ, D), 1.0)
    ffn1_norm = gain((L, D))
    ffn1_up = normal((L, D, 2 * D_FF), D ** -0.5)
    ffn1_down = normal((L, D_FF, D), D_FF ** -0.5)
    mix_norm = gain((L, D))
    w_in = normal((L, D, D_IN), D ** -0.5)
    gm_ln_w = gain((L, GM_WIDTH))
    gm_ln_b = normal((L, GM_WIDTH), 0.02)
    gm_w_s = normal((L, GM_GROUPS, GM_CHUNK, GM_CHUNK), GM_CHUNK ** -0.5)
    gm_b_s = 1.0 + normal((L, GM_GROUPS, GM_CHUNK), 0.02)
    hg_lb_table = normal((L, HG_HEADS * HG_DK), 0.1)
    hg_norm = gain((L, HG_HEADS * HG_DV))
    ssm_conv_w = normal((L, SSM_CONV, SSM_CONV_DIM), SSM_CONV ** -0.5)
    ssm_conv_b = normal((L, SSM_CONV_DIM), 0.02)
    u = jax.random.uniform(next(ks), (L, SSM_HEADS), f32)
    dt0 = jnp.exp(u * (jnp.log(0.1) - jnp.log(0.001)) + jnp.log(0.001))
    ssm_dt_bias = dt0 + jnp.log(-jnp.expm1(-dt0))
    ssm_a_log = jnp.log(jax.random.uniform(next(ks), (L, SSM_HEADS), f32, 1.0, 16.0))
    ssm_d = 1.0 + normal((L, SSM_HEADS), 0.1)
    ssm_norm = gain((L, SSM_INNER))
    att_q_norm = gain((L, ATT_HEAD_DIM))
    att_k_norm = gain((L, ATT_HEAD_DIM))
    att_sinks = normal((L, ATT_Q_HEADS), 0.5)
    w_branch = normal((L, N_BRANCH, BRANCH_WIDTH, D), BRANCH_WIDTH ** -0.5)
    w_out = normal((L, D, D), D ** -0.5)
    ffn2_norm = gain((L, D))
    ffn2_up = normal((L, D, 2 * D_FF), D ** -0.5)
    ffn2_down = normal((L, D_FF, D), D_FF ** -0.5)
    return {
        "x": x, "ffn1_norm": ffn1_norm, "ffn1_up": ffn1_up, "ffn1_down": ffn1_down,
        "mix_norm": mix_norm, "w_in": w_in, "gm_ln_w": gm_ln_w, "gm_ln_b": gm_ln_b,
        "gm_w_s": gm_w_s, "gm_b_s": gm_b_s, "hg_lb_table": hg_lb_table, "hg_norm": hg_norm,
        "ssm_conv_w": ssm_conv_w, "ssm_conv_b": ssm_conv_b, "ssm_dt_bias": ssm_dt_bias,
        "ssm_a_log": ssm_a_log, "ssm_d": ssm_d, "ssm_norm": ssm_norm,
        "att_q_norm": att_q_norm, "att_k_norm": att_k_norm, "att_sinks": att_sinks,
        "w_branch": w_branch, "w_out": w_out,
        "ffn2_norm": ffn2_norm, "ffn2_up": ffn2_up, "ffn2_down": ffn2_down,
    }


def reference(x, ffn1_norm, ffn1_up, ffn1_down, mix_norm, w_in, gm_ln_w, gm_ln_b, gm_w_s, gm_b_s,
              hg_lb_table, hg_norm, ssm_conv_w, ssm_conv_b, ssm_dt_bias, ssm_a_log, ssm_d, ssm_norm,
              att_q_norm, att_k_norm, att_sinks, w_branch, w_out, ffn2_norm, ffn2_up, ffn2_down):
    lb_p = jax.nn.softmax(hg_lb_table.astype(jnp.float32), axis=0)
    lower_bounds = jnp.cumsum(lb_p, axis=0) - lb_p[0:1]
    for l in range(DEPTH):
        x = x + 0.5 * swiglu_ffn(rms_norm(x, ffn1_norm[l]), ffn1_up[l], ffn1_down[l])
        x = x + hybrid_mixer(rms_norm(x, mix_norm[l]), w_in[l], gm_ln_w[l], gm_ln_b[l], gm_w_s[l], gm_b_s[l],
                             lower_bounds[l], hg_norm[l], ssm_conv_w[l], ssm_conv_b[l], ssm_dt_bias[l],
                             ssm_a_log[l], ssm_d[l], ssm_norm[l], att_q_norm[l], att_k_norm[l], att_sinks[l],
                             w_branch[l], w_out[l]).astype(x.dtype)
        x = x + 0.5 * swiglu_ffn(rms_norm(x, ffn2_norm[l]), ffn2_up[l], ffn2_down[l])
    return x
```

```python
import functools

import jax
import jax.numpy as jnp
from jax import lax
from jax.experimental import pallas as pl
from jax.experimental.pallas import tpu as pltpu

F32 = jnp.float32
BF16 = jnp.bfloat16
HIGHEST = lax.Precision.HIGHEST

NORM_EPS = 1e-6
LN_EPS = 1e-5

V7X_VMEM_BYTES = 64 * 1024 * 1024
VMEM_LIMIT_BYTES = V7X_VMEM_BYTES - 8 * 1024 * 1024
LANES = 128

BRANCH_WIDTH = 1024
N_BRANCH = 4
GM_GROUPS = 8
GM_CHUNK = 128
HG_HEADS = 8
HG_DK = 128
HG_CHUNK = 64
HG_SUB = 16
SSM_HEADS = 16
SSM_HEADDIM = 64
SSM_GROUPS = 4
SSM_STATE = 128
SSM_CONV = 4
SSM_CHUNK = 128
SSM_TAIL = 8
ATT_HEAD_DIM = 64
ATT_Q_HEADS = 16
ATT_KV_HEADS = 2
ATT_BLOCK = 128

COL_GATES = 0
COL_GM = 16384
COL_HG_Q = 18432
COL_HG_F = 19456
COL_HG_I = 20480
COL_HG_G = 21504
COL_SSM_Z = 22528
COL_SSM_X = 23552
COL_SSM_BC = 24576
COL_ATT_Q = 25600
COL_ATT_K = 26624
COL_ATT_V = 26752
COL_SSM_DT = 26880
D_IN_PADDED = 27648

D_FF = 11008
D_FF_PADDED = 11264


def _params(semantics):
    return pltpu.CompilerParams(dimension_semantics=semantics, vmem_limit_bytes=VMEM_LIMIT_BYTES)


def _silu(x):
    return x * jax.nn.sigmoid(x)


def _gelu_exact(x):
    return 0.5 * x * (1.0 + lax.erf(x * (2.0 ** -0.5)))


def _rmsnorm_kernel(x_ref, w_ref, o_ref):
    x = x_ref[...]
    ms = jnp.mean(x * x, axis=-1, keepdims=True)
    o_ref[...] = (x * lax.rsqrt(ms + NORM_EPS) * w_ref[...]).astype(o_ref.dtype)


def _rmsnorm(x, w):
    m, d = x.shape
    bm = 256
    return pl.pallas_call(
        _rmsnorm_kernel,
        grid=(m // bm,),
        in_specs=[pl.BlockSpec((bm, d), lambda i: (i, 0)),
                  pl.BlockSpec((1, d), lambda i: (0, 0))],
        out_specs=pl.BlockSpec((bm, d), lambda i: (i, 0)),
        out_shape=jax.ShapeDtypeStruct((m, d), BF16),
        compiler_params=_params(("parallel",)),
        name="rmsnorm",
    )(x, w.reshape(1, d))


def _up_swiglu_kernel(x_ref, wa_ref, wb_ref, o_ref):
    x = x_ref[...]
    a = jnp.dot(x, wa_ref[...], preferred_element_type=F32)
    b = jnp.dot(x, wb_ref[...], preferred_element_type=F32)
    o_ref[...] = (_silu(a) * b).astype(o_ref.dtype)


def _up_swiglu(h, wa, wb):
    m, k = h.shape
    n = wa.shape[1]
    bm, bn = 1024, 512
    return pl.pallas_call(
        _up_swiglu_kernel,
        grid=(m // bm, n // bn),
        in_specs=[pl.BlockSpec((bm, k), lambda i, j: (i, 0)),
                  pl.BlockSpec((k, bn), lambda i, j: (0, j)),
                  pl.BlockSpec((k, bn), lambda i, j: (0, j))],
        out_specs=pl.BlockSpec((bm, bn), lambda i, j: (i, j)),
        out_shape=jax.ShapeDtypeStruct((m, n), BF16),
        compiler_params=_params(("parallel", "arbitrary")),
        name="ffn_up_swiglu",
    )(h, wa, wb)


def _down_residual_kernel(a_ref, w_ref, res_ref, o_ref, acc_ref, *, nk):
    k = pl.program_id(2)
    d = jnp.dot(a_ref[...], w_ref[...], preferred_element_type=F32)

    @pl.when(k == 0)
    def _():
        acc_ref[...] = d

    @pl.when(k > 0)
    def _():
        acc_ref[...] += d

    @pl.when(k == nk - 1)
    def _():
        o_ref[...] = res_ref[...] + 0.5 * acc_ref[...]


def _down_residual(act, w, res):
    m, k = act.shape
    n = w.shape[1]
    bm, bn, nk = 1024, 1024, 4
    bk = k // nk
    return pl.pallas_call(
        functools.partial(_down_residual_kernel, nk=nk),
        grid=(m // bm, n // bn, nk),
        in_specs=[pl.BlockSpec((bm, bk), lambda i, j, kk: (i, kk)),
                  pl.BlockSpec((bk, bn), lambda i, j, kk: (kk, j)),
                  pl.BlockSpec((bm, bn), lambda i, j, kk: (i, j))],
        out_specs=pl.BlockSpec((bm, bn), lambda i, j, kk: (i, j)),
        out_shape=jax.ShapeDtypeStruct((m, n), F32),
        scratch_shapes=[pltpu.VMEM((bm, bn), F32)],
        compiler_params=_params(("parallel", "parallel", "arbitrary")),
        name="ffn_down_residual",
    )(act, w, res)


def _matmul_kernel(x_ref, w_ref, o_ref):
    o_ref[...] = jnp.dot(x_ref[...], w_ref[...], preferred_element_type=F32).astype(o_ref.dtype)


def _in_projection(h, w):
    m, k = h.shape
    n = w.shape[1]
    bm, bn = 1024, 1024
    return pl.pallas_call(
        _matmul_kernel,
        grid=(m // bm, n // bn),
        in_specs=[pl.BlockSpec((bm, k), lambda i, j: (i, 0)),
                  pl.BlockSpec((k, bn), lambda i, j: (0, j))],
        out_specs=pl.BlockSpec((bm, bn), lambda i, j: (i, j)),
        out_shape=jax.ShapeDtypeStruct((m, n), F32),
        compiler_params=_params(("parallel", "arbitrary")),
        name="mixer_in_projection",
    )(h, w)


def _out_residual_kernel(x_ref, w_ref, res_ref, o_ref):
    o_ref[...] = res_ref[...] + jnp.dot(x_ref[...], w_ref[...], preferred_element_type=F32)


def _out_residual(merged, w, res):
    m, k = merged.shape
    n = w.shape[1]
    bm, bn = 1024, 512
    return pl.pallas_call(
        _out_residual_kernel,
        grid=(m // bm, n // bn),
        in_specs=[pl.BlockSpec((bm, k), lambda i, j: (i, 0)),
                  pl.BlockSpec((k, bn), lambda i, j: (0, j)),
                  pl.BlockSpec((bm, bn), lambda i, j: (i, j))],
        out_specs=pl.BlockSpec((bm, bn), lambda i, j: (i, j)),
        out_shape=jax.ShapeDtypeStruct((m, n), F32),
        compiler_params=_params(("parallel", "arbitrary")),
        name="mixer_out_residual",
    )(merged, w, res)


def _merge_kernel(b0_ref, b1_ref, b2_ref, b3_ref, w_ref, g0_ref, g1_ref, g2_ref, g3_ref, o_ref):
    branches = (b0_ref, b1_ref, b2_ref, b3_ref)
    gates = (g0_ref, g1_ref, g2_ref, g3_ref)
    merged = None
    for b in range(N_BRANCH):
        y = jnp.dot(branches[b][...], w_ref[b], preferred_element_type=F32)
        term = jax.nn.sigmoid(gates[b][...]) * y
        merged = term if merged is None else merged + term
    o_ref[...] = merged.astype(o_ref.dtype)


def _merge(branches, w_branch, p):
    m = p.shape[0]
    nb, kb, n = w_branch.shape
    bm, bn = 512, 1024
    gate_blocks = n // bn

    def gate_spec(b):
        return pl.BlockSpec((bm, bn), lambda j, i: (i, COL_GATES // bn + b * gate_blocks + j))

    return pl.pallas_call(
        _merge_kernel,
        grid=(n // bn, m // bm),
        in_specs=[pl.BlockSpec((bm, kb), lambda j, i: (i, 0)) for _ in range(nb)]
        + [pl.BlockSpec((nb, kb, bn), lambda j, i: (0, 0, j))]
        + [gate_spec(b) for b in range(nb)],
        out_specs=pl.BlockSpec((bm, bn), lambda j, i: (i, j)),
        out_shape=jax.ShapeDtypeStruct((m, n), BF16),
        compiler_params=_params(("parallel", "arbitrary")),
        name="branch_merge",
    )(*branches, w_branch, p, p, p, p)


def _gmlp_kernel(p_ref, lnw_ref, lnb_ref, ws_ref, bst_ref, o_ref, *, chunks):
    t = GM_CHUNK
    width = BRANCH_WIDTH
    gw = width // GM_GROUPS
    row = lax.broadcasted_iota(jnp.int32, (t, t), 0)
    col = lax.broadcasted_iota(jnp.int32, (t, t), 1)
    causal = row >= col
    mixers = [jnp.where(causal, ws_ref[g], 0.0) for g in range(GM_GROUPS)]
    for c in range(chunks):
        rows = slice(c * t, (c + 1) * t)
        act = _gelu_exact(p_ref[rows, :])
        u = act[:, :width]
        v = act[:, width:]
        mu = jnp.mean(v, axis=-1, keepdims=True)
        var = jnp.mean(jnp.square(v - mu), axis=-1, keepdims=True)
        vn = (v - mu) * lax.rsqrt(var + LN_EPS) * lnw_ref[...] + lnb_ref[...]
        for g in range(GM_GROUPS):
            cols = slice(g * gw, (g + 1) * gw)
            mixed = jnp.dot(mixers[g], vn[:, cols], preferred_element_type=F32) + bst_ref[:, g:g + 1]
            o_ref[rows, cols] = (u[:, cols] * mixed).astype(o_ref.dtype)


def _gmlp_branch(p, ln_w, ln_b, w_s, b_s):
    m = p.shape[0]
    tb = 512
    width = BRANCH_WIDTH
    return pl.pallas_call(
        functools.partial(_gmlp_kernel, chunks=tb // GM_CHUNK),
        grid=(m // tb,),
        in_specs=[pl.BlockSpec((tb, 2 * width), lambda i: (i, COL_GM // (2 * width))),
                  pl.BlockSpec((1, width), lambda i: (0, 0)),
                  pl.BlockSpec((1, width), lambda i: (0, 0)),
                  pl.BlockSpec((GM_GROUPS, GM_CHUNK, GM_CHUNK), lambda i: (0, 0, 0)),
                  pl.BlockSpec((GM_CHUNK, GM_GROUPS), lambda i: (0, 0))],
        out_specs=pl.BlockSpec((tb, width), lambda i: (i, 0)),
        out_shape=jax.ShapeDtypeStruct((m, width), BF16),
        compiler_params=_params(("parallel",)),
        name="gmlp_branch",
    )(p, ln_w.reshape(1, width), ln_b.reshape(1, width), w_s, b_s.T)


def _hgrn2_kernel(q_ref, f_ref, i_ref, g_ref, lbt_ref, nw_ref, o_ref, state_ref, *, layer, chunks):
    c_len = HG_CHUNK

    @pl.when(pl.program_id(2) == 0)
    def _():
        state_ref[...] = jnp.zeros_like(state_ref)

    table = lbt_ref[...]
    e = jnp.exp(table - jnp.max(table, axis=0, keepdims=True))
    probs = e / jnp.sum(e, axis=0, keepdims=True)
    lb = jnp.sum(probs[:layer + 1], axis=0, keepdims=True) - probs[0:1]

    row = lax.broadcasted_iota(jnp.int32, (c_len, c_len), 0)
    col = lax.broadcasted_iota(jnp.int32, (c_len, c_len), 1)
    cumsum_mat = (row >= col).astype(F32)
    sub_row = lax.broadcasted_iota(jnp.int32, (HG_SUB, HG_DK), 0)
    nt_dims = (((1,), (1,)), ((), ()))
    tn_dims = (((0,), (0,)), ((), ()))

    def chunk_body(c, carry):
        r0 = pl.multiple_of(c * c_len, c_len)
        rows = pl.ds(r0, c_len)
        q = _silu(q_ref[rows, :])
        f = lb + (1.0 - lb) * jax.nn.sigmoid(f_ref[rows, :])
        log_f = jnp.log(f)
        k = 1.0 - f
        v = i_ref[rows, :]
        g = jnp.dot(cumsum_mat, log_f, precision=HIGHEST, preferred_element_type=F32)
        state_t = state_ref[...]
        o_inter = lax.dot_general(q * jnp.exp(g), state_t, nt_dims, preferred_element_type=F32)
        outs = []
        for blk in range(c_len // HG_SUB):
            lo = blk * HG_SUB
            g_b, q_b, k_b, v_b = g[lo:lo + HG_SUB], q[lo:lo + HG_SUB], k[lo:lo + HG_SUB], v[lo:lo + HG_SUB]
            o_b = o_inter[lo:lo + HG_SUB]
            if blk > 0:
                ref_g = g[lo - 1:lo]
                q_s = q_b * jnp.exp(g_b - ref_g)
                k_s = k[:lo] * jnp.exp(ref_g - g[:lo])
                scores = lax.dot_general(q_s, k_s, nt_dims, preferred_element_type=F32)
                o_b = o_b + jnp.dot(scores, v[:lo], preferred_element_type=F32)
            for s in range(HG_SUB):
                decay = jnp.exp(jnp.where(sub_row >= s, g_b - g_b[s:s + 1], -jnp.inf))
                score_col = jnp.sum(q_b * k_b[s:s + 1] * decay, axis=-1, keepdims=True)
                o_b = o_b + score_col * v_b[s:s + 1]
            outs.append(o_b)
        o = jnp.concatenate(outs, axis=0)
        g_last = g[c_len - 1:c_len]
        k_s = k * jnp.exp(g_last - g)
        state_ref[...] = state_t * jnp.exp(g_last) + lax.dot_general(v, k_s, tn_dims, preferred_element_type=F32)
        ms = jnp.mean(o * o, axis=-1, keepdims=True)
        y = o * lax.rsqrt(ms + NORM_EPS) * nw_ref[...]
        o_ref[rows, :] = (y * _silu(g_ref[rows, :])).astype(o_ref.dtype)
        return carry

    lax.fori_loop(0, chunks, chunk_body, 0)


def _hgrn2_branch(p, lb_table, norm_w, layer, batch, seq):
    m = p.shape[0]
    ts = 512
    steps = seq // ts
    depth = lb_table.shape[0]

    def in_spec(col0):
        return pl.BlockSpec((ts, HG_DK), lambda b, h, t: (b * steps + t, col0 // HG_DK + h))

    return pl.pallas_call(
        functools.partial(_hgrn2_kernel, layer=layer, chunks=ts // HG_CHUNK),
        grid=(batch, HG_HEADS, steps),
        in_specs=[in_spec(COL_HG_Q), in_spec(COL_HG_F), in_spec(COL_HG_I), in_spec(COL_HG_G),
                  pl.BlockSpec((depth, HG_DK), lambda b, h, t: (0, h)),
                  pl.BlockSpec((1, HG_DK), lambda b, h, t: (0, h))],
        out_specs=pl.BlockSpec((ts, HG_DK), lambda b, h, t: (b * steps + t, h)),
        out_shape=jax.ShapeDtypeStruct((m, BRANCH_WIDTH), BF16),
        scratch_shapes=[pltpu.VMEM((HG_DK, HG_DK), F32)],
        compiler_params=_params(("parallel", "parallel", "arbitrary")),
        name="hgrn2_branch",
    )(p, p, p, p, lb_table, norm_w.reshape(1, BRANCH_WIDTH))


def _ssd_kernel(z_ref, x_ref, bc_ref, dt_ref, cw_ref, cb_ref, dtb_ref, alog_ref, dsk_ref, nw_ref,
                expand_ref, o_ref, tail_ref, state_ref):
    length = SSM_CHUNK
    inner = BRANCH_WIDTH
    gw = inner // SSM_GROUPS
    n = SSM_STATE

    @pl.when(pl.program_id(1) == 0)
    def _():
        tail_ref[...] = jnp.zeros_like(tail_ref)
        state_ref[...] = jnp.zeros_like(state_ref)

    xbc = jnp.concatenate([x_ref[...], bc_ref[...]], axis=1)
    ext = jnp.concatenate([tail_ref[...], xbc], axis=0)
    conv = cb_ref[...]
    for j in range(SSM_CONV):
        start = SSM_TAIL - (SSM_CONV - 1) + j
        conv = conv + ext[start:start + length] * cw_ref[j:j + 1, :]
    tail_ref[...] = xbc[length - SSM_TAIL:length]
    xbc_act = _silu(conv)
    x_c = xbc_act[:, :inner]
    b_m = xbc_act[:, inner:inner + SSM_GROUPS * n]
    c_m = xbc_act[:, inner + SSM_GROUPS * n:]

    dt = jax.nn.softplus(dt_ref[...] + dtb_ref[...])
    a = dt * (-jnp.exp(alog_ref[...]))
    row = lax.broadcasted_iota(jnp.int32, (length, length), 0)
    col = lax.broadcasted_iota(jnp.int32, (length, length), 1)
    causal = row >= col
    cs = jnp.dot(causal.astype(F32), a, precision=HIGHEST, preferred_element_type=F32)
    cs_t = cs.T
    expand = expand_ref[...]
    dt_e = jnp.dot(dt, expand, precision=HIGHEST, preferred_element_type=F32)
    cs_e = jnp.dot(cs, expand, precision=HIGHEST, preferred_element_type=F32)
    cs_last_e = cs_e[length - 1:length]
    xs = x_c * dt_e
    in_decay_e = jnp.exp(cs_e)
    state_decay_e = jnp.exp(cs_last_e - cs_e)
    chunk_decay_e = jnp.exp(cs_last_e)
    z = z_ref[...]

    nt_dims = (((1,), (1,)), ((), ()))
    tn_dims = (((0,), (0,)), ((), ()))
    heads_per_group = SSM_HEADS // SSM_GROUPS
    p = SSM_HEADDIM
    for g in range(SSM_GROUPS):
        gcols = slice(g * gw, (g + 1) * gw)
        b_g = b_m[:, g * n:(g + 1) * n]
        c_g = c_m[:, g * n:(g + 1) * n]
        cb = lax.dot_general(c_g, b_g, nt_dims, preferred_element_type=F32)
        prev_t = state_ref[g]
        y_off = jnp.dot(c_g, prev_t, preferred_element_type=F32) * in_decay_e[:, gcols]
        parts = []
        for r in range(heads_per_group):
            h = g * heads_per_group + r
            decay = jnp.exp(jnp.where(causal, cs[:, h:h + 1] - cs_t[h:h + 1, :], -jnp.inf))
            parts.append(jnp.dot(cb * decay, xs[:, h * p:(h + 1) * p], preferred_element_type=F32))
        y_diag = jnp.concatenate(parts, axis=1)
        state_ref[g] = prev_t * chunk_decay_e[:, gcols] + lax.dot_general(
            b_g, xs[:, gcols] * state_decay_e[:, gcols], tn_dims, preferred_element_type=F32)
        y = (y_diag + y_off) + x_c[:, gcols] * dsk_ref[:, gcols]
        y = y * _silu(z[:, gcols])
        ms = jnp.mean(y * y, axis=-1, keepdims=True)
        o_ref[:, gcols] = (y * lax.rsqrt(ms + NORM_EPS) * nw_ref[:, gcols]).astype(o_ref.dtype)


def _ssd_branch(p, conv_w, conv_b, dt_bias, a_log, d_skip, norm_w, batch, seq):
    m = p.shape[0]
    length = SSM_CHUNK
    steps = seq // length
    inner = BRANCH_WIDTH
    conv_dim = conv_w.shape[1]

    def pad_heads(vec):
        return jnp.pad(vec, (0, LANES - SSM_HEADS)).reshape(1, LANES)

    expand = (jnp.arange(LANES)[:, None] == (jnp.arange(inner) // SSM_HEADDIM)[None, :]).astype(F32)
    d_skip_e = jnp.repeat(d_skip, SSM_HEADDIM).reshape(1, inner)

    def rows(width, col0):
        return pl.BlockSpec((length, width), lambda b, t: (b * steps + t, col0 // width))

    def whole(shape):
        return pl.BlockSpec(shape, lambda b, t: (0,) * len(shape))

    return pl.pallas_call(
        _ssd_kernel,
        grid=(batch, steps),
        in_specs=[rows(inner, COL_SSM_Z), rows(inner, COL_SSM_X), rows(inner, COL_SSM_BC),
                  rows(LANES, COL_SSM_DT),
                  whole((SSM_CONV, conv_dim)), whole((1, conv_dim)),
                  whole((1, LANES)), whole((1, LANES)), whole((1, inner)), whole((1, inner)),
                  whole((LANES, inner))],
        out_specs=pl.BlockSpec((length, inner), lambda b, t: (b * steps + t, 0)),
        out_shape=jax.ShapeDtypeStruct((m, inner), BF16),
        scratch_shapes=[pltpu.VMEM((SSM_TAIL, conv_dim), F32),
                        pltpu.VMEM((SSM_GROUPS, SSM_STATE, inner // SSM_GROUPS), F32)],
        compiler_params=_params(("parallel", "arbitrary")),
        name="ssd_branch",
    )(p, p, p, p, conv_w, conv_b.reshape(1, conv_dim), pad_heads(dt_bias), pad_heads(a_log),
      d_skip_e, norm_w.reshape(1, inner), expand)


def _swa_kernel(sink_ref, q_ref, kc_ref, kp_ref, vc_ref, vp_ref, qw_ref, kw_ref, o_ref):
    t = ATT_BLOCK
    hd = ATT_HEAD_DIM
    group = ATT_Q_HEADS // ATT_KV_HEADS
    blk = pl.program_id(1)
    q = q_ref[...]
    keys = jnp.concatenate([kp_ref[...], kc_ref[...]], axis=0)
    vals = jnp.concatenate([vp_ref[...], vc_ref[...]], axis=0)
    qi = lax.broadcasted_iota(jnp.int32, (t, 2 * t), 0)
    kj = lax.broadcasted_iota(jnp.int32, (t, 2 * t), 1)
    first_key = jnp.where(blk > 0, 0, t)
    valid = (kj > qi) & (kj <= qi + t) & (kj >= first_key)
    nt_dims = (((1,), (1,)), ((), ()))
    scale = hd ** -0.5
    outs = []
    for c in range(ATT_KV_HEADS):
        k_h = keys[:, c * hd:(c + 1) * hd]
        k_n = k_h * lax.rsqrt(jnp.mean(k_h * k_h, axis=-1, keepdims=True) + NORM_EPS) * kw_ref[...]
        v_h = vals[:, c * hd:(c + 1) * hd]
        for r in range(group):
            h = c * group + r
            q_h = q[:, h * hd:(h + 1) * hd]
            q_n = q_h * lax.rsqrt(jnp.mean(q_h * q_h, axis=-1, keepdims=True) + NORM_EPS) * qw_ref[...]
            s = lax.dot_general(q_n, k_n, nt_dims, preferred_element_type=F32) * scale
            s = jnp.where(valid, s, -jnp.inf)
            sink = sink_ref[h]
            mx = jnp.maximum(jnp.max(s, axis=-1, keepdims=True), sink)
            e = jnp.exp(s - mx)
            probs = e / (jnp.sum(e, axis=-1, keepdims=True) + jnp.exp(sink - mx))
            outs.append(jnp.dot(probs, v_h, preferred_element_type=F32))
    o_ref[...] = jnp.concatenate(outs, axis=1).astype(o_ref.dtype)


def _swa_branch(p, q_norm_w, k_norm_w, sinks, batch, seq):
    m = p.shape[0]
    t = ATT_BLOCK
    steps = seq // t
    qw = ATT_Q_HEADS * ATT_HEAD_DIM
    kvw = ATT_KV_HEADS * ATT_HEAD_DIM

    def cur(width, col0):
        return pl.BlockSpec((t, width), lambda b, i: (b * steps + i, col0 // width))

    def prev(width, col0):
        return pl.BlockSpec((t, width), lambda b, i: (b * steps + jnp.maximum(i - 1, 0), col0 // width))

    return pl.pallas_call(
        _swa_kernel,
        grid=(batch, steps),
        in_specs=[pl.BlockSpec(memory_space=pltpu.SMEM),
                  cur(qw, COL_ATT_Q), cur(kvw, COL_ATT_K), prev(kvw, COL_ATT_K),
                  cur(kvw, COL_ATT_V), prev(kvw, COL_ATT_V),
                  pl.BlockSpec((1, ATT_HEAD_DIM), lambda b, i: (0, 0)),
                  pl.BlockSpec((1, ATT_HEAD_DIM), lambda b, i: (0, 0))],
        out_specs=pl.BlockSpec((t, qw), lambda b, i: (b * steps + i, 0)),
        out_shape=jax.ShapeDtypeStruct((m, qw), BF16),
        compiler_params=_params(("parallel", "parallel")),
        name="swa_branch",
    )(sinks, p, p, p, p, p, q_norm_w.reshape(1, ATT_HEAD_DIM), k_norm_w.reshape(1, ATT_HEAD_DIM))


def _prep_ffn(w_up, w_down):
    pad = D_FF_PADDED - D_FF
    wa = jnp.pad(w_up[:, :D_FF].astype(BF16), ((0, 0), (0, pad)))
    wb = jnp.pad(w_up[:, D_FF:].astype(BF16), ((0, 0), (0, pad)))
    wd = jnp.pad(w_down.astype(BF16), ((0, pad), (0, 0)))
    return wa, wb, wd


def _prep_w_in(w_in):
    d = w_in.shape[0]
    main = 9216
    dt_end = main + SSM_HEADS
    att_end = dt_end + (ATT_Q_HEADS + 2 * ATT_KV_HEADS) * ATT_HEAD_DIM
    used = COL_SSM_DT + SSM_HEADS
    parts = [w_in[:, att_end:], w_in[:, :main], w_in[:, dt_end:att_end], w_in[:, main:dt_end],
             jnp.zeros((d, D_IN_PADDED - used), w_in.dtype)]
    return jnp.concatenate(parts, axis=1).astype(BF16)


def _ffn(x, norm_w, w_up, w_down):
    wa, wb, wd = _prep_ffn(w_up, w_down)
    h = _rmsnorm(x, norm_w)
    act = _up_swiglu(h, wa, wb)
    return _down_residual(act, wd, x)


def _mixer(x, layer, batch, seq, mix_norm, w_in, gm_ln_w, gm_ln_b, gm_w_s, gm_b_s, hg_lb_table, hg_norm,
           ssm_conv_w, ssm_conv_b, ssm_dt_bias, ssm_a_log, ssm_d, ssm_norm,
           att_q_norm, att_k_norm, att_sinks, w_branch, w_out):
    h = _rmsnorm(x, mix_norm)
    p = _in_projection(h, _prep_w_in(w_in))
    branches = (
        _gmlp_branch(p, gm_ln_w, gm_ln_b, gm_w_s, gm_b_s),
        _hgrn2_branch(p, hg_lb_table, hg_norm, layer, batch, seq),
        _ssd_branch(p, ssm_conv_w, ssm_conv_b, ssm_dt_bias, ssm_a_log, ssm_d, ssm_norm, batch, seq),
        _swa_branch(p, att_q_norm, att_k_norm, att_sinks, batch, seq),
    )
    merged = _merge(branches, w_branch.astype(BF16), p)
    return _out_residual(merged, w_out.astype(BF16), x)


def kernel(x, ffn1_norm, ffn1_up, ffn1_down, mix_norm, w_in, gm_ln_w, gm_ln_b, gm_w_s, gm_b_s, hg_lb_table, hg_norm, ssm_conv_w, ssm_conv_b, ssm_dt_bias, ssm_a_log, ssm_d, ssm_norm, att_q_norm, att_k_norm, att_sinks, w_branch, w_out, ffn2_norm, ffn2_up, ffn2_down):
    batch, seq, d = x.shape
    depth = ffn1_norm.shape[0]
    xf = x.reshape(batch * seq, d)
    for l in range(depth):
        xf = _ffn(xf, ffn1_norm[l], ffn1_up[l], ffn1_down[l])
        xf = _mixer(xf, l, batch, seq, mix_norm[l], w_in[l], gm_ln_w[l], gm_ln_b[l], gm_w_s[l], gm_b_s[l],
                    hg_lb_table, hg_norm[l], ssm_conv_w[l], ssm_conv_b[l], ssm_dt_bias[l], ssm_a_log[l],
                    ssm_d[l], ssm_norm[l], att_q_norm[l], att_k_norm[l], att_sinks[l], w_branch[l], w_out[l])
        xf = _ffn(xf, ffn2_norm[l], ffn2_up[l], ffn2_down[l])
    return xf.reshape(batch, seq, d)
```

```python
import functools

import jax
import jax.numpy as jnp
from jax import lax
from jax.experimental import pallas as pl
from jax.experimental.pallas import tpu as pltpu

F32 = jnp.float32
BF16 = jnp.bfloat16
HIGHEST = lax.Precision.HIGHEST

NORM_EPS = 1e-6
LN_EPS = 1e-5

V7X_VMEM_BYTES = 64 * 1024 * 1024
VMEM_LIMIT_BYTES = V7X_VMEM_BYTES - 8 * 1024 * 1024
LANES = 128

BRANCH_WIDTH = 1024
N_BRANCH = 4
GM_GROUPS = 8
GM_CHUNK = 128
HG_HEADS = 8
HG_DK = 128
HG_CHUNK = 64
HG_SUB = 8
HG_HEADS_PER_STEP = 4
SSM_HEADS = 16
SSM_HEADDIM = 64
SSM_GROUPS = 4
SSM_STATE = 128
SSM_CONV = 4
SSM_CHUNK = 128
SSM_TAIL = 8
ATT_HEAD_DIM = 64
ATT_Q_HEADS = 16
ATT_KV_HEADS = 2
ATT_BLOCK = 128

COL_GM = 0
COL_HG_Q = 2048
COL_HG_F = 3072
COL_HG_I = 4096
COL_HG_G = 5120
COL_SSM_Z = 6144
COL_SSM_X = 7168
COL_SSM_BC = 8192
D_IN_MAIN = 9216
COL_GATES = 0
COL_ATT_Q = 16384
COL_ATT_K = 17408
COL_ATT_V = 17536
COL_SSM_DT = 17664
D_IN_TAIL = 17920

D_FF = 11008
D_FF_PADDED = 11264


def _params(semantics):
    return pltpu.CompilerParams(dimension_semantics=semantics, vmem_limit_bytes=VMEM_LIMIT_BYTES)


def _silu(x):
    return x * jax.nn.sigmoid(x)


def _gelu_exact(x):
    return 0.5 * x * (1.0 + lax.erf(x * (2.0 ** -0.5)))


def _rmsnorm_kernel(x_ref, w_ref, o_ref):
    x = x_ref[...]
    ms = jnp.mean(x * x, axis=-1, keepdims=True)
    o_ref[...] = (x * lax.rsqrt(ms + NORM_EPS) * w_ref[...]).astype(o_ref.dtype)


def _rmsnorm(x, w):
    m, d = x.shape
    bm = 256
    return pl.pallas_call(
        _rmsnorm_kernel,
        grid=(m // bm,),
        in_specs=[pl.BlockSpec((bm, d), lambda i: (i, 0)),
                  pl.BlockSpec((1, d), lambda i: (0, 0))],
        out_specs=pl.BlockSpec((bm, d), lambda i: (i, 0)),
        out_shape=jax.ShapeDtypeStruct((m, d), BF16),
        compiler_params=_params(("parallel",)),
        name="rmsnorm",
    )(x, w.reshape(1, d))


def _up_swiglu_kernel(x_ref, wa_ref, wb_ref, o_ref, *, real_blocks):
    j = pl.program_id(1)

    @pl.when(j < real_blocks)
    def _():
        x = x_ref[...]
        a = jnp.dot(x, wa_ref[...].astype(BF16), preferred_element_type=F32)
        b = jnp.dot(x, wb_ref[...].astype(BF16), preferred_element_type=F32)
        o_ref[...] = (_silu(a) * b).astype(o_ref.dtype)

    @pl.when(j >= real_blocks)
    def _():
        o_ref[...] = jnp.zeros_like(o_ref)


def _up_swiglu(h, w_up, layer):
    m, k = h.shape
    bm, bn = 2048, 256
    real_blocks = D_FF // bn
    last = real_blocks - 1
    return pl.pallas_call(
        functools.partial(_up_swiglu_kernel, real_blocks=real_blocks),
        grid=(m // bm, D_FF_PADDED // bn),
        in_specs=[pl.BlockSpec((bm, k), lambda i, j: (i, 0), pipeline_mode=pl.Buffered(1)),
                  pl.BlockSpec((None, k, bn), lambda i, j: (layer, 0, jnp.minimum(j, last))),
                  pl.BlockSpec((None, k, bn), lambda i, j: (layer, 0, real_blocks + jnp.minimum(j, last)))],
        out_specs=pl.BlockSpec((bm, bn), lambda i, j: (i, j)),
        out_shape=jax.ShapeDtypeStruct((m, D_FF_PADDED), BF16),
        compiler_params=_params(("parallel", "arbitrary")),
        name="ffn_up_swiglu",
    )(h, w_up, w_up)


def _down_residual_kernel(a_ref, w_ref, res_ref, o_ref, acc_ref, *, nk):
    k = pl.program_id(2)
    d = jnp.dot(a_ref[...], w_ref[...], preferred_element_type=F32)

    @pl.when(k == 0)
    def _():
        acc_ref[...] = d

    @pl.when(k > 0)
    def _():
        acc_ref[...] += d

    @pl.when(k == nk - 1)
    def _():
        o_ref[...] = res_ref[...] + 0.5 * acc_ref[...]


def _down_residual(act, w, res):
    m, k = act.shape
    n = w.shape[1]
    bm, bn, nk = 1024, 1024, 4
    bk = k // nk
    return pl.pallas_call(
        functools.partial(_down_residual_kernel, nk=nk),
        grid=(m // bm, n // bn, nk),
        in_specs=[pl.BlockSpec((bm, bk), lambda i, j, kk: (i, kk)),
                  pl.BlockSpec((bk, bn), lambda i, j, kk: (kk, j)),
                  pl.BlockSpec((bm, bn), lambda i, j, kk: (i, j))],
        out_specs=pl.BlockSpec((bm, bn), lambda i, j, kk: (i, j)),
        out_shape=jax.ShapeDtypeStruct((m, n), F32),
        scratch_shapes=[pltpu.VMEM((bm, bn), F32)],
        compiler_params=_params(("parallel", "parallel", "arbitrary")),
        name="ffn_down_residual",
    )(act, w, res)


def _matmul_kernel(x_ref, w_ref, o_ref):
    o_ref[...] = jnp.dot(x_ref[...], w_ref[...].astype(BF16), preferred_element_type=F32).astype(o_ref.dtype)


def _in_projection_main(h, w_in, layer):
    m, k = h.shape
    bm, bn = 2048, 512
    return pl.pallas_call(
        _matmul_kernel,
        grid=(m // bm, D_IN_MAIN // bn),
        in_specs=[pl.BlockSpec((bm, k), lambda i, j: (i, 0), pipeline_mode=pl.Buffered(1)),
                  pl.BlockSpec((None, k, bn), lambda i, j: (layer, 0, j))],
        out_specs=pl.BlockSpec((bm, bn), lambda i, j: (i, j)),
        out_shape=jax.ShapeDtypeStruct((m, D_IN_MAIN), F32),
        compiler_params=_params(("parallel", "arbitrary")),
        name="mixer_in_projection_main",
    )(h, w_in)


def _in_projection_tail(h, w):
    m, k = h.shape
    n = w.shape[1]
    bm, bn = 2048, 512
    return pl.pallas_call(
        _matmul_kernel,
        grid=(m // bm, n // bn),
        in_specs=[pl.BlockSpec((bm, k), lambda i, j: (i, 0), pipeline_mode=pl.Buffered(1)),
                  pl.BlockSpec((k, bn), lambda i, j: (0, j))],
        out_specs=pl.BlockSpec((bm, bn), lambda i, j: (i, j)),
        out_shape=jax.ShapeDtypeStruct((m, n), F32),
        compiler_params=_params(("parallel", "arbitrary")),
        name="mixer_in_projection_tail",
    )(h, w)


def _out_residual_kernel(x_ref, w_ref, res_ref, o_ref):
    o_ref[...] = res_ref[...] + jnp.dot(x_ref[...], w_ref[...], preferred_element_type=F32)


def _out_residual(merged, w, res):
    m, k = merged.shape
    n = w.shape[1]
    bm, bn = 1024, 512
    return pl.pallas_call(
        _out_residual_kernel,
        grid=(m // bm, n // bn),
        in_specs=[pl.BlockSpec((bm, k), lambda i, j: (i, 0)),
                  pl.BlockSpec((k, bn), lambda i, j: (0, j)),
                  pl.BlockSpec((bm, bn), lambda i, j: (i, j))],
        out_specs=pl.BlockSpec((bm, bn), lambda i, j: (i, j)),
        out_shape=jax.ShapeDtypeStruct((m, n), F32),
        compiler_params=_params(("parallel", "arbitrary")),
        name="mixer_out_residual",
    )(merged, w, res)


def _merge_kernel(b0_ref, b1_ref, b2_ref, b3_ref, w_ref, g0_ref, g1_ref, g2_ref, g3_ref, o_ref):
    branches = (b0_ref, b1_ref, b2_ref, b3_ref)
    gates = (g0_ref, g1_ref, g2_ref, g3_ref)
    merged = None
    for b in range(N_BRANCH):
        y = jnp.dot(branches[b][...], w_ref[b], preferred_element_type=F32)
        term = jax.nn.sigmoid(gates[b][...]) * y
        merged = term if merged is None else merged + term
    o_ref[...] = merged.astype(o_ref.dtype)


def _merge(branches, w_branch, p):
    m = p.shape[0]
    nb, kb, n = w_branch.shape
    bm, bn = 512, 1024
    gate_blocks = n // bn

    def gate_spec(b):
        return pl.BlockSpec((bm, bn), lambda j, i: (i, COL_GATES // bn + b * gate_blocks + j))

    return pl.pallas_call(
        _merge_kernel,
        grid=(n // bn, m // bm),
        in_specs=[pl.BlockSpec((bm, kb), lambda j, i: (i, 0)) for _ in range(nb)]
        + [pl.BlockSpec((nb, kb, bn), lambda j, i: (0, 0, j))]
        + [gate_spec(b) for b in range(nb)],
        out_specs=pl.BlockSpec((bm, bn), lambda j, i: (i, j)),
        out_shape=jax.ShapeDtypeStruct((m, n), BF16),
        compiler_params=_params(("parallel", "arbitrary")),
        name="branch_merge",
    )(*branches, w_branch, p, p, p, p)


def _gmlp_kernel(p_ref, lnw_ref, lnb_ref, ws_ref, bst_ref, o_ref, *, chunks):
    t = GM_CHUNK
    width = BRANCH_WIDTH
    gw = width // GM_GROUPS
    row = lax.broadcasted_iota(jnp.int32, (t, t), 0)
    col = lax.broadcasted_iota(jnp.int32, (t, t), 1)
    causal = row >= col
    mixers = [jnp.where(causal, ws_ref[g], 0.0) for g in range(GM_GROUPS)]
    for c in range(chunks):
        rows = slice(c * t, (c + 1) * t)
        act = _gelu_exact(p_ref[rows, :])
        u = act[:, :width]
        v = act[:, width:]
        mu = jnp.mean(v, axis=-1, keepdims=True)
        var = jnp.mean(jnp.square(v - mu), axis=-1, keepdims=True)
        vn = (v - mu) * lax.rsqrt(var + LN_EPS) * lnw_ref[...] + lnb_ref[...]
        for g in range(GM_GROUPS):
            cols = slice(g * gw, (g + 1) * gw)
            mixed = jnp.dot(mixers[g], vn[:, cols], preferred_element_type=F32) + bst_ref[:, g:g + 1]
            o_ref[rows, cols] = (u[:, cols] * mixed).astype(o_ref.dtype)


def _gmlp_branch(p, ln_w, ln_b, w_s, b_s):
    m = p.shape[0]
    tb = 512
    width = BRANCH_WIDTH
    return pl.pallas_call(
        functools.partial(_gmlp_kernel, chunks=tb // GM_CHUNK),
        grid=(m // tb,),
        in_specs=[pl.BlockSpec((tb, 2 * width), lambda i: (i, COL_GM // (2 * width))),
                  pl.BlockSpec((1, width), lambda i: (0, 0)),
                  pl.BlockSpec((1, width), lambda i: (0, 0)),
                  pl.BlockSpec((GM_GROUPS, GM_CHUNK, GM_CHUNK), lambda i: (0, 0, 0)),
                  pl.BlockSpec((GM_CHUNK, GM_GROUPS), lambda i: (0, 0))],
        out_specs=pl.BlockSpec((tb, width), lambda i: (i, 0)),
        out_shape=jax.ShapeDtypeStruct((m, width), BF16),
        compiler_params=_params(("parallel",)),
        name="gmlp_branch",
    )(p, ln_w.reshape(1, width), ln_b.reshape(1, width), w_s, b_s.T)


def _rows_from(x, index_of_block, block_rows):
    blocks = x.shape[0] // block_rows
    parts = [jnp.broadcast_to(x[index_of_block(b):index_of_block(b) + 1], (block_rows, x.shape[1]))
             for b in range(blocks)]
    return jnp.concatenate(parts, axis=0)


def _hgrn2_intra_scores(q, k, g, seg_ref):
    c_len, dk = q.shape
    row = lax.broadcasted_iota(jnp.int32, (c_len, c_len), 0)
    col = lax.broadcasted_iota(jnp.int32, (c_len, c_len), 1)
    pos = lax.broadcasted_iota(jnp.int32, (c_len, dk), 0)
    sub = pos & (HG_SUB - 1)
    nt_dims = (((1,), (1,)), ((), ()))

    pieces = []
    for s in range(HG_SUB):
        g_s = _rows_from(g, lambda b: b * HG_SUB + s, HG_SUB)
        k_s = _rows_from(k, lambda b: b * HG_SUB + s, HG_SUB)
        decay = jnp.exp(jnp.where(sub >= s, g - g_s, -jnp.inf))
        pieces.append(q * k_s * decay)
    diag = jnp.dot(jnp.concatenate(pieces, axis=1), seg_ref[...], preferred_element_type=F32)
    scores = jnp.where((row ^ col) < HG_SUB, diag, 0.0)

    width = HG_SUB
    while width < c_len:
        right = (pos & width) != 0
        g_ref = _rows_from(g, lambda b: b * 2 * width + width - 1, 2 * width)
        q_s = q * jnp.exp(jnp.where(right, g - g_ref, -jnp.inf))
        k_s = k * jnp.exp(jnp.where(right, -jnp.inf, g_ref - g))
        cross = lax.dot_general(q_s, k_s, nt_dims, preferred_element_type=F32)
        scores = scores + jnp.where((row ^ col) < 2 * width, cross, 0.0)
        width *= 2
    return scores


def _hgrn2_kernel(q_ref, f_ref, i_ref, g_ref, lbt_ref, nw_ref, seg_ref, o_ref, state_ref, *, layer, chunks):
    c_len = HG_CHUNK

    @pl.when(pl.program_id(2) == 0)
    def _():
        state_ref[...] = jnp.zeros_like(state_ref)

    table = lbt_ref[...]
    e = jnp.exp(table - jnp.max(table, axis=0, keepdims=True))
    probs = e / jnp.sum(e, axis=0, keepdims=True)
    lb_all = jnp.sum(probs[:layer + 1], axis=0, keepdims=True) - probs[0:1]

    row = lax.broadcasted_iota(jnp.int32, (c_len, c_len), 0)
    col = lax.broadcasted_iota(jnp.int32, (c_len, c_len), 1)
    cumsum_mat = (row >= col).astype(F32)
    nt_dims = (((1,), (1,)), ((), ()))
    tn_dims = (((0,), (0,)), ((), ()))

    def chunk_body(c, carry):
        r0 = pl.multiple_of(c * c_len, c_len)
        rows = pl.ds(r0, c_len)
        for h in range(HG_HEADS_PER_STEP):
            cols = slice(h * HG_DK, (h + 1) * HG_DK)
            lb = lb_all[:, cols]
            q = _silu(q_ref[rows, cols])
            f = lb + (1.0 - lb) * jax.nn.sigmoid(f_ref[rows, cols])
            log_f = jnp.log(f)
            k = 1.0 - f
            v = i_ref[rows, cols]
            g = jnp.dot(cumsum_mat, log_f, precision=HIGHEST, preferred_element_type=F32)
            state_t = state_ref[h]
            o = lax.dot_general(q * jnp.exp(g), state_t, nt_dims, preferred_element_type=F32)
            o = o + jnp.dot(_hgrn2_intra_scores(q, k, g, seg_ref), v, preferred_element_type=F32)
            g_last = g[c_len - 1:c_len]
            k_s = k * jnp.exp(g_last - g)
            state_ref[h] = state_t * jnp.exp(g_last) + lax.dot_general(v, k_s, tn_dims,
                                                                       preferred_element_type=F32)
            ms = jnp.mean(o * o, axis=-1, keepdims=True)
            y = o * lax.rsqrt(ms + NORM_EPS) * nw_ref[:, cols]
            o_ref[rows, cols] = (y * _silu(g_ref[rows, cols])).astype(o_ref.dtype)
        return carry

    lax.fori_loop(0, chunks, chunk_body, 0, unroll=2)


def _hgrn2_branch(p, lb_table, norm_w, layer, batch, seq):
    m = p.shape[0]
    ts = 512
    steps = seq // ts
    depth = lb_table.shape[0]
    wblk = HG_HEADS_PER_STEP * HG_DK

    def in_spec(col0):
        return pl.BlockSpec((ts, wblk), lambda b, h, t: (b * steps + t, col0 // wblk + h))

    seg = (jnp.arange(HG_SUB * HG_DK)[:, None] // HG_DK == jnp.arange(HG_CHUNK)[None, :] % HG_SUB).astype(F32)

    return pl.pallas_call(
        functools.partial(_hgrn2_kernel, layer=layer, chunks=ts // HG_CHUNK),
        grid=(batch, HG_HEADS // HG_HEADS_PER_STEP, steps),
        in_specs=[in_spec(COL_HG_Q), in_spec(COL_HG_F), in_spec(COL_HG_I), in_spec(COL_HG_G),
                  pl.BlockSpec((depth, wblk), lambda b, h, t: (0, h)),
                  pl.BlockSpec((1, wblk), lambda b, h, t: (0, h)),
                  pl.BlockSpec((HG_SUB * HG_DK, HG_CHUNK), lambda b, h, t: (0, 0))],
        out_specs=pl.BlockSpec((ts, wblk), lambda b, h, t: (b * steps + t, h)),
        out_shape=jax.ShapeDtypeStruct((m, BRANCH_WIDTH), BF16),
        scratch_shapes=[pltpu.VMEM((HG_HEADS_PER_STEP, HG_DK, HG_DK), F32)],
        compiler_params=_params(("parallel", "parallel", "arbitrary")),
        name="hgrn2_branch",
    )(p, p, p, p, lb_table, norm_w.reshape(1, BRANCH_WIDTH), seg)


def _ssd_kernel(z_ref, x_ref, bc_ref, dt_ref, cw_ref, cb_ref, dtb_ref, alog_ref, dsk_ref, nw_ref,
                expand_ref, o_ref, tail_ref, state_ref):
    length = SSM_CHUNK
    inner = BRANCH_WIDTH
    gw = inner // SSM_GROUPS
    n = SSM_STATE

    @pl.when(pl.program_id(1) == 0)
    def _():
        tail_ref[...] = jnp.zeros_like(tail_ref)
        state_ref[...] = jnp.zeros_like(state_ref)

    xbc = jnp.concatenate([x_ref[...], bc_ref[...]], axis=1)
    ext = jnp.concatenate([tail_ref[...], xbc], axis=0)
    conv = cb_ref[...]
    for j in range(SSM_CONV):
        start = SSM_TAIL - (SSM_CONV - 1) + j
        conv = conv + ext[start:start + length] * cw_ref[j:j + 1, :]
    tail_ref[...] = xbc[length - SSM_TAIL:length]
    xbc_act = _silu(conv)
    x_c = xbc_act[:, :inner]
    b_m = xbc_act[:, inner:inner + SSM_GROUPS * n]
    c_m = xbc_act[:, inner + SSM_GROUPS * n:]

    dt = jax.nn.softplus(dt_ref[...] + dtb_ref[...])
    a = dt * (-jnp.exp(alog_ref[...]))
    row = lax.broadcasted_iota(jnp.int32, (length, length), 0)
    col = lax.broadcasted_iota(jnp.int32, (length, length), 1)
    causal = row >= col
    cs = jnp.dot(causal.astype(F32), a, precision=HIGHEST, preferred_element_type=F32)
    cs_t = cs.T
    expand = expand_ref[...]
    dt_e = jnp.dot(dt, expand, precision=HIGHEST, preferred_element_type=F32)
    cs_e = jnp.dot(cs, expand, precision=HIGHEST, preferred_element_type=F32)
    cs_last_e = cs_e[length - 1:length]
    xs = x_c * dt_e
    in_decay_e = jnp.exp(cs_e)
    state_decay_e = jnp.exp(cs_last_e - cs_e)
    chunk_decay_e = jnp.exp(cs_last_e)
    z = z_ref[...]

    nt_dims = (((1,), (1,)), ((), ()))
    tn_dims = (((0,), (0,)), ((), ()))
    heads_per_group = SSM_HEADS // SSM_GROUPS
    p = SSM_HEADDIM
    for g in range(SSM_GROUPS):
        gcols = slice(g * gw, (g + 1) * gw)
        b_g = b_m[:, g * n:(g + 1) * n]
        c_g = c_m[:, g * n:(g + 1) * n]
        cb = lax.dot_general(c_g, b_g, nt_dims, preferred_element_type=F32)
        prev_t = state_ref[g]
        y_off = jnp.dot(c_g, prev_t, preferred_element_type=F32) * in_decay_e[:, gcols]
        parts = []
        for r in range(heads_per_group):
            h = g * heads_per_group + r
            decay = jnp.exp(jnp.where(causal, cs[:, h:h + 1] - cs_t[h:h + 1, :], -jnp.inf))
            parts.append(jnp.dot(cb * decay, xs[:, h * p:(h + 1) * p], preferred_element_type=F32))
        y_diag = jnp.concatenate(parts, axis=1)
        state_ref[g] = prev_t * chunk_decay_e[:, gcols] + lax.dot_general(
            b_g, xs[:, gcols] * state_decay_e[:, gcols], tn_dims, preferred_element_type=F32)
        y = (y_diag + y_off) + x_c[:, gcols] * dsk_ref[:, gcols]
        y = y * _silu(z[:, gcols])
        ms = jnp.mean(y * y, axis=-1, keepdims=True)
        o_ref[:, gcols] = (y * lax.rsqrt(ms + NORM_EPS) * nw_ref[:, gcols]).astype(o_ref.dtype)


def _ssd_branch(p_main, p_tail, conv_w, conv_b, dt_bias, a_log, d_skip, norm_w, batch, seq):
    m = p_main.shape[0]
    length = SSM_CHUNK
    steps = seq // length
    inner = BRANCH_WIDTH
    conv_dim = conv_w.shape[1]

    def pad_heads(vec):
        return jnp.pad(vec, (0, LANES - SSM_HEADS)).reshape(1, LANES)

    expand = (jnp.arange(LANES)[:, None] == (jnp.arange(inner) // SSM_HEADDIM)[None, :]).astype(F32)
    d_skip_e = jnp.repeat(d_skip, SSM_HEADDIM).reshape(1, inner)

    def rows(width, col0):
        return pl.BlockSpec((length, width), lambda b, t: (b * steps + t, col0 // width))

    def whole(shape):
        return pl.BlockSpec(shape, lambda b, t: (0,) * len(shape))

    return pl.pallas_call(
        _ssd_kernel,
        grid=(batch, steps),
        in_specs=[rows(inner, COL_SSM_Z), rows(inner, COL_SSM_X), rows(inner, COL_SSM_BC),
                  rows(LANES, COL_SSM_DT),
                  whole((SSM_CONV, conv_dim)), whole((1, conv_dim)),
                  whole((1, LANES)), whole((1, LANES)), whole((1, inner)), whole((1, inner)),
                  whole((LANES, inner))],
        out_specs=pl.BlockSpec((length, inner), lambda b, t: (b * steps + t, 0)),
        out_shape=jax.ShapeDtypeStruct((m, inner), BF16),
        scratch_shapes=[pltpu.VMEM((SSM_TAIL, conv_dim), F32),
                        pltpu.VMEM((SSM_GROUPS, SSM_STATE, inner // SSM_GROUPS), F32)],
        compiler_params=_params(("parallel", "arbitrary")),
        name="ssd_branch",
    )(p_main, p_main, p_main, p_tail, conv_w, conv_b.reshape(1, conv_dim), pad_heads(dt_bias), pad_heads(a_log),
      d_skip_e, norm_w.reshape(1, inner), expand)


def _swa_kernel(sink_ref, q_ref, kc_ref, kp_ref, vc_ref, vp_ref, qw_ref, kw_ref, o_ref):
    t = ATT_BLOCK
    hd = ATT_HEAD_DIM
    group = ATT_Q_HEADS // ATT_KV_HEADS
    blk = pl.program_id(1)
    q = q_ref[...]
    keys = jnp.concatenate([kp_ref[...], kc_ref[...]], axis=0)
    vals = jnp.concatenate([vp_ref[...], vc_ref[...]], axis=0)
    qi = lax.broadcasted_iota(jnp.int32, (t, 2 * t), 0)
    kj = lax.broadcasted_iota(jnp.int32, (t, 2 * t), 1)
    first_key = jnp.where(blk > 0, 0, t)
    valid = (kj > qi) & (kj <= qi + t) & (kj >= first_key)
    nt_dims = (((1,), (1,)), ((), ()))
    scale = hd ** -0.5
    outs = []
    for c in range(ATT_KV_HEADS):
        k_h = keys[:, c * hd:(c + 1) * hd]
        k_n = k_h * lax.rsqrt(jnp.mean(k_h * k_h, axis=-1, keepdims=True) + NORM_EPS) * kw_ref[...]
        v_h = vals[:, c * hd:(c + 1) * hd]
        for r in range(group):
            h = c * group + r
            q_h = q[:, h * hd:(h + 1) * hd]
            q_n = q_h * lax.rsqrt(jnp.mean(q_h * q_h, axis=-1, keepdims=True) + NORM_EPS) * qw_ref[...]
            s = lax.dot_general(q_n, k_n, nt_dims, preferred_element_type=F32) * scale
            s = jnp.where(valid, s, -jnp.inf)
            sink = sink_ref[h]
            mx = jnp.maximum(jnp.max(s, axis=-1, keepdims=True), sink)
            e = jnp.exp(s - mx)
            probs = e / (jnp.sum(e, axis=-1, keepdims=True) + jnp.exp(sink - mx))
            outs.append(jnp.dot(probs, v_h, preferred_element_type=F32))
    o_ref[...] = jnp.concatenate(outs, axis=1).astype(o_ref.dtype)


def _swa_branch(p, q_norm_w, k_norm_w, sinks, batch, seq):
    m = p.shape[0]
    t = ATT_BLOCK
    steps = seq // t
    qw = ATT_Q_HEADS * ATT_HEAD_DIM
    kvw = ATT_KV_HEADS * ATT_HEAD_DIM

    def cur(width, col0):
        return pl.BlockSpec((t, width), lambda b, i: (b * steps + i, col0 // width))

    def prev(width, col0):
        return pl.BlockSpec((t, width), lambda b, i: (b * steps + jnp.maximum(i - 1, 0), col0 // width))

    return pl.pallas_call(
        _swa_kernel,
        grid=(batch, steps),
        in_specs=[pl.BlockSpec(memory_space=pltpu.SMEM),
                  cur(qw, COL_ATT_Q), cur(kvw, COL_ATT_K), prev(kvw, COL_ATT_K),
                  cur(kvw, COL_ATT_V), prev(kvw, COL_ATT_V),
                  pl.BlockSpec((1, ATT_HEAD_DIM), lambda b, i: (0, 0)),
                  pl.BlockSpec((1, ATT_HEAD_DIM), lambda b, i: (0, 0))],
        out_specs=pl.BlockSpec((t, qw), lambda b, i: (b * steps + i, 0)),
        out_shape=jax.ShapeDtypeStruct((m, qw), BF16),
        compiler_params=_params(("parallel", "parallel")),
        name="swa_branch",
    )(sinks, p, p, p, p, p, q_norm_w.reshape(1, ATT_HEAD_DIM), k_norm_w.reshape(1, ATT_HEAD_DIM))


def _prep_w_down(w_down):
    return jnp.pad(w_down.astype(BF16), ((0, D_FF_PADDED - D_FF), (0, 0)))


def _prep_w_in_tail(w_in):
    d = w_in.shape[0]
    dt_end = D_IN_MAIN + SSM_HEADS
    att_end = dt_end + (ATT_Q_HEADS + 2 * ATT_KV_HEADS) * ATT_HEAD_DIM
    used = COL_SSM_DT + SSM_HEADS
    parts = [w_in[:, att_end:], w_in[:, dt_end:att_end], w_in[:, D_IN_MAIN:dt_end],
             jnp.zeros((d, D_IN_TAIL - used), w_in.dtype)]
    return jnp.concatenate(parts, axis=1).astype(BF16)


def _ffn(x, layer, norm_w, w_up, w_down):
    h = _rmsnorm(x, norm_w[layer])
    act = _up_swiglu(h, w_up, layer)
    return _down_residual(act, _prep_w_down(w_down[layer]), x)


def _mixer(x, layer, batch, seq, mix_norm, w_in, gm_ln_w, gm_ln_b, gm_w_s, gm_b_s, hg_lb_table, hg_norm,
           ssm_conv_w, ssm_conv_b, ssm_dt_bias, ssm_a_log, ssm_d, ssm_norm,
           att_q_norm, att_k_norm, att_sinks, w_branch, w_out):
    h = _rmsnorm(x, mix_norm)
    p_main = _in_projection_main(h, w_in, layer)
    p_tail = _in_projection_tail(h, _prep_w_in_tail(w_in[layer]))
    branches = (
        _gmlp_branch(p_main, gm_ln_w, gm_ln_b, gm_w_s, gm_b_s),
        _hgrn2_branch(p_main, hg_lb_table, hg_norm, layer, batch, seq),
        _ssd_branch(p_main, p_tail, ssm_conv_w, ssm_conv_b, ssm_dt_bias, ssm_a_log, ssm_d, ssm_norm, batch, seq),
        _swa_branch(p_tail, att_q_norm, att_k_norm, att_sinks, batch, seq),
    )
    merged = _merge(branches, w_branch.astype(BF16), p_tail)
    return _out_residual(merged, w_out.astype(BF16), x)


def kernel(x, ffn1_norm, ffn1_up, ffn1_down, mix_norm, w_in, gm_ln_w, gm_ln_b, gm_w_s, gm_b_s, hg_lb_table, hg_norm, ssm_conv_w, ssm_conv_b, ssm_dt_bias, ssm_a_log, ssm_d, ssm_norm, att_q_norm, att_k_norm, att_sinks, w_branch, w_out, ffn2_norm, ffn2_up, ffn2_down):
    batch, seq, d = x.shape
    depth = ffn1_norm.shape[0]
    xf = x.reshape(batch * seq, d)
    for l in range(depth):
        xf = _ffn(xf, l, ffn1_norm, ffn1_up, ffn1_down)
        xf = _mixer(xf, l, batch, seq, mix_norm[l], w_in, gm_ln_w[l], gm_ln_b[l], gm_w_s[l], gm_b_s[l],
                    hg_lb_table, hg_norm[l], ssm_conv_w[l], ssm_conv_b[l], ssm_dt_bias[l], ssm_a_log[l],
                    ssm_d[l], ssm_norm[l], att_q_norm[l], att_k_norm[l], att_sinks[l], w_branch[l], w_out[l])
        xf = _ffn(xf, l, ffn2_norm, ffn2_up, ffn2_down)
    return xf.reshape(batch, seq, d)
```

```python
import functools

import jax
import jax.numpy as jnp
from jax import lax
from jax.experimental import pallas as pl
from jax.experimental.pallas import tpu as pltpu

F32 = jnp.float32
BF16 = jnp.bfloat16
HIGHEST = lax.Precision.HIGHEST

NORM_EPS = 1e-6
LN_EPS = 1e-5

V7X_VMEM_BYTES = 64 * 1024 * 1024
VMEM_LIMIT_BYTES = V7X_VMEM_BYTES - 8 * 1024 * 1024
LANES = 128
SUBLANES = 8

BRANCH_WIDTH = 1024
N_BRANCH = 4
GM_GROUPS = 8
GM_CHUNK = 128
HG_HEADS = 8
HG_DK = 128
HG_CHUNK = 64
HG_SUB = 8
HG_HEADS_PER_STEP = 4
SSM_HEADS = 16
SSM_HEADDIM = 64
SSM_GROUPS = 4
SSM_STATE = 128
SSM_CONV = 4
SSM_CHUNK = 128
SSM_TAIL = 8
ATT_HEAD_DIM = 64
ATT_Q_HEADS = 16
ATT_KV_HEADS = 2
ATT_BLOCK = 128

ROW0_MAIN = 0
D_IN_MAIN = 9216
COL_GM = 0
COL_HG_Q = 2048
COL_HG_F = 3072
COL_HG_I = 4096
COL_HG_G = 5120
COL_SSM_Z = 6144
COL_SSM_X = 7168
COL_SSM_BC = 8192
DT_LANE0 = LANES - SSM_HEADS
ROW0_DT = D_IN_MAIN - DT_LANE0
ROW0_ATT = D_IN_MAIN + SSM_HEADS
D_IN_ATT = (ATT_Q_HEADS + 2 * ATT_KV_HEADS) * ATT_HEAD_DIM
COL_ATT_Q = 0
COL_ATT_K = ATT_Q_HEADS * ATT_HEAD_DIM
COL_ATT_V = COL_ATT_K + ATT_KV_HEADS * ATT_HEAD_DIM
ROW0_GATES = ROW0_ATT + D_IN_ATT
D_IN_GATES = N_BRANCH * 4096

D_FF = 11008
D_FF_PADDED = 11264


def _params(semantics):
    return pltpu.CompilerParams(dimension_semantics=semantics, vmem_limit_bytes=VMEM_LIMIT_BYTES)


def _silu(x):
    return x * jax.nn.sigmoid(x)


def _gelu_exact(x):
    return 0.5 * x * (1.0 + lax.erf(x * (2.0 ** -0.5)))


def _rmsnorm_kernel(x_ref, w_ref, o_ref):
    x = x_ref[...]
    ms = jnp.mean(x * x, axis=-1, keepdims=True)
    o_ref[...] = (x * lax.rsqrt(ms + NORM_EPS) * w_ref[...]).astype(o_ref.dtype)


def _rmsnorm(x, w):
    m, d = x.shape
    bm = 256
    return pl.pallas_call(
        _rmsnorm_kernel,
        grid=(m // bm,),
        in_specs=[pl.BlockSpec((bm, d), lambda i: (i, 0)),
                  pl.BlockSpec((1, d), lambda i: (0, 0))],
        out_specs=pl.BlockSpec((bm, d), lambda i: (i, 0)),
        out_shape=jax.ShapeDtypeStruct((m, d), BF16),
        compiler_params=_params(("parallel",)),
        name="rmsnorm",
    )(x, w.reshape(1, d))


def _up_swiglu_kernel(x_ref, wa_ref, wb_ref, o_ref, *, real_blocks):
    j = pl.program_id(1)

    @pl.when(j < real_blocks)
    def _():
        x = x_ref[...]
        a = jnp.dot(x, wa_ref[...].astype(BF16), preferred_element_type=F32)
        b = jnp.dot(x, wb_ref[...].astype(BF16), preferred_element_type=F32)
        o_ref[...] = (_silu(a) * b).astype(o_ref.dtype)

    @pl.when(j >= real_blocks)
    def _():
        o_ref[...] = jnp.zeros_like(o_ref)


def _up_swiglu(h, w_up, layer):
    m, k = h.shape
    bm, bn = 2048, 256
    real_blocks = D_FF // bn
    last = real_blocks - 1
    return pl.pallas_call(
        functools.partial(_up_swiglu_kernel, real_blocks=real_blocks),
        grid=(m // bm, D_FF_PADDED // bn),
        in_specs=[pl.BlockSpec((bm, k), lambda i, j: (i, 0), pipeline_mode=pl.Buffered(1)),
                  pl.BlockSpec((None, k, bn), lambda i, j: (layer, 0, jnp.minimum(j, last))),
                  pl.BlockSpec((None, k, bn), lambda i, j: (layer, 0, real_blocks + jnp.minimum(j, last)))],
        out_specs=pl.BlockSpec((bm, bn), lambda i, j: (i, j)),
        out_shape=jax.ShapeDtypeStruct((m, D_FF_PADDED), BF16),
        compiler_params=_params(("parallel", "arbitrary")),
        name="ffn_up_swiglu",
    )(h, w_up, w_up)


def _down_residual_kernel(a_ref, w_ref, res_ref, o_ref, acc_ref, *, nk):
    k = pl.program_id(2)
    d = jnp.dot(a_ref[...], w_ref[...], preferred_element_type=F32)

    @pl.when(k == 0)
    def _():
        acc_ref[...] = d

    @pl.when(k > 0)
    def _():
        acc_ref[...] += d

    @pl.when(k == nk - 1)
    def _():
        o_ref[...] = res_ref[...] + 0.5 * acc_ref[...]


def _down_residual(act, w, layer, res):
    m, k = act.shape
    n = w.shape[2]
    bm, bn, nk = 1024, 1024, 4
    bk = k // nk
    return pl.pallas_call(
        functools.partial(_down_residual_kernel, nk=nk),
        grid=(m // bm, n // bn, nk),
        in_specs=[pl.BlockSpec((bm, bk), lambda i, j, kk: (i, kk)),
                  pl.BlockSpec((None, bk, bn), lambda i, j, kk: (layer, kk, j)),
                  pl.BlockSpec((bm, bn), lambda i, j, kk: (i, j))],
        out_specs=pl.BlockSpec((bm, bn), lambda i, j, kk: (i, j)),
        out_shape=jax.ShapeDtypeStruct((m, n), F32),
        scratch_shapes=[pltpu.VMEM((bm, bn), F32)],
        compiler_params=_params(("parallel", "parallel", "arbitrary")),
        name="ffn_down_residual",
    )(act, w, res)


def _nt_matmul_kernel(x_ref, wt_ref, o_ref):
    w = wt_ref[...].astype(BF16)
    o_ref[...] = lax.dot_general(x_ref[...], w, (((1,), (1,)), ((), ())), preferred_element_type=F32)


def _in_projection(h, w_in_t, layer, row0, n, bn, name):
    m, k = h.shape
    bm = 2048
    return pl.pallas_call(
        _nt_matmul_kernel,
        grid=(m // bm, n // bn),
        in_specs=[pl.BlockSpec((bm, k), lambda i, j: (i, 0), pipeline_mode=pl.Buffered(1)),
                  pl.BlockSpec((None, pl.Element(bn), pl.Element(k)),
                               lambda i, j: (layer, pl.multiple_of(row0 + j * bn, SUBLANES), 0))],
        out_specs=pl.BlockSpec((bm, bn), lambda i, j: (i, j)),
        out_shape=jax.ShapeDtypeStruct((m, n), F32),
        compiler_params=_params(("parallel", "arbitrary")),
        name=name,
    )(h, w_in_t)


def _out_residual_kernel(x_ref, w_ref, res_ref, o_ref):
    o_ref[...] = res_ref[...] + jnp.dot(x_ref[...], w_ref[...], preferred_element_type=F32)


def _out_residual(merged, w, layer, res):
    m, k = merged.shape
    n = w.shape[2]
    bm, bn = 1024, 512
    return pl.pallas_call(
        _out_residual_kernel,
        grid=(m // bm, n // bn),
        in_specs=[pl.BlockSpec((bm, k), lambda i, j: (i, 0)),
                  pl.BlockSpec((None, k, bn), lambda i, j: (layer, 0, j)),
                  pl.BlockSpec((bm, bn), lambda i, j: (i, j))],
        out_specs=pl.BlockSpec((bm, bn), lambda i, j: (i, j)),
        out_shape=jax.ShapeDtypeStruct((m, n), F32),
        compiler_params=_params(("parallel", "arbitrary")),
        name="mixer_out_residual",
    )(merged, w, res)


def _merge_kernel(b0_ref, b1_ref, b2_ref, b3_ref, w_ref, g0_ref, g1_ref, g2_ref, g3_ref, o_ref):
    branches = (b0_ref, b1_ref, b2_ref, b3_ref)
    gates = (g0_ref, g1_ref, g2_ref, g3_ref)
    merged = None
    for b in range(N_BRANCH):
        y = jnp.dot(branches[b][...], w_ref[b], preferred_element_type=F32)
        term = jax.nn.sigmoid(gates[b][...]) * y
        merged = term if merged is None else merged + term
    o_ref[...] = merged.astype(o_ref.dtype)


def _merge(branches, w_branch, layer, p_gates):
    m = p_gates.shape[0]
    _, nb, kb, n = w_branch.shape
    bm, bn = 512, 1024
    gate_blocks = n // bn

    def gate_spec(b):
        return pl.BlockSpec((bm, bn), lambda j, i: (i, b * gate_blocks + j))

    return pl.pallas_call(
        _merge_kernel,
        grid=(n // bn, m // bm),
        in_specs=[pl.BlockSpec((bm, kb), lambda j, i: (i, 0)) for _ in range(nb)]
        + [pl.BlockSpec((None, nb, kb, bn), lambda j, i: (layer, 0, 0, j))]
        + [gate_spec(b) for b in range(nb)],
        out_specs=pl.BlockSpec((bm, bn), lambda j, i: (i, j)),
        out_shape=jax.ShapeDtypeStruct((m, n), BF16),
        compiler_params=_params(("parallel", "arbitrary")),
        name="branch_merge",
    )(*branches, w_branch, p_gates, p_gates, p_gates, p_gates)


def _gmlp_kernel(p_ref, lnw_ref, lnb_ref, ws_ref, bst_ref, o_ref, *, chunks):
    t = GM_CHUNK
    width = BRANCH_WIDTH
    gw = width // GM_GROUPS
    row = lax.broadcasted_iota(jnp.int32, (t, t), 0)
    col = lax.broadcasted_iota(jnp.int32, (t, t), 1)
    causal = row >= col
    mixers = [jnp.where(causal, ws_ref[g], 0.0) for g in range(GM_GROUPS)]
    for c in range(chunks):
        rows = slice(c * t, (c + 1) * t)
        act = _gelu_exact(p_ref[rows, :])
        u = act[:, :width]
        v = act[:, width:]
        mu = jnp.mean(v, axis=-1, keepdims=True)
        var = jnp.mean(jnp.square(v - mu), axis=-1, keepdims=True)
        vn = (v - mu) * lax.rsqrt(var + LN_EPS) * lnw_ref[...] + lnb_ref[...]
        for g in range(GM_GROUPS):
            cols = slice(g * gw, (g + 1) * gw)
            mixed = jnp.dot(mixers[g], vn[:, cols], preferred_element_type=F32) + bst_ref[:, g:g + 1]
            o_ref[rows, cols] = (u[:, cols] * mixed).astype(o_ref.dtype)


def _gmlp_branch(p, ln_w, ln_b, w_s, b_s):
    m = p.shape[0]
    tb = 512
    width = BRANCH_WIDTH
    return pl.pallas_call(
        functools.partial(_gmlp_kernel, chunks=tb // GM_CHUNK),
        grid=(m // tb,),
        in_specs=[pl.BlockSpec((tb, 2 * width), lambda i: (i, COL_GM // (2 * width))),
                  pl.BlockSpec((1, width), lambda i: (0, 0)),
                  pl.BlockSpec((1, width), lambda i: (0, 0)),
                  pl.BlockSpec((GM_GROUPS, GM_CHUNK, GM_CHUNK), lambda i: (0, 0, 0)),
                  pl.BlockSpec((GM_CHUNK, GM_GROUPS), lambda i: (0, 0))],
        out_specs=pl.BlockSpec((tb, width), lambda i: (i, 0)),
        out_shape=jax.ShapeDtypeStruct((m, width), BF16),
        compiler_params=_params(("parallel",)),
        name="gmlp_branch",
    )(p, ln_w.reshape(1, width), ln_b.reshape(1, width), w_s, b_s.T)


def _rows_from(x, index_of_block, block_rows):
    blocks = x.shape[0] // block_rows
    parts = [jnp.broadcast_to(x[index_of_block(b):index_of_block(b) + 1], (block_rows, x.shape[1]))
             for b in range(blocks)]
    return jnp.concatenate(parts, axis=0)


def _hgrn2_intra_scores(q, k, g, seg_ref):
    c_len, dk = q.shape
    row = lax.broadcasted_iota(jnp.int32, (c_len, c_len), 0)
    col = lax.broadcasted_iota(jnp.int32, (c_len, c_len), 1)
    pos = lax.broadcasted_iota(jnp.int32, (c_len, dk), 0)
    sub = pos & (HG_SUB - 1)
    nt_dims = (((1,), (1,)), ((), ()))

    pieces = []
    for s in range(HG_SUB):
        g_s = _rows_from(g, lambda b: b * HG_SUB + s, HG_SUB)
        k_s = _rows_from(k, lambda b: b * HG_SUB + s, HG_SUB)
        decay = jnp.exp(jnp.where(sub >= s, g - g_s, -jnp.inf))
        pieces.append(q * k_s * decay)
    diag = jnp.dot(jnp.concatenate(pieces, axis=1), seg_ref[...], preferred_element_type=F32)
    scores = jnp.where((row ^ col) < HG_SUB, diag, 0.0)

    width = HG_SUB
    while width < c_len:
        right = (pos & width) != 0
        g_ref = _rows_from(g, lambda b: b * 2 * width + width - 1, 2 * width)
        q_s = q * jnp.exp(jnp.where(right, g - g_ref, -jnp.inf))
        k_s = k * jnp.exp(jnp.where(right, -jnp.inf, g_ref - g))
        cross = lax.dot_general(q_s, k_s, nt_dims, preferred_element_type=F32)
        scores = scores + jnp.where((row ^ col) < 2 * width, cross, 0.0)
        width *= 2
    return scores


def _hgrn2_kernel(q_ref, f_ref, i_ref, g_ref, lbt_ref, nw_ref, seg_ref, o_ref, state_ref, *, layer, chunks):
    c_len = HG_CHUNK

    @pl.when(pl.program_id(2) == 0)
    def _():
        state_ref[...] = jnp.zeros_like(state_ref)

    table = lbt_ref[...]
    e = jnp.exp(table - jnp.max(table, axis=0, keepdims=True))
    probs = e / jnp.sum(e, axis=0, keepdims=True)
    lb_all = jnp.sum(probs[:layer + 1], axis=0, keepdims=True) - probs[0:1]

    row = lax.broadcasted_iota(jnp.int32, (c_len, c_len), 0)
    col = lax.broadcasted_iota(jnp.int32, (c_len, c_len), 1)
    cumsum_mat = (row >= col).astype(F32)
    nt_dims = (((1,), (1,)), ((), ()))
    tn_dims = (((0,), (0,)), ((), ()))

    def chunk_body(c, carry):
        r0 = pl.multiple_of(c * c_len, c_len)
        rows = pl.ds(r0, c_len)
        for h in range(HG_HEADS_PER_STEP):
            cols = slice(h * HG_DK, (h + 1) * HG_DK)
            lb = lb_all[:, cols]
            q = _silu(q_ref[rows, cols])
            f = lb + (1.0 - lb) * jax.nn.sigmoid(f_ref[rows, cols])
            log_f = jnp.log(f)
            k = 1.0 - f
            v = i_ref[rows, cols]
            g = jnp.dot(cumsum_mat, log_f, precision=HIGHEST, preferred_element_type=F32)
            state_t = state_ref[h]
            o = lax.dot_general(q * jnp.exp(g), state_t, nt_dims, preferred_element_type=F32)
            o = o + jnp.dot(_hgrn2_intra_scores(q, k, g, seg_ref), v, preferred_element_type=F32)
            g_last = g[c_len - 1:c_len]
            k_s = k * jnp.exp(g_last - g)
            state_ref[h] = state_t * jnp.exp(g_last) + lax.dot_general(v, k_s, tn_dims,
                                                                       preferred_element_type=F32)
            ms = jnp.mean(o * o, axis=-1, keepdims=True)
            y = o * lax.rsqrt(ms + NORM_EPS) * nw_ref[:, cols]
            o_ref[rows, cols] = (y * _silu(g_ref[rows, cols])).astype(o_ref.dtype)
        return carry

    lax.fori_loop(0, chunks, chunk_body, 0, unroll=2)


def _hgrn2_branch(p, lb_table, norm_w, layer, batch, seq):
    m = p.shape[0]
    ts = 512
    steps = seq // ts
    depth = lb_table.shape[0]
    wblk = HG_HEADS_PER_STEP * HG_DK

    def in_spec(col0):
        return pl.BlockSpec((ts, wblk), lambda b, h, t: (b * steps + t, col0 // wblk + h))

    seg = (jnp.arange(HG_SUB * HG_DK)[:, None] // HG_DK == jnp.arange(HG_CHUNK)[None, :] % HG_SUB).astype(F32)

    return pl.pallas_call(
        functools.partial(_hgrn2_kernel, layer=layer, chunks=ts // HG_CHUNK),
        grid=(batch, HG_HEADS // HG_HEADS_PER_STEP, steps),
        in_specs=[in_spec(COL_HG_Q), in_spec(COL_HG_F), in_spec(COL_HG_I), in_spec(COL_HG_G),
                  pl.BlockSpec((depth, wblk), lambda b, h, t: (0, h)),
                  pl.BlockSpec((1, wblk), lambda b, h, t: (0, h)),
                  pl.BlockSpec((HG_SUB * HG_DK, HG_CHUNK), lambda b, h, t: (0, 0))],
        out_specs=pl.BlockSpec((ts, wblk), lambda b, h, t: (b * steps + t, h)),
        out_shape=jax.ShapeDtypeStruct((m, BRANCH_WIDTH), BF16),
        scratch_shapes=[pltpu.VMEM((HG_HEADS_PER_STEP, HG_DK, HG_DK), F32)],
        compiler_params=_params(("parallel", "parallel", "arbitrary")),
        name="hgrn2_branch",
    )(p, p, p, p, lb_table, norm_w.reshape(1, BRANCH_WIDTH), seg)


def _ssd_kernel(z_ref, x_ref, bc_ref, dt_ref, cw_ref, cb_ref, dtb_ref, alog_ref, dsk_ref, nw_ref,
                expand_ref, o_ref, tail_ref, state_ref):
    length = SSM_CHUNK
    inner = BRANCH_WIDTH
    gw = inner // SSM_GROUPS
    n = SSM_STATE

    @pl.when(pl.program_id(1) == 0)
    def _():
        tail_ref[...] = jnp.zeros_like(tail_ref)
        state_ref[...] = jnp.zeros_like(state_ref)

    xbc = jnp.concatenate([x_ref[...], bc_ref[...]], axis=1)
    ext = jnp.concatenate([tail_ref[...], xbc], axis=0)
    conv = cb_ref[...]
    for j in range(SSM_CONV):
        start = SSM_TAIL - (SSM_CONV - 1) + j
        conv = conv + ext[start:start + length] * cw_ref[j:j + 1, :]
    tail_ref[...] = xbc[length - SSM_TAIL:length]
    xbc_act = _silu(conv)
    x_c = xbc_act[:, :inner]
    b_m = xbc_act[:, inner:inner + SSM_GROUPS * n]
    c_m = xbc_act[:, inner + SSM_GROUPS * n:]

    dt = jax.nn.softplus(dt_ref[...] + dtb_ref[...])
    a = dt * (-jnp.exp(alog_ref[...]))
    row = lax.broadcasted_iota(jnp.int32, (length, length), 0)
    col = lax.broadcasted_iota(jnp.int32, (length, length), 1)
    causal = row >= col
    cs = jnp.dot(causal.astype(F32), a, precision=HIGHEST, preferred_element_type=F32)
    cs_t = cs.T
    expand = expand_ref[...]
    dt_e = jnp.dot(dt, expand, precision=HIGHEST, preferred_element_type=F32)
    cs_e = jnp.dot(cs, expand, precision=HIGHEST, preferred_element_type=F32)
    cs_last_e = cs_e[length - 1:length]
    xs = x_c * dt_e
    in_decay_e = jnp.exp(cs_e)
    state_decay_e = jnp.exp(cs_last_e - cs_e)
    chunk_decay_e = jnp.exp(cs_last_e)
    z = z_ref[...]

    nt_dims = (((1,), (1,)), ((), ()))
    tn_dims = (((0,), (0,)), ((), ()))
    heads_per_group = SSM_HEADS // SSM_GROUPS
    p = SSM_HEADDIM
    for g in range(SSM_GROUPS):
        gcols = slice(g * gw, (g + 1) * gw)
        b_g = b_m[:, g * n:(g + 1) * n]
        c_g = c_m[:, g * n:(g + 1) * n]
        cb = lax.dot_general(c_g, b_g, nt_dims, preferred_element_type=F32)
        prev_t = state_ref[g]
        y_off = jnp.dot(c_g, prev_t, preferred_element_type=F32) * in_decay_e[:, gcols]
        parts = []
        for r in range(heads_per_group):
            h = g * heads_per_group + r
            lane = DT_LANE0 + h
            decay = jnp.exp(jnp.where(causal, cs[:, lane:lane + 1] - cs_t[lane:lane + 1, :], -jnp.inf))
            parts.append(jnp.dot(cb * decay, xs[:, h * p:(h + 1) * p], preferred_element_type=F32))
        y_diag = jnp.concatenate(parts, axis=1)
        state_ref[g] = prev_t * chunk_decay_e[:, gcols] + lax.dot_general(
            b_g, xs[:, gcols] * state_decay_e[:, gcols], tn_dims, preferred_element_type=F32)
        y = (y_diag + y_off) + x_c[:, gcols] * dsk_ref[:, gcols]
        y = y * _silu(z[:, gcols])
        ms = jnp.mean(y * y, axis=-1, keepdims=True)
        o_ref[:, gcols] = (y * lax.rsqrt(ms + NORM_EPS) * nw_ref[:, gcols]).astype(o_ref.dtype)


def _ssd_branch(p_main, p_dt, conv_w, conv_b, dt_bias, a_log, d_skip, norm_w, batch, seq):
    m = p_main.shape[0]
    length = SSM_CHUNK
    steps = seq // length
    inner = BRANCH_WIDTH
    conv_dim = conv_w.shape[1]

    def pad_heads(vec):
        return jnp.pad(vec, (DT_LANE0, 0)).reshape(1, LANES)

    expand = (jnp.arange(LANES)[:, None] == (DT_LANE0 + jnp.arange(inner) // SSM_HEADDIM)[None, :]).astype(F32)
    d_skip_e = jnp.repeat(d_skip, SSM_HEADDIM).reshape(1, inner)

    def rows(width, col0):
        return pl.BlockSpec((length, width), lambda b, t: (b * steps + t, col0 // width))

    def whole(shape):
        return pl.BlockSpec(shape, lambda b, t: (0,) * len(shape))

    return pl.pallas_call(
        _ssd_kernel,
        grid=(batch, steps),
        in_specs=[rows(inner, COL_SSM_Z), rows(inner, COL_SSM_X), rows(inner, COL_SSM_BC),
                  rows(LANES, 0),
                  whole((SSM_CONV, conv_dim)), whole((1, conv_dim)),
                  whole((1, LANES)), whole((1, LANES)), whole((1, inner)), whole((1, inner)),
                  whole((LANES, inner))],
        out_specs=pl.BlockSpec((length, inner), lambda b, t: (b * steps + t, 0)),
        out_shape=jax.ShapeDtypeStruct((m, inner), BF16),
        scratch_shapes=[pltpu.VMEM((SSM_TAIL, conv_dim), F32),
                        pltpu.VMEM((SSM_GROUPS, SSM_STATE, inner // SSM_GROUPS), F32)],
        compiler_params=_params(("parallel", "arbitrary")),
        name="ssd_branch",
    )(p_main, p_main, p_main, p_dt, conv_w, conv_b.reshape(1, conv_dim), pad_heads(dt_bias), pad_heads(a_log),
      d_skip_e, norm_w.reshape(1, inner), expand)


def _swa_kernel(sink_ref, q_ref, kc_ref, kp_ref, vc_ref, vp_ref, qw_ref, kw_ref, o_ref):
    t = ATT_BLOCK
    hd = ATT_HEAD_DIM
    group = ATT_Q_HEADS // ATT_KV_HEADS
    blk = pl.program_id(1)
    q = q_ref[...]
    keys = jnp.concatenate([kp_ref[...], kc_ref[...]], axis=0)
    vals = jnp.concatenate([vp_ref[...], vc_ref[...]], axis=0)
    qi = lax.broadcasted_iota(jnp.int32, (t, 2 * t), 0)
    kj = lax.broadcasted_iota(jnp.int32, (t, 2 * t), 1)
    first_key = jnp.where(blk > 0, 0, t)
    valid = (kj > qi) & (kj <= qi + t) & (kj >= first_key)
    nt_dims = (((1,), (1,)), ((), ()))
    scale = hd ** -0.5
    outs = []
    for c in range(ATT_KV_HEADS):
        k_h = keys[:, c * hd:(c + 1) * hd]
        k_n = k_h * lax.rsqrt(jnp.mean(k_h * k_h, axis=-1, keepdims=True) + NORM_EPS) * kw_ref[...]
        v_h = vals[:, c * hd:(c + 1) * hd]
        for r in range(group):
            h = c * group + r
            q_h = q[:, h * hd:(h + 1) * hd]
            q_n = q_h * lax.rsqrt(jnp.mean(q_h * q_h, axis=-1, keepdims=True) + NORM_EPS) * qw_ref[...]
            s = lax.dot_general(q_n, k_n, nt_dims, preferred_element_type=F32) * scale
            s = jnp.where(valid, s, -jnp.inf)
            sink = sink_ref[h]
            mx = jnp.maximum(jnp.max(s, axis=-1, keepdims=True), sink)
            e = jnp.exp(s - mx)
            probs = e / (jnp.sum(e, axis=-1, keepdims=True) + jnp.exp(sink - mx))
            outs.append(jnp.dot(probs, v_h, preferred_element_type=F32))
    o_ref[...] = jnp.concatenate(outs, axis=1).astype(o_ref.dtype)


def _swa_branch(p, q_norm_w, k_norm_w, sinks, batch, seq):
    m = p.shape[0]
    t = ATT_BLOCK
    steps = seq // t
    qw = ATT_Q_HEADS * ATT_HEAD_DIM
    kvw = ATT_KV_HEADS * ATT_HEAD_DIM

    def cur(width, col0):
        return pl.BlockSpec((t, width), lambda b, i: (b * steps + i, col0 // width))

    def prev(width, col0):
        return pl.BlockSpec((t, width), lambda b, i: (b * steps + jnp.maximum(i - 1, 0), col0 // width))

    return pl.pallas_call(
        _swa_kernel,
        grid=(batch, steps),
        in_specs=[pl.BlockSpec(memory_space=pltpu.SMEM),
                  cur(qw, COL_ATT_Q), cur(kvw, COL_ATT_K), prev(kvw, COL_ATT_K),
                  cur(kvw, COL_ATT_V), prev(kvw, COL_ATT_V),
                  pl.BlockSpec((1, ATT_HEAD_DIM), lambda b, i: (0, 0)),
                  pl.BlockSpec((1, ATT_HEAD_DIM), lambda b, i: (0, 0))],
        out_specs=pl.BlockSpec((t, qw), lambda b, i: (b * steps + i, 0)),
        out_shape=jax.ShapeDtypeStruct((m, qw), BF16),
        compiler_params=_params(("parallel", "parallel")),
        name="swa_branch",
    )(sinks, p, p, p, p, p, q_norm_w.reshape(1, ATT_HEAD_DIM), k_norm_w.reshape(1, ATT_HEAD_DIM))


def _ffn(x, layer, norm_w, w_up, w_down_padded):
    h = _rmsnorm(x, norm_w[layer])
    act = _up_swiglu(h, w_up, layer)
    return _down_residual(act, w_down_padded, layer, x)


def _pad_w_down(w_down):
    return jnp.pad(w_down.astype(BF16), ((0, 0), (0, D_FF_PADDED - D_FF), (0, 0)))


def _mixer(x, layer, batch, seq, mix_norm, w_in_t, gm_ln_w, gm_ln_b, gm_w_s, gm_b_s, hg_lb_table, hg_norm,
           ssm_conv_w, ssm_conv_b, ssm_dt_bias, ssm_a_log, ssm_d, ssm_norm,
           att_q_norm, att_k_norm, att_sinks, w_branch, w_out):
    h = _rmsnorm(x, mix_norm)
    p_main = _in_projection(h, w_in_t, layer, ROW0_MAIN, D_IN_MAIN, 512, "mixer_in_projection_main")
    p_gates = _in_projection(h, w_in_t, layer, ROW0_GATES, D_IN_GATES, 512, "mixer_in_projection_gates")
    p_att = _in_projection(h, w_in_t, layer, ROW0_ATT, D_IN_ATT, 256, "mixer_in_projection_att")
    p_dt = _in_projection(h, w_in_t, layer, ROW0_DT, LANES, LANES, "mixer_in_projection_dt")
    branches = (
        _gmlp_branch(p_main, gm_ln_w, gm_ln_b, gm_w_s, gm_b_s),
        _hgrn2_branch(p_main, hg_lb_table, hg_norm, layer, batch, seq),
        _ssd_branch(p_main, p_dt, ssm_conv_w, ssm_conv_b, ssm_dt_bias, ssm_a_log, ssm_d, ssm_norm, batch, seq),
        _swa_branch(p_att, att_q_norm, att_k_norm, att_sinks, batch, seq),
    )
    merged = _merge(branches, w_branch, layer, p_gates)
    return _out_residual(merged, w_out, layer, x)


def kernel(x, ffn1_norm, ffn1_up, ffn1_down, mix_norm, w_in, gm_ln_w, gm_ln_b, gm_w_s, gm_b_s, hg_lb_table, hg_norm, ssm_conv_w, ssm_conv_b, ssm_dt_bias, ssm_a_log, ssm_d, ssm_norm, att_q_norm, att_k_norm, att_sinks, w_branch, w_out, ffn2_norm, ffn2_up, ffn2_down):
    batch, seq, d = x.shape
    depth = ffn1_norm.shape[0]
    xf = x.reshape(batch * seq, d)
    w_in_t = jnp.transpose(w_in, (0, 2, 1))
    ffn1_down_p, ffn2_down_p = _pad_w_down(ffn1_down), _pad_w_down(ffn2_down)
    w_branch_b, w_out_b = w_branch.astype(BF16), w_out.astype(BF16)
    for l in range(depth):
        xf = _ffn(xf, l, ffn1_norm, ffn1_up, ffn1_down_p)
        xf = _mixer(xf, l, batch, seq, mix_norm[l], w_in_t, gm_ln_w[l], gm_ln_b[l], gm_w_s[l], gm_b_s[l],
                    hg_lb_table, hg_norm[l], ssm_conv_w[l], ssm_conv_b[l], ssm_dt_bias[l], ssm_a_log[l],
                    ssm_d[l], ssm_norm[l], att_q_norm[l], att_k_norm[l], att_sinks[l], w_branch_b, w_out_b)
        xf = _ffn(xf, l, ffn2_norm, ffn2_up, ffn2_down_p)
    return xf.reshape(batch, seq, d)
```

```python
import functools

import jax
import jax.numpy as jnp
from jax import lax
from jax.experimental import pallas as pl
from jax.experimental.pallas import tpu as pltpu

F32 = jnp.float32
BF16 = jnp.bfloat16

NORM_EPS = 1e-6
LN_EPS = 1e-5

V7X_VMEM_BYTES = 64 * 1024 * 1024
VMEM_LIMIT_BYTES = V7X_VMEM_BYTES - 8 * 1024 * 1024
LANES = 128
SUBLANES = 8

BRANCH_WIDTH = 1024
N_BRANCH = 4
GM_GROUPS = 8
GM_CHUNK = 128
HG_HEADS = 8
HG_DK = 128
HG_CHUNK = 64
HG_SUB = 8
HG_HEADS_PER_STEP = 8
SSM_HEADS = 16
SSM_HEADDIM = 64
SSM_GROUPS = 4
SSM_STATE = 128
SSM_CONV = 4
SSM_CHUNK = 128
SSM_TAIL = 8
ATT_HEAD_DIM = 64
ATT_Q_HEADS = 16
ATT_KV_HEADS = 2
ATT_BLOCK = 128

ROW0_MAIN = 0
D_IN_MAIN = 9216
COL_GM = 0
COL_HG_Q = 2048
COL_HG_F = 3072
COL_HG_I = 4096
COL_HG_G = 5120
COL_SSM_Z = 6144
COL_SSM_X = 7168
COL_SSM_BC = 8192
DT_LANE0 = LANES - SSM_HEADS
ROW0_DT = D_IN_MAIN - DT_LANE0
ROW0_ATT = D_IN_MAIN + SSM_HEADS
D_IN_ATT = (ATT_Q_HEADS + 2 * ATT_KV_HEADS) * ATT_HEAD_DIM
COL_ATT_Q = 0
COL_ATT_K = ATT_Q_HEADS * ATT_HEAD_DIM
COL_ATT_V = COL_ATT_K + ATT_KV_HEADS * ATT_HEAD_DIM
ROW0_GATES = ROW0_ATT + D_IN_ATT
D_IN_GATES = N_BRANCH * 4096

D_FF = 11008
D_FF_PADDED = 11264


def _params(semantics):
    return pltpu.CompilerParams(dimension_semantics=semantics, vmem_limit_bytes=VMEM_LIMIT_BYTES)


def _silu(x):
    return x * jax.nn.sigmoid(x)


def _dot(a, b):
    return jnp.dot(a.astype(BF16), b.astype(BF16), preferred_element_type=F32)


def _dot_nt(a, b):
    return lax.dot_general(a.astype(BF16), b.astype(BF16), (((1,), (1,)), ((), ())), preferred_element_type=F32)


def _dot_tn(a, b):
    return lax.dot_general(a.astype(BF16), b.astype(BF16), (((0,), (0,)), ((), ())), preferred_element_type=F32)


def _split3(x):
    hi = x.astype(BF16)
    r1 = x - hi.astype(F32)
    mid = r1.astype(BF16)
    lo = (r1 - mid.astype(F32)).astype(BF16)
    return hi, mid, lo


def _dot_exact_lhs(mask, x):
    m = mask.astype(BF16)
    hi, mid, lo = _split3(x)
    return (jnp.dot(m, hi, preferred_element_type=F32) + jnp.dot(m, mid, preferred_element_type=F32)
            + jnp.dot(m, lo, preferred_element_type=F32))


def _dot_exact_rhs(x, mask):
    m = mask.astype(BF16)
    hi, mid, lo = _split3(x)
    return (jnp.dot(hi, m, preferred_element_type=F32) + jnp.dot(mid, m, preferred_element_type=F32)
            + jnp.dot(lo, m, preferred_element_type=F32))


def _gelu_exact(x):
    return 0.5 * x * (1.0 + lax.erf(x * (2.0 ** -0.5)))


def _rmsnorm_kernel(x_ref, w_ref, o_ref):
    x = x_ref[...]
    ms = jnp.mean(x * x, axis=-1, keepdims=True)
    o_ref[...] = (x * lax.rsqrt(ms + NORM_EPS) * w_ref[...]).astype(o_ref.dtype)


def _rmsnorm(x, w):
    m, d = x.shape
    bm = 256
    return pl.pallas_call(
        _rmsnorm_kernel,
        grid=(m // bm,),
        in_specs=[pl.BlockSpec((bm, d), lambda i: (i, 0)),
                  pl.BlockSpec((1, d), lambda i: (0, 0))],
        out_specs=pl.BlockSpec((bm, d), lambda i: (i, 0)),
        out_shape=jax.ShapeDtypeStruct((m, d), BF16),
        compiler_params=_params(("parallel",)),
        name="rmsnorm",
    )(x, w.reshape(1, d))


def _up_swiglu_kernel(x_ref, wa_ref, wb_ref, wd_ref, o_ref, wdo_ref, *, real_blocks, wd_real_blocks, wd_blocks):
    i = pl.program_id(0)
    j = pl.program_id(1)
    nj = pl.num_programs(1)

    @pl.when(j < real_blocks)
    def _():
        x = x_ref[...]
        a = jnp.dot(x, wa_ref[...].astype(BF16), preferred_element_type=F32)
        b = jnp.dot(x, wb_ref[...].astype(BF16), preferred_element_type=F32)
        o_ref[...] = (_silu(a) * b).astype(o_ref.dtype)

    @pl.when(j >= real_blocks)
    def _():
        o_ref[...] = jnp.zeros_like(o_ref)

    t = i * nj + j

    @pl.when(t < wd_real_blocks)
    def _():
        wdo_ref[...] = wd_ref[...].astype(wdo_ref.dtype)

    @pl.when((t >= wd_real_blocks) & (t < wd_blocks))
    def _():
        wdo_ref[...] = jnp.zeros_like(wdo_ref)


def _up_swiglu(h, w_up, w_down, layer):
    m, k = h.shape
    n_out = w_down.shape[2]
    bm, bn = 2048, 256
    real_blocks = D_FF // bn
    last = real_blocks - 1
    nj = D_FF_PADDED // bn
    wd_rows = LANES
    wd_real_blocks = D_FF // wd_rows
    wd_blocks = D_FF_PADDED // wd_rows
    assert wd_blocks <= (m // bm) * nj

    def wd_step(i, j):
        return jnp.minimum(i * nj + j, wd_blocks - 1)

    return pl.pallas_call(
        functools.partial(_up_swiglu_kernel, real_blocks=real_blocks, wd_real_blocks=wd_real_blocks,
                          wd_blocks=wd_blocks),
        grid=(m // bm, nj),
        in_specs=[pl.BlockSpec((bm, k), lambda i, j: (i, 0), pipeline_mode=pl.Buffered(1)),
                  pl.BlockSpec((None, k, bn), lambda i, j: (layer, 0, jnp.minimum(j, last))),
                  pl.BlockSpec((None, k, bn), lambda i, j: (layer, 0, real_blocks + jnp.minimum(j, last))),
                  pl.BlockSpec((None, wd_rows, n_out),
                               lambda i, j: (layer, jnp.minimum(wd_step(i, j), wd_real_blocks - 1), 0))],
        out_specs=[pl.BlockSpec((bm, bn), lambda i, j: (i, j)),
                   pl.BlockSpec((wd_rows, n_out), lambda i, j: (wd_step(i, j), 0))],
        out_shape=[jax.ShapeDtypeStruct((m, D_FF_PADDED), BF16),
                   jax.ShapeDtypeStruct((D_FF_PADDED, n_out), BF16)],
        compiler_params=_params(("arbitrary", "arbitrary")),
        name="ffn_up_swiglu",
    )(h, w_up, w_up, w_down)


def _down_residual_kernel(a_ref, w_ref, res_ref, o_ref, acc_ref, *, nk):
    k = pl.program_id(2)
    d = jnp.dot(a_ref[...], w_ref[...], preferred_element_type=F32)

    @pl.when(k == 0)
    def _():
        acc_ref[...] = d

    @pl.when(k > 0)
    def _():
        acc_ref[...] += d

    @pl.when(k == nk - 1)
    def _():
        o_ref[...] = res_ref[...] + 0.5 * acc_ref[...]


def _down_residual(act, w, res):
    m, k = act.shape
    n = w.shape[1]
    bm, bn, nk = 1024, 1024, 4
    bk = k // nk
    return pl.pallas_call(
        functools.partial(_down_residual_kernel, nk=nk),
        grid=(m // bm, n // bn, nk),
        in_specs=[pl.BlockSpec((bm, bk), lambda i, j, kk: (i, kk)),
                  pl.BlockSpec((bk, bn), lambda i, j, kk: (kk, j)),
                  pl.BlockSpec((bm, bn), lambda i, j, kk: (i, j))],
        out_specs=pl.BlockSpec((bm, bn), lambda i, j, kk: (i, j)),
        out_shape=jax.ShapeDtypeStruct((m, n), F32),
        scratch_shapes=[pltpu.VMEM((bm, bn), F32)],
        compiler_params=_params(("parallel", "parallel", "arbitrary")),
        name="ffn_down_residual",
    )(act, w, res)


def _nt_matmul_kernel(x_ref, wt_ref, o_ref):
    w = wt_ref[...].astype(BF16)
    o_ref[...] = lax.dot_general(x_ref[...], w, (((1,), (1,)), ((), ())),
                                 preferred_element_type=F32).astype(o_ref.dtype)


def _in_projection(h, w_in_t, layer, row0, n, bn, name, out_dtype=F32):
    m, k = h.shape
    bm = 2048
    return pl.pallas_call(
        _nt_matmul_kernel,
        grid=(m // bm, n // bn),
        in_specs=[pl.BlockSpec((bm, k), lambda i, j: (i, 0), pipeline_mode=pl.Buffered(1)),
                  pl.BlockSpec((None, pl.Element(bn), pl.Element(k)),
                               lambda i, j: (layer, pl.multiple_of(row0 + j * bn, SUBLANES), 0))],
        out_specs=pl.BlockSpec((bm, bn), lambda i, j: (i, j)),
        out_shape=jax.ShapeDtypeStruct((m, n), out_dtype),
        compiler_params=_params(("parallel", "arbitrary")),
        name=name,
    )(h, w_in_t)


def _out_residual_kernel(x_ref, w_ref, res_ref, o_ref):
    o_ref[...] = res_ref[...] + jnp.dot(x_ref[...], w_ref[...], preferred_element_type=F32)


def _out_residual(merged, w, layer, res):
    m, k = merged.shape
    n = w.shape[2]
    bm, bn = 1024, 512
    return pl.pallas_call(
        _out_residual_kernel,
        grid=(m // bm, n // bn),
        in_specs=[pl.BlockSpec((bm, k), lambda i, j: (i, 0)),
                  pl.BlockSpec((None, k, bn), lambda i, j: (layer, 0, j)),
                  pl.BlockSpec((bm, bn), lambda i, j: (i, j))],
        out_specs=pl.BlockSpec((bm, bn), lambda i, j: (i, j)),
        out_shape=jax.ShapeDtypeStruct((m, n), F32),
        compiler_params=_params(("parallel", "arbitrary")),
        name="mixer_out_residual",
    )(merged, w, res)


def _merge_kernel(b0_ref, b1_ref, b2_ref, b3_ref, w_ref, g0_ref, g1_ref, g2_ref, g3_ref, o_ref):
    branches = (b0_ref, b1_ref, b2_ref, b3_ref)
    gates = (g0_ref, g1_ref, g2_ref, g3_ref)
    merged = None
    for b in range(N_BRANCH):
        y = jnp.dot(branches[b][...], w_ref[b], preferred_element_type=F32)
        term = jax.nn.sigmoid(gates[b][...].astype(F32)) * y
        merged = term if merged is None else merged + term
    o_ref[...] = merged.astype(o_ref.dtype)


def _merge(branches, w_branch, layer, p_gates):
    m = p_gates.shape[0]
    _, nb, kb, n = w_branch.shape
    bm, bn = 512, 1024
    gate_blocks = n // bn

    def gate_spec(b):
        return pl.BlockSpec((bm, bn), lambda j, i: (i, b * gate_blocks + j))

    return pl.pallas_call(
        _merge_kernel,
        grid=(n // bn, m // bm),
        in_specs=[pl.BlockSpec((bm, kb), lambda j, i: (i, 0)) for _ in range(nb)]
        + [pl.BlockSpec((None, nb, kb, bn), lambda j, i: (layer, 0, 0, j))]
        + [gate_spec(b) for b in range(nb)],
        out_specs=pl.BlockSpec((bm, bn), lambda j, i: (i, j)),
        out_shape=jax.ShapeDtypeStruct((m, n), BF16),
        compiler_params=_params(("parallel", "arbitrary")),
        name="branch_merge",
    )(*branches, w_branch, p_gates, p_gates, p_gates, p_gates)


def _gmlp_kernel(p_ref, lnw_ref, lnb_ref, ws_ref, bst_ref, o_ref, *, chunks):
    t = GM_CHUNK
    width = BRANCH_WIDTH
    gw = width // GM_GROUPS
    row = lax.broadcasted_iota(jnp.int32, (t, t), 0)
    col = lax.broadcasted_iota(jnp.int32, (t, t), 1)
    causal = row >= col
    mixers = [jnp.where(causal, ws_ref[g], 0.0).astype(BF16) for g in range(GM_GROUPS)]
    for c in range(chunks):
        rows = slice(c * t, (c + 1) * t)
        act = _gelu_exact(p_ref[rows, :])
        u = act[:, :width]
        v = act[:, width:]
        mu = jnp.mean(v, axis=-1, keepdims=True)
        var = jnp.mean(jnp.square(v - mu), axis=-1, keepdims=True)
        vn = (v - mu) * lax.rsqrt(var + LN_EPS) * lnw_ref[...] + lnb_ref[...]
        for g in range(GM_GROUPS):
            cols = slice(g * gw, (g + 1) * gw)
            mixed = _dot(mixers[g], vn[:, cols]) + bst_ref[:, g:g + 1]
            o_ref[rows, cols] = (u[:, cols] * mixed).astype(o_ref.dtype)


def _gmlp_branch(p, ln_w, ln_b, w_s, b_s):
    m = p.shape[0]
    tb = 512
    width = BRANCH_WIDTH
    return pl.pallas_call(
        functools.partial(_gmlp_kernel, chunks=tb // GM_CHUNK),
        grid=(m // tb,),
        in_specs=[pl.BlockSpec((tb, 2 * width), lambda i: (i, COL_GM // (2 * width))),
                  pl.BlockSpec((1, width), lambda i: (0, 0)),
                  pl.BlockSpec((1, width), lambda i: (0, 0)),
                  pl.BlockSpec((GM_GROUPS, GM_CHUNK, GM_CHUNK), lambda i: (0, 0, 0)),
                  pl.BlockSpec((GM_CHUNK, GM_GROUPS), lambda i: (0, 0))],
        out_specs=pl.BlockSpec((tb, width), lambda i: (i, 0)),
        out_shape=jax.ShapeDtypeStruct((m, width), BF16),
        compiler_params=_params(("parallel",)),
        name="gmlp_branch",
    )(p, ln_w.reshape(1, width), ln_b.reshape(1, width), w_s, b_s.T)


def _rows_from(x, index_of_block, block_rows):
    blocks = x.shape[0] // block_rows
    parts = [jnp.broadcast_to(x[index_of_block(b):index_of_block(b) + 1], (block_rows, x.shape[1]))
             for b in range(blocks)]
    return jnp.concatenate(parts, axis=0)


def _hgrn2_intra_scores(q, k, g, seg_ref):
    c_len, dk = q.shape
    row = lax.broadcasted_iota(jnp.int32, (c_len, c_len), 0)
    col = lax.broadcasted_iota(jnp.int32, (c_len, c_len), 1)
    pos = lax.broadcasted_iota(jnp.int32, (c_len, dk), 0)
    sub = pos & (HG_SUB - 1)

    pieces = []
    for s in range(HG_SUB):
        g_s = _rows_from(g, lambda b: b * HG_SUB + s, HG_SUB)
        k_s = _rows_from(k, lambda b: b * HG_SUB + s, HG_SUB)
        decay = jnp.exp(jnp.where(sub >= s, g - g_s, -jnp.inf))
        pieces.append((q * k_s * decay).astype(BF16))
    diag = jnp.dot(jnp.concatenate(pieces, axis=1), seg_ref[...], preferred_element_type=F32)
    scores = jnp.where((row ^ col) < HG_SUB, diag, 0.0)

    width = HG_SUB
    while width < c_len:
        right = (pos & width) != 0
        g_ref = _rows_from(g, lambda b: b * 2 * width + width - 1, 2 * width)
        q_s = q * jnp.exp(jnp.where(right, g - g_ref, -jnp.inf))
        k_s = k * jnp.exp(jnp.where(right, -jnp.inf, g_ref - g))
        cross = _dot_nt(q_s, k_s)
        scores = scores + jnp.where((row ^ col) < 2 * width, cross, 0.0)
        width *= 2
    return scores


def _hgrn2_kernel(q_ref, f_ref, i_ref, g_ref, lbt_ref, nw_ref, seg_ref, o_ref, state_ref, *, layer, chunks):
    c_len = HG_CHUNK

    @pl.when(pl.program_id(2) == 0)
    def _():
        state_ref[...] = jnp.zeros_like(state_ref)

    table = lbt_ref[...]
    e = jnp.exp(table - jnp.max(table, axis=0, keepdims=True))
    probs = e / jnp.sum(e, axis=0, keepdims=True)
    lb_all = jnp.sum(probs[:layer + 1], axis=0, keepdims=True) - probs[0:1]

    row = lax.broadcasted_iota(jnp.int32, (c_len, c_len), 0)
    col = lax.broadcasted_iota(jnp.int32, (c_len, c_len), 1)
    cumsum_mat = row >= col

    def chunk_body(c, carry):
        r0 = pl.multiple_of(c * c_len, c_len)
        rows = pl.ds(r0, c_len)
        for h in range(HG_HEADS_PER_STEP):
            cols = slice(h * HG_DK, (h + 1) * HG_DK)
            lb = lb_all[:, cols]
            q = _silu(q_ref[rows, cols])
            f = lb + (1.0 - lb) * jax.nn.sigmoid(f_ref[rows, cols])
            log_f = jnp.log(f)
            k = 1.0 - f
            v = i_ref[rows, cols]
            g = _dot_exact_lhs(cumsum_mat, log_f)
            state_t = state_ref[h]
            o = _dot_nt(q * jnp.exp(g), state_t)
            o = o + _dot(_hgrn2_intra_scores(q, k, g, seg_ref), v)
            g_last = g[c_len - 1:c_len]
            k_s = k * jnp.exp(g_last - g)
            state_ref[h] = state_t * jnp.exp(g_last) + _dot_tn(v, k_s)
            ms = jnp.mean(o * o, axis=-1, keepdims=True)
            y = o * lax.rsqrt(ms + NORM_EPS) * nw_ref[:, cols]
            o_ref[rows, cols] = (y * _silu(g_ref[rows, cols])).astype(o_ref.dtype)
        return carry

    lax.fori_loop(0, chunks, chunk_body, 0, unroll=2)


def _hgrn2_branch(p, lb_table, norm_w, layer, batch, seq):
    m = p.shape[0]
    ts = 512
    steps = seq // ts
    depth = lb_table.shape[0]
    wblk = HG_HEADS_PER_STEP * HG_DK

    def in_spec(col0):
        return pl.BlockSpec((ts, wblk), lambda b, h, t: (b * steps + t, col0 // wblk + h))

    seg = (jnp.arange(HG_SUB * HG_DK)[:, None] // HG_DK == jnp.arange(HG_CHUNK)[None, :] % HG_SUB).astype(BF16)

    return pl.pallas_call(
        functools.partial(_hgrn2_kernel, layer=layer, chunks=ts // HG_CHUNK),
        grid=(batch, HG_HEADS // HG_HEADS_PER_STEP, steps),
        in_specs=[in_spec(COL_HG_Q), in_spec(COL_HG_F), in_spec(COL_HG_I), in_spec(COL_HG_G),
                  pl.BlockSpec((depth, wblk), lambda b, h, t: (0, h)),
                  pl.BlockSpec((1, wblk), lambda b, h, t: (0, h)),
                  pl.BlockSpec((HG_SUB * HG_DK, HG_CHUNK), lambda b, h, t: (0, 0))],
        out_specs=pl.BlockSpec((ts, wblk), lambda b, h, t: (b * steps + t, h)),
        out_shape=jax.ShapeDtypeStruct((m, BRANCH_WIDTH), BF16),
        scratch_shapes=[pltpu.VMEM((HG_HEADS_PER_STEP, HG_DK, HG_DK), F32)],
        compiler_params=_params(("parallel", "parallel", "arbitrary")),
        name="hgrn2_branch",
    )(p, p, p, p, lb_table, norm_w.reshape(1, BRANCH_WIDTH), seg)


def _ssd_kernel(z_ref, x_ref, bc_ref, dt_ref, cw_ref, cb_ref, dtb_ref, alog_ref, dsk_ref, nw_ref,
                expand_ref, o_ref, tail_ref, state_ref):
    length = SSM_CHUNK
    inner = BRANCH_WIDTH
    gw = inner // SSM_GROUPS
    n = SSM_STATE

    @pl.when(pl.program_id(1) == 0)
    def _():
        tail_ref[...] = jnp.zeros_like(tail_ref)
        state_ref[...] = jnp.zeros_like(state_ref)

    xbc = jnp.concatenate([x_ref[...], bc_ref[...]], axis=1)
    ext = jnp.concatenate([tail_ref[...], xbc], axis=0)
    conv = cb_ref[...]
    for j in range(SSM_CONV):
        start = SSM_TAIL - (SSM_CONV - 1) + j
        conv = conv + ext[start:start + length] * cw_ref[j:j + 1, :]
    tail_ref[...] = xbc[length - SSM_TAIL:length]
    xbc_act = _silu(conv)
    x_c = xbc_act[:, :inner]
    b_m = xbc_act[:, inner:inner + SSM_GROUPS * n]
    c_m = xbc_act[:, inner + SSM_GROUPS * n:]

    dt = jax.nn.softplus(dt_ref[...] + dtb_ref[...])
    a = dt * (-jnp.exp(alog_ref[...]))
    row = lax.broadcasted_iota(jnp.int32, (length, length), 0)
    col = lax.broadcasted_iota(jnp.int32, (length, length), 1)
    causal = row >= col
    cs = _dot_exact_lhs(causal, a)
    cs_t = cs.T
    expand = expand_ref[...]
    dt_e = _dot_exact_rhs(dt, expand)
    cs_e = _dot_exact_rhs(cs, expand)
    cs_last_e = cs_e[length - 1:length]
    xs = x_c * dt_e
    in_decay_e = jnp.exp(cs_e)
    state_decay_e = jnp.exp(cs_last_e - cs_e)
    chunk_decay_e = jnp.exp(cs_last_e)
    z = z_ref[...]

    heads_per_group = SSM_HEADS // SSM_GROUPS
    p = SSM_HEADDIM
    for g in range(SSM_GROUPS):
        gcols = slice(g * gw, (g + 1) * gw)
        b_g = b_m[:, g * n:(g + 1) * n]
        c_g = c_m[:, g * n:(g + 1) * n]
        cb = _dot_nt(c_g, b_g)
        prev_t = state_ref[g]
        y_off = _dot(c_g, prev_t) * in_decay_e[:, gcols]
        parts = []
        for r in range(heads_per_group):
            h = g * heads_per_group + r
            lane = DT_LANE0 + h
            decay = jnp.exp(jnp.where(causal, cs[:, lane:lane + 1] - cs_t[lane:lane + 1, :], -jnp.inf))
            parts.append(_dot(cb * decay, xs[:, h * p:(h + 1) * p]))
        y_diag = jnp.concatenate(parts, axis=1)
        state_ref[g] = prev_t * chunk_decay_e[:, gcols] + _dot_tn(b_g, xs[:, gcols] * state_decay_e[:, gcols])
        y = (y_diag + y_off) + x_c[:, gcols] * dsk_ref[:, gcols]
        y = y * _silu(z[:, gcols])
        ms = jnp.mean(y * y, axis=-1, keepdims=True)
        o_ref[:, gcols] = (y * lax.rsqrt(ms + NORM_EPS) * nw_ref[:, gcols]).astype(o_ref.dtype)


def _ssd_branch(p_main, p_dt, conv_w, conv_b, dt_bias, a_log, d_skip, norm_w, batch, seq):
    m = p_main.shape[0]
    length = SSM_CHUNK
    steps = seq // length
    inner = BRANCH_WIDTH
    conv_dim = conv_w.shape[1]

    def pad_heads(vec):
        return jnp.pad(vec, (DT_LANE0, 0)).reshape(1, LANES)

    expand = (jnp.arange(LANES)[:, None] == (DT_LANE0 + jnp.arange(inner) // SSM_HEADDIM)[None, :]).astype(BF16)
    d_skip_e = jnp.repeat(d_skip, SSM_HEADDIM).reshape(1, inner)

    def rows(width, col0):
        return pl.BlockSpec((length, width), lambda b, t: (b * steps + t, col0 // width))

    def whole(shape):
        return pl.BlockSpec(shape, lambda b, t: (0,) * len(shape))

    return pl.pallas_call(
        _ssd_kernel,
        grid=(batch, steps),
        in_specs=[rows(inner, COL_SSM_Z), rows(inner, COL_SSM_X), rows(inner, COL_SSM_BC),
                  rows(LANES, 0),
                  whole((SSM_CONV, conv_dim)), whole((1, conv_dim)),
                  whole((1, LANES)), whole((1, LANES)), whole((1, inner)), whole((1, inner)),
                  whole((LANES, inner))],
        out_specs=pl.BlockSpec((length, inner), lambda b, t: (b * steps + t, 0)),
        out_shape=jax.ShapeDtypeStruct((m, inner), BF16),
        scratch_shapes=[pltpu.VMEM((SSM_TAIL, conv_dim), F32),
                        pltpu.VMEM((SSM_GROUPS, SSM_STATE, inner // SSM_GROUPS), F32)],
        compiler_params=_params(("parallel", "arbitrary")),
        name="ssd_branch",
    )(p_main, p_main, p_main, p_dt, conv_w, conv_b.reshape(1, conv_dim), pad_heads(dt_bias), pad_heads(a_log),
      d_skip_e, norm_w.reshape(1, inner), expand)


def _swa_kernel(sink_ref, q_ref, kc_ref, kp_ref, vc_ref, vp_ref, qw_ref, kw_ref, seg_ref, o_ref):
    t = ATT_BLOCK
    hd = ATT_HEAD_DIM
    group = ATT_Q_HEADS // ATT_KV_HEADS
    pairs = group // 2
    blk = pl.program_id(1)
    q = q_ref[...]
    keys = jnp.concatenate([kp_ref[...], kc_ref[...]], axis=0)
    vals = jnp.concatenate([vp_ref[...], vc_ref[...]], axis=0)
    lane = lax.broadcasted_iota(jnp.int32, (2 * t, LANES), 1)
    low = lane < hd

    ssq = _dot_exact_rhs(q * q, seg_ref[...])
    q_scale = lax.rsqrt(ssq * (1.0 / hd) + NORM_EPS)

    k_sq = keys * keys
    k_ssq = jnp.where(low, jnp.sum(jnp.where(low, k_sq, 0.0), axis=-1, keepdims=True),
                      jnp.sum(jnp.where(low, 0.0, k_sq), axis=-1, keepdims=True))
    k_fold = keys * lax.rsqrt(k_ssq * (1.0 / hd) + NORM_EPS) * (kw_ref[...] * qw_ref[...] * (hd ** -0.5))

    qi = lax.broadcasted_iota(jnp.int32, (t, 2 * t), 0)
    kj = lax.broadcasted_iota(jnp.int32, (t, 2 * t), 1)
    first_key = jnp.where(blk > 0, 0, t)
    bias = jnp.where((kj > qi) & (kj <= qi + t) & (kj >= first_key), 0.0, -jnp.inf)

    out_pairs = [None] * (ATT_Q_HEADS // 2)
    for c in range(ATT_KV_HEADS):
        own_low = c % 2 == 0
        k_own = jnp.where(low if own_low else ~low, k_fold, 0.0)
        v_own = jnp.where(low if own_low else ~low, vals, 0.0)
        k_other = pltpu.roll(k_own, hd, axis=1)
        v_other = pltpu.roll(v_own, hd, axis=1)
        for parity in range(2):
            k_sel = (k_own if (parity == 0) == own_low else k_other).astype(BF16)
            v_sel = (v_own if (parity == 0) == own_low else v_other).astype(BF16)
            pair0 = c * pairs
            q_stack = jnp.concatenate([q[:, (pair0 + pp) * LANES:(pair0 + pp + 1) * LANES]
                                       for pp in range(pairs)], axis=0).astype(BF16)
            s_all = lax.dot_general(q_stack, k_sel, (((1,), (1,)), ((), ())), preferred_element_type=F32)
            e_parts, inv_parts = [], []
            for pp in range(pairs):
                h = c * group + 2 * pp + parity
                s = s_all[pp * t:(pp + 1) * t] * q_scale[:, h:h + 1] + bias
                sink = sink_ref[h]
                mx = jnp.maximum(jnp.max(s, axis=-1, keepdims=True), sink)
                e = jnp.exp(s - mx)
                e_parts.append(e)
                inv_parts.append(1.0 / (jnp.sum(e, axis=-1, keepdims=True) + jnp.exp(sink - mx)))
            pv = jnp.dot(jnp.concatenate(e_parts, axis=0).astype(BF16), v_sel, preferred_element_type=F32)
            for pp in range(pairs):
                half = pv[pp * t:(pp + 1) * t] * inv_parts[pp]
                idx = pair0 + pp
                out_pairs[idx] = half if out_pairs[idx] is None else out_pairs[idx] + half
    for idx, pair in enumerate(out_pairs):
        o_ref[:, idx * LANES:(idx + 1) * LANES] = pair.astype(o_ref.dtype)


def _swa_branch(p, q_norm_w, k_norm_w, sinks, batch, seq):
    m = p.shape[0]
    t = ATT_BLOCK
    steps = seq // t
    qw = ATT_Q_HEADS * ATT_HEAD_DIM
    kvw = ATT_KV_HEADS * ATT_HEAD_DIM

    def cur(width, col0):
        return pl.BlockSpec((t, width), lambda b, i: (b * steps + i, col0 // width))

    def prev(width, col0):
        return pl.BlockSpec((t, width), lambda b, i: (b * steps + jnp.maximum(i - 1, 0), col0 // width))

    def both_halves(w):
        return jnp.concatenate([w, w]).reshape(1, LANES)

    seg = (jnp.arange(qw)[:, None] // ATT_HEAD_DIM == jnp.arange(LANES)[None, :]).astype(BF16)

    return pl.pallas_call(
        _swa_kernel,
        grid=(batch, steps),
        in_specs=[pl.BlockSpec(memory_space=pltpu.SMEM),
                  cur(qw, COL_ATT_Q), cur(kvw, COL_ATT_K), prev(kvw, COL_ATT_K),
                  cur(kvw, COL_ATT_V), prev(kvw, COL_ATT_V),
                  pl.BlockSpec((1, LANES), lambda b, i: (0, 0)),
                  pl.BlockSpec((1, LANES), lambda b, i: (0, 0)),
                  pl.BlockSpec((qw, LANES), lambda b, i: (0, 0))],
        out_specs=pl.BlockSpec((t, qw), lambda b, i: (b * steps + i, 0)),
        out_shape=jax.ShapeDtypeStruct((m, qw), BF16),
        compiler_params=_params(("parallel", "parallel")),
        name="swa_branch",
    )(sinks, p, p, p, p, p, both_halves(q_norm_w), both_halves(k_norm_w), seg)


def _ffn(x, layer, norm_w, w_up, w_down):
    h = _rmsnorm(x, norm_w[layer])
    act, w_down_padded = _up_swiglu(h, w_up, w_down, layer)
    return _down_residual(act, w_down_padded, x)


def _mixer(x, layer, batch, seq, mix_norm, w_in_t, gm_ln_w, gm_ln_b, gm_w_s, gm_b_s, hg_lb_table, hg_norm,
           ssm_conv_w, ssm_conv_b, ssm_dt_bias, ssm_a_log, ssm_d, ssm_norm,
           att_q_norm, att_k_norm, att_sinks, w_branch, w_out):
    h = _rmsnorm(x, mix_norm)
    p_main = _in_projection(h, w_in_t, layer, ROW0_MAIN, D_IN_MAIN, 512, "mixer_in_projection_main")
    p_gates = _in_projection(h, w_in_t, layer, ROW0_GATES, D_IN_GATES, 512, "mixer_in_projection_gates", BF16)
    p_att = _in_projection(h, w_in_t, layer, ROW0_ATT, D_IN_ATT, 256, "mixer_in_projection_att")
    p_dt = _in_projection(h, w_in_t, layer, ROW0_DT, LANES, LANES, "mixer_in_projection_dt")
    branches = (
        _gmlp_branch(p_main, gm_ln_w, gm_ln_b, gm_w_s, gm_b_s),
        _hgrn2_branch(p_main, hg_lb_table, hg_norm, layer, batch, seq),
        _ssd_branch(p_main, p_dt, ssm_conv_w, ssm_conv_b, ssm_dt_bias, ssm_a_log, ssm_d, ssm_norm, batch, seq),
        _swa_branch(p_att, att_q_norm, att_k_norm, att_sinks, batch, seq),
    )
    merged = _merge(branches, w_branch, layer, p_gates)
    return _out_residual(merged, w_out, layer, x)


def kernel(x, ffn1_norm, ffn1_up, ffn1_down, mix_norm, w_in, gm_ln_w, gm_ln_b, gm_w_s, gm_b_s, hg_lb_table, hg_norm, ssm_conv_w, ssm_conv_b, ssm_dt_bias, ssm_a_log, ssm_d, ssm_norm, att_q_norm, att_k_norm, att_sinks, w_branch, w_out, ffn2_norm, ffn2_up, ffn2_down):
    batch, seq, d = x.shape
    depth = ffn1_norm.shape[0]
    xf = x.reshape(batch * seq, d)
    w_in_t = jnp.transpose(w_in, (0, 2, 1))
    w_branch_b, w_out_b = w_branch.astype(BF16), w_out.astype(BF16)
    for l in range(depth):
        xf = _ffn(xf, l, ffn1_norm, ffn1_up, ffn1_down)
        xf = _mixer(xf, l, batch, seq, mix_norm[l], w_in_t, gm_ln_w[l], gm_ln_b[l], gm_w_s[l], gm_b_s[l],
                    hg_lb_table, hg_norm[l], ssm_conv_w[l], ssm_conv_b[l], ssm_dt_bias[l], ssm_a_log[l],
                    ssm_d[l], ssm_norm[l], att_q_norm[l], att_k_norm[l], att_sinks[l], w_branch_b, w_out_b)
        xf = _ffn(xf, l, ffn2_norm, ffn2_up, ffn2_down)
    return xf.reshape(batch, seq, d)
```

```python
import functools

import jax
import jax.numpy as jnp
from jax import lax
from jax.experimental import pallas as pl
from jax.experimental.pallas import tpu as pltpu

F32 = jnp.float32
BF16 = jnp.bfloat16

NORM_EPS = 1e-6
LN_EPS = 1e-5

V7X_VMEM_BYTES = 64 * 1024 * 1024
VMEM_LIMIT_BYTES = V7X_VMEM_BYTES - 8 * 1024 * 1024
LANES = 128
SUBLANES = 8

BRANCH_WIDTH = 1024
N_BRANCH = 4
GM_GROUPS = 8
GM_CHUNK = 128
HG_HEADS = 8
HG_DK = 128
HG_CHUNK = 64
HG_SUB = 8
HG_HEADS_PER_STEP = 8
SSM_HEADS = 16
SSM_HEADDIM = 64
SSM_GROUPS = 4
SSM_STATE = 128
SSM_CONV = 4
SSM_CHUNK = 128
SSM_TAIL = 8
ATT_HEAD_DIM = 64
ATT_Q_HEADS = 16
ATT_KV_HEADS = 2
ATT_BLOCK = 128

ROW0_MAIN = 0
D_IN_MAIN = 9216
COL_GM = 0
COL_HG_Q = 2048
COL_HG_F = 3072
COL_HG_I = 4096
COL_HG_G = 5120
COL_SSM_Z = 6144
COL_SSM_X = 7168
COL_SSM_BC = 8192
DT_LANE0 = LANES - SSM_HEADS
ROW0_DT = D_IN_MAIN - DT_LANE0
ROW0_ATT = D_IN_MAIN + SSM_HEADS
D_IN_ATT = (ATT_Q_HEADS + 2 * ATT_KV_HEADS) * ATT_HEAD_DIM
COL_ATT_Q = 0
COL_ATT_K = ATT_Q_HEADS * ATT_HEAD_DIM
COL_ATT_V = COL_ATT_K + ATT_KV_HEADS * ATT_HEAD_DIM
ROW0_GATES = ROW0_ATT + D_IN_ATT
D_IN_GATES = N_BRANCH * 4096

D_FF = 11008
D_FF_PADDED = 11264


def _params(semantics):
    return pltpu.CompilerParams(dimension_semantics=semantics, vmem_limit_bytes=VMEM_LIMIT_BYTES)


def _silu(x):
    return x * jax.nn.sigmoid(x)


def _dot(a, b):
    return jnp.dot(a.astype(BF16), b.astype(BF16), preferred_element_type=F32)


def _dot_nt(a, b):
    return lax.dot_general(a.astype(BF16), b.astype(BF16), (((1,), (1,)), ((), ())), preferred_element_type=F32)


def _dot_tn(a, b):
    return lax.dot_general(a.astype(BF16), b.astype(BF16), (((0,), (0,)), ((), ())), preferred_element_type=F32)


def _split3(x):
    hi = x.astype(BF16)
    r1 = x - hi.astype(F32)
    mid = r1.astype(BF16)
    lo = (r1 - mid.astype(F32)).astype(BF16)
    return hi, mid, lo


def _dot_exact_lhs(mask, x):
    m = mask.astype(BF16)
    hi, mid, lo = _split3(x)
    return (jnp.dot(m, hi, preferred_element_type=F32) + jnp.dot(m, mid, preferred_element_type=F32)
            + jnp.dot(m, lo, preferred_element_type=F32))


def _dot_exact_rhs(x, mask):
    m = mask.astype(BF16)
    hi, mid, lo = _split3(x)
    return (jnp.dot(hi, m, preferred_element_type=F32) + jnp.dot(mid, m, preferred_element_type=F32)
            + jnp.dot(lo, m, preferred_element_type=F32))


def _gelu_exact(x):
    return 0.5 * x * (1.0 + lax.erf(x * (2.0 ** -0.5)))


def _rmsnorm_kernel(x_ref, w_ref, o_ref):
    x = x_ref[...]
    ms = jnp.mean(x * x, axis=-1, keepdims=True)
    o_ref[...] = (x * lax.rsqrt(ms + NORM_EPS) * w_ref[...]).astype(o_ref.dtype)


def _rmsnorm(x, w):
    m, d = x.shape
    bm = 256
    return pl.pallas_call(
        _rmsnorm_kernel,
        grid=(m // bm,),
        in_specs=[pl.BlockSpec((bm, d), lambda i: (i, 0)),
                  pl.BlockSpec((1, d), lambda i: (0, 0))],
        out_specs=pl.BlockSpec((bm, d), lambda i: (i, 0)),
        out_shape=jax.ShapeDtypeStruct((m, d), BF16),
        compiler_params=_params(("parallel",)),
        name="rmsnorm",
    )(x, w.reshape(1, d))


def _up_swiglu_kernel(x_ref, wa_ref, wb_ref, wd_ref, o_ref, wdo_ref, *, real_blocks, wd_real_blocks, wd_blocks):
    i = pl.program_id(0)
    j = pl.program_id(1)
    nj = pl.num_programs(1)

    @pl.when(j < real_blocks)
    def _():
        x = x_ref[...]
        a = jnp.dot(x, wa_ref[...].astype(BF16), preferred_element_type=F32)
        b = jnp.dot(x, wb_ref[...].astype(BF16), preferred_element_type=F32)
        o_ref[...] = (_silu(a) * b).astype(o_ref.dtype)

    @pl.when(j >= real_blocks)
    def _():
        o_ref[...] = jnp.zeros_like(o_ref)

    t = i * nj + j

    @pl.when(t < wd_real_blocks)
    def _():
        wdo_ref[...] = wd_ref[...].astype(wdo_ref.dtype)

    @pl.when((t >= wd_real_blocks) & (t < wd_blocks))
    def _():
        wdo_ref[...] = jnp.zeros_like(wdo_ref)


def _up_swiglu(h, w_up, w_down, layer):
    m, k = h.shape
    n_out = w_down.shape[2]
    bm, bn = 2048, 256
    real_blocks = D_FF // bn
    last = real_blocks - 1
    nj = D_FF_PADDED // bn
    wd_rows = LANES
    wd_real_blocks = D_FF // wd_rows
    wd_blocks = D_FF_PADDED // wd_rows
    assert wd_blocks <= (m // bm) * nj

    def wd_step(i, j):
        return jnp.minimum(i * nj + j, wd_blocks - 1)

    return pl.pallas_call(
        functools.partial(_up_swiglu_kernel, real_blocks=real_blocks, wd_real_blocks=wd_real_blocks,
                          wd_blocks=wd_blocks),
        grid=(m // bm, nj),
        in_specs=[pl.BlockSpec((bm, k), lambda i, j: (i, 0), pipeline_mode=pl.Buffered(1)),
                  pl.BlockSpec((None, k, bn), lambda i, j: (layer, 0, jnp.minimum(j, last))),
                  pl.BlockSpec((None, k, bn), lambda i, j: (layer, 0, real_blocks + jnp.minimum(j, last))),
                  pl.BlockSpec((None, wd_rows, n_out),
                               lambda i, j: (layer, jnp.minimum(wd_step(i, j), wd_real_blocks - 1), 0))],
        out_specs=[pl.BlockSpec((bm, bn), lambda i, j: (i, j)),
                   pl.BlockSpec((wd_rows, n_out), lambda i, j: (wd_step(i, j), 0))],
        out_shape=[jax.ShapeDtypeStruct((m, D_FF_PADDED), BF16),
                   jax.ShapeDtypeStruct((D_FF_PADDED, n_out), BF16)],
        compiler_params=_params(("arbitrary", "arbitrary")),
        name="ffn_up_swiglu",
    )(h, w_up, w_up, w_down)


def _down_residual_kernel(a_ref, w_ref, res_ref, o_ref, acc_ref, *, nk):
    k = pl.program_id(2)

    @pl.when(k == 0)
    def _():
        acc_ref[...] = jnp.zeros_like(acc_ref)

    acc_ref[...] += jnp.dot(a_ref[...], w_ref[...], preferred_element_type=F32)

    @pl.when(k == nk - 1)
    def _():
        o_ref[...] = res_ref[...] + 0.5 * acc_ref[...]


def _down_residual(act, w, res):
    m, k = act.shape
    n = w.shape[1]
    bm, bn, nk = 1024, 1024, 4
    bk = k // nk
    return pl.pallas_call(
        functools.partial(_down_residual_kernel, nk=nk),
        grid=(m // bm, n // bn, nk),
        in_specs=[pl.BlockSpec((bm, bk), lambda i, j, kk: (i, kk)),
                  pl.BlockSpec((bk, bn), lambda i, j, kk: (kk, j)),
                  pl.BlockSpec((bm, bn), lambda i, j, kk: (i, j))],
        out_specs=pl.BlockSpec((bm, bn), lambda i, j, kk: (i, j)),
        out_shape=jax.ShapeDtypeStruct((m, n), F32),
        scratch_shapes=[pltpu.VMEM((bm, bn), F32)],
        compiler_params=_params(("parallel", "parallel", "arbitrary")),
        name="ffn_down_residual",
    )(act, w, res)


def _nt_matmul_kernel(x_ref, wt_ref, *refs, n_casts):
    cast_in, o_ref, cast_out = refs[:n_casts], refs[n_casts], refs[n_casts + 1:]
    w = wt_ref[...].astype(BF16)
    o_ref[...] = lax.dot_general(x_ref[...], w, (((1,), (1,)), ((), ())),
                                 preferred_element_type=F32).astype(o_ref.dtype)
    for src, dst in zip(cast_in, cast_out):
        dst[...] = src[...].astype(dst.dtype)


def _in_projection(h, w_in_t, layer, row0, n, bn, name, out_dtype=F32, cast_weights=()):
    m, k = h.shape
    bm = 2048
    nj = n // bn
    steps = (m // bm) * nj
    cast_specs, cast_out_specs, cast_shapes = [], [], []
    for w in cast_weights:
        _, r, c = w.shape
        rows = r // steps
        assert rows * steps == r and rows % (2 * SUBLANES) == 0
        cast_specs.append(pl.BlockSpec((None, rows, c), lambda i, j: (layer, i * nj + j, 0)))
        cast_out_specs.append(pl.BlockSpec((rows, c), lambda i, j: (i * nj + j, 0)))
        cast_shapes.append(jax.ShapeDtypeStruct((r, c), BF16))
    out = pl.pallas_call(
        functools.partial(_nt_matmul_kernel, n_casts=len(cast_weights)),
        grid=(m // bm, nj),
        in_specs=[pl.BlockSpec((bm, k), lambda i, j: (i, 0), pipeline_mode=pl.Buffered(1)),
                  pl.BlockSpec((None, pl.Element(bn), pl.Element(k)),
                               lambda i, j: (layer, pl.multiple_of(row0 + j * bn, SUBLANES), 0))] + cast_specs,
        out_specs=[pl.BlockSpec((bm, bn), lambda i, j: (i, j))] + cast_out_specs,
        out_shape=[jax.ShapeDtypeStruct((m, n), out_dtype)] + cast_shapes,
        compiler_params=_params(("arbitrary", "arbitrary")),
        name=name,
    )(h, w_in_t, *cast_weights)
    return out if cast_weights else out[0]


def _out_residual_kernel(x_ref, w_ref, res_ref, o_ref):
    o_ref[...] = res_ref[...] + jnp.dot(x_ref[...], w_ref[...], preferred_element_type=F32)


def _out_residual(merged, w, res):
    m, k = merged.shape
    n = w.shape[1]
    bm, bn = 1024, 512
    return pl.pallas_call(
        _out_residual_kernel,
        grid=(m // bm, n // bn),
        in_specs=[pl.BlockSpec((bm, k), lambda i, j: (i, 0)),
                  pl.BlockSpec((k, bn), lambda i, j: (0, j)),
                  pl.BlockSpec((bm, bn), lambda i, j: (i, j))],
        out_specs=pl.BlockSpec((bm, bn), lambda i, j: (i, j)),
        out_shape=jax.ShapeDtypeStruct((m, n), F32),
        compiler_params=_params(("parallel", "arbitrary")),
        name="mixer_out_residual",
    )(merged, w, res)


def _merge_kernel(b0_ref, b1_ref, b2_ref, b3_ref, w_ref, g0_ref, g1_ref, g2_ref, g3_ref, o_ref):
    branches = (b0_ref, b1_ref, b2_ref, b3_ref)
    gates = (g0_ref, g1_ref, g2_ref, g3_ref)
    merged = None
    for b in range(N_BRANCH):
        y = jnp.dot(branches[b][...], w_ref[b], preferred_element_type=F32)
        term = jax.nn.sigmoid(gates[b][...].astype(F32)) * y
        merged = term if merged is None else merged + term
    o_ref[...] = merged.astype(o_ref.dtype)


def _merge(branches, w_branch, p_gates):
    m = p_gates.shape[0]
    nb, kb, n = w_branch.shape
    bm, bn = 512, 1024
    gate_blocks = n // bn

    def gate_spec(b):
        return pl.BlockSpec((bm, bn), lambda j, i: (i, b * gate_blocks + j))

    return pl.pallas_call(
        _merge_kernel,
        grid=(n // bn, m // bm),
        in_specs=[pl.BlockSpec((bm, kb), lambda j, i: (i, 0)) for _ in range(nb)]
        + [pl.BlockSpec((nb, kb, bn), lambda j, i: (0, 0, j))]
        + [gate_spec(b) for b in range(nb)],
        out_specs=pl.BlockSpec((bm, bn), lambda j, i: (i, j)),
        out_shape=jax.ShapeDtypeStruct((m, n), BF16),
        compiler_params=_params(("parallel", "arbitrary")),
        name="branch_merge",
    )(*branches, w_branch, p_gates, p_gates, p_gates, p_gates)


def _gmlp_kernel(p_ref, lnw_ref, lnb_ref, ws_ref, bst_ref, o_ref, *, chunks):
    t = GM_CHUNK
    width = BRANCH_WIDTH
    gw = width // GM_GROUPS
    row = lax.broadcasted_iota(jnp.int32, (t, t), 0)
    col = lax.broadcasted_iota(jnp.int32, (t, t), 1)
    causal = row >= col
    mixers = [jnp.where(causal, ws_ref[g], 0.0).astype(BF16) for g in range(GM_GROUPS)]
    for c in range(chunks):
        rows = slice(c * t, (c + 1) * t)
        act = _gelu_exact(p_ref[rows, :])
        u = act[:, :width]
        v = act[:, width:]
        mu = jnp.mean(v, axis=-1, keepdims=True)
        var = jnp.mean(jnp.square(v - mu), axis=-1, keepdims=True)
        vn = (v - mu) * lax.rsqrt(var + LN_EPS) * lnw_ref[...] + lnb_ref[...]
        for g in range(GM_GROUPS):
            cols = slice(g * gw, (g + 1) * gw)
            mixed = _dot(mixers[g], vn[:, cols]) + bst_ref[:, g:g + 1]
            o_ref[rows, cols] = (u[:, cols] * mixed).astype(o_ref.dtype)


def _gmlp_branch(p, ln_w, ln_b, w_s, b_s):
    m = p.shape[0]
    tb = 512
    width = BRANCH_WIDTH
    return pl.pallas_call(
        functools.partial(_gmlp_kernel, chunks=tb // GM_CHUNK),
        grid=(m // tb,),
        in_specs=[pl.BlockSpec((tb, 2 * width), lambda i: (i, COL_GM // (2 * width))),
                  pl.BlockSpec((1, width), lambda i: (0, 0)),
                  pl.BlockSpec((1, width), lambda i: (0, 0)),
                  pl.BlockSpec((GM_GROUPS, GM_CHUNK, GM_CHUNK), lambda i: (0, 0, 0)),
                  pl.BlockSpec((GM_CHUNK, GM_GROUPS), lambda i: (0, 0))],
        out_specs=pl.BlockSpec((tb, width), lambda i: (i, 0)),
        out_shape=jax.ShapeDtypeStruct((m, width), BF16),
        compiler_params=_params(("parallel",)),
        name="gmlp_branch",
    )(p, ln_w.reshape(1, width), ln_b.reshape(1, width), w_s, b_s.T)


def _rows_from(x, index_of_block, block_rows):
    blocks = x.shape[0] // block_rows
    parts = [jnp.broadcast_to(x[index_of_block(b):index_of_block(b) + 1], (block_rows, x.shape[1]))
             for b in range(blocks)]
    return jnp.concatenate(parts, axis=0)


def _hgrn2_intra_scores(q, k, g, seg_ref):
    c_len, dk = q.shape
    row = lax.broadcasted_iota(jnp.int32, (c_len, c_len), 0)
    col = lax.broadcasted_iota(jnp.int32, (c_len, c_len), 1)
    pos = lax.broadcasted_iota(jnp.int32, (c_len, dk), 0)
    sub = pos & (HG_SUB - 1)

    pieces = []
    for s in range(HG_SUB):
        g_s = _rows_from(g, lambda b: b * HG_SUB + s, HG_SUB)
        k_s = _rows_from(k, lambda b: b * HG_SUB + s, HG_SUB)
        decay = jnp.exp(jnp.where(sub >= s, g - g_s, -jnp.inf))
        pieces.append((q * k_s * decay).astype(BF16))
    diag = jnp.dot(jnp.concatenate(pieces, axis=1), seg_ref[...], preferred_element_type=F32)
    scores = jnp.where((row ^ col) < HG_SUB, diag, 0.0)

    width = HG_SUB
    while width < c_len:
        right = (pos & width) != 0
        g_ref = _rows_from(g, lambda b: b * 2 * width + width - 1, 2 * width)
        q_s = q * jnp.exp(jnp.where(right, g - g_ref, -jnp.inf))
        k_s = k * jnp.exp(jnp.where(right, -jnp.inf, g_ref - g))
        cross = _dot_nt(q_s, k_s)
        scores = scores + jnp.where((row ^ col) < 2 * width, cross, 0.0)
        width *= 2
    return scores


def _hgrn2_kernel(q_ref, f_ref, i_ref, g_ref, lbt_ref, nw_ref, seg_ref, o_ref, state_ref, *, layer, chunks):
    c_len = HG_CHUNK

    @pl.when(pl.program_id(2) == 0)
    def _():
        state_ref[...] = jnp.zeros_like(state_ref)

    table = lbt_ref[...]
    e = jnp.exp(table - jnp.max(table, axis=0, keepdims=True))
    probs = e / jnp.sum(e, axis=0, keepdims=True)
    lb_all = jnp.sum(probs[:layer + 1], axis=0, keepdims=True) - probs[0:1]

    row = lax.broadcasted_iota(jnp.int32, (c_len, c_len), 0)
    col = lax.broadcasted_iota(jnp.int32, (c_len, c_len), 1)
    cumsum_mat = row >= col

    def chunk_body(c, carry):
        r0 = pl.multiple_of(c * c_len, c_len)
        rows = pl.ds(r0, c_len)
        for h in range(HG_HEADS_PER_STEP):
            cols = slice(h * HG_DK, (h + 1) * HG_DK)
            lb = lb_all[:, cols]
            q = _silu(q_ref[rows, cols])
            f = lb + (1.0 - lb) * jax.nn.sigmoid(f_ref[rows, cols])
            log_f = jnp.log(f)
            k = 1.0 - f
            v = i_ref[rows, cols]
            g = _dot_exact_lhs(cumsum_mat, log_f)
            state_t = state_ref[h]
            o = _dot_nt(q * jnp.exp(g), state_t)
            o = o + _dot(_hgrn2_intra_scores(q, k, g, seg_ref), v)
            g_last = g[c_len - 1:c_len]
            k_s = k * jnp.exp(g_last - g)
            state_ref[h] = state_t * jnp.exp(g_last) + _dot_tn(v, k_s)
            ms = jnp.mean(o * o, axis=-1, keepdims=True)
            y = o * lax.rsqrt(ms + NORM_EPS) * nw_ref[:, cols]
            o_ref[rows, cols] = (y * _silu(g_ref[rows, cols])).astype(o_ref.dtype)
        return carry

    lax.fori_loop(0, chunks, chunk_body, 0, unroll=2)


def _hgrn2_branch(p, lb_table, norm_w, layer, batch, seq):
    m = p.shape[0]
    ts = 512
    steps = seq // ts
    depth = lb_table.shape[0]
    wblk = HG_HEADS_PER_STEP * HG_DK

    def in_spec(col0):
        return pl.BlockSpec((ts, wblk), lambda b, h, t: (b * steps + t, col0 // wblk + h))

    seg = (jnp.arange(HG_SUB * HG_DK)[:, None] // HG_DK == jnp.arange(HG_CHUNK)[None, :] % HG_SUB).astype(BF16)

    return pl.pallas_call(
        functools.partial(_hgrn2_kernel, layer=layer, chunks=ts // HG_CHUNK),
        grid=(batch, HG_HEADS // HG_HEADS_PER_STEP, steps),
        in_specs=[in_spec(COL_HG_Q), in_spec(COL_HG_F), in_spec(COL_HG_I), in_spec(COL_HG_G),
                  pl.BlockSpec((depth, wblk), lambda b, h, t: (0, h)),
                  pl.BlockSpec((1, wblk), lambda b, h, t: (0, h)),
                  pl.BlockSpec((HG_SUB * HG_DK, HG_CHUNK), lambda b, h, t: (0, 0))],
        out_specs=pl.BlockSpec((ts, wblk), lambda b, h, t: (b * steps + t, h)),
        out_shape=jax.ShapeDtypeStruct((m, BRANCH_WIDTH), BF16),
        scratch_shapes=[pltpu.VMEM((HG_HEADS_PER_STEP, HG_DK, HG_DK), F32)],
        compiler_params=_params(("parallel", "parallel", "arbitrary")),
        name="hgrn2_branch",
    )(p, p, p, p, lb_table, norm_w.reshape(1, BRANCH_WIDTH), seg)


def _ssd_kernel(z_ref, x_ref, bc_ref, dt_ref, cw_ref, cb_ref, dtb_ref, alog_ref, dsk_ref, nw_ref,
                expand_ref, o_ref, tail_ref, state_ref):
    length = SSM_CHUNK
    inner = BRANCH_WIDTH
    gw = inner // SSM_GROUPS
    n = SSM_STATE

    @pl.when(pl.program_id(1) == 0)
    def _():
        tail_ref[...] = jnp.zeros_like(tail_ref)
        state_ref[...] = jnp.zeros_like(state_ref)

    xbc = jnp.concatenate([x_ref[...], bc_ref[...]], axis=1)
    ext = jnp.concatenate([tail_ref[...], xbc], axis=0)
    conv = cb_ref[...]
    for j in range(SSM_CONV):
        start = SSM_TAIL - (SSM_CONV - 1) + j
        conv = conv + ext[start:start + length] * cw_ref[j:j + 1, :]
    tail_ref[...] = xbc[length - SSM_TAIL:length]
    xbc_act = _silu(conv)
    x_c = xbc_act[:, :inner]
    b_m = xbc_act[:, inner:inner + SSM_GROUPS * n]
    c_m = xbc_act[:, inner + SSM_GROUPS * n:]

    dt = jax.nn.softplus(dt_ref[...] + dtb_ref[...])
    a = dt * (-jnp.exp(alog_ref[...]))
    row = lax.broadcasted_iota(jnp.int32, (length, length), 0)
    col = lax.broadcasted_iota(jnp.int32, (length, length), 1)
    causal = row >= col
    cs = _dot_exact_lhs(causal, a)
    cs_t = cs.T
    expand = expand_ref[...]
    dt_e = _dot_exact_rhs(dt, expand)
    cs_e = _dot_exact_rhs(cs, expand)
    cs_last_e = cs_e[length - 1:length]
    xs = x_c * dt_e
    in_decay_e = jnp.exp(cs_e)
    state_decay_e = jnp.exp(cs_last_e - cs_e)
    chunk_decay_e = jnp.exp(cs_last_e)
    z = z_ref[...]

    heads_per_group = SSM_HEADS // SSM_GROUPS
    p = SSM_HEADDIM
    for g in range(SSM_GROUPS):
        gcols = slice(g * gw, (g + 1) * gw)
        b_g = b_m[:, g * n:(g + 1) * n]
        c_g = c_m[:, g * n:(g + 1) * n]
        cb = _dot_nt(c_g, b_g)
        prev_t = state_ref[g]
        y_off = _dot(c_g, prev_t) * in_decay_e[:, gcols]
        parts = []
        for r in range(heads_per_group):
            h = g * heads_per_group + r
            lane = DT_LANE0 + h
            decay = jnp.exp(jnp.where(causal, cs[:, lane:lane + 1] - cs_t[lane:lane + 1, :], -jnp.inf))
            parts.append(_dot(cb * decay, xs[:, h * p:(h + 1) * p]))
        y_diag = jnp.concatenate(parts, axis=1)
        state_ref[g] = prev_t * chunk_decay_e[:, gcols] + _dot_tn(b_g, xs[:, gcols] * state_decay_e[:, gcols])
        y = (y_diag + y_off) + x_c[:, gcols] * dsk_ref[:, gcols]
        y = y * _silu(z[:, gcols])
        ms = jnp.mean(y * y, axis=-1, keepdims=True)
        o_ref[:, gcols] = (y * lax.rsqrt(ms + NORM_EPS) * nw_ref[:, gcols]).astype(o_ref.dtype)


def _ssd_branch(p_main, p_dt, conv_w, conv_b, dt_bias, a_log, d_skip, norm_w, batch, seq):
    m = p_main.shape[0]
    length = SSM_CHUNK
    steps = seq // length
    inner = BRANCH_WIDTH
    conv_dim = conv_w.shape[1]

    def pad_heads(vec):
        return jnp.pad(vec, (DT_LANE0, 0)).reshape(1, LANES)

    expand = (jnp.arange(LANES)[:, None] == (DT_LANE0 + jnp.arange(inner) // SSM_HEADDIM)[None, :]).astype(BF16)
    d_skip_e = jnp.repeat(d_skip, SSM_HEADDIM).reshape(1, inner)

    def rows(width, col0):
        return pl.BlockSpec((length, width), lambda b, t: (b * steps + t, col0 // width))

    def whole(shape):
        return pl.BlockSpec(shape, lambda b, t: (0,) * len(shape))

    return pl.pallas_call(
        _ssd_kernel,
        grid=(batch, steps),
        in_specs=[rows(inner, COL_SSM_Z), rows(inner, COL_SSM_X), rows(inner, COL_SSM_BC),
                  rows(LANES, 0),
                  whole((SSM_CONV, conv_dim)), whole((1, conv_dim)),
                  whole((1, LANES)), whole((1, LANES)), whole((1, inner)), whole((1, inner)),
                  whole((LANES, inner))],
        out_specs=pl.BlockSpec((length, inner), lambda b, t: (b * steps + t, 0)),
        out_shape=jax.ShapeDtypeStruct((m, inner), BF16),
        scratch_shapes=[pltpu.VMEM((SSM_TAIL, conv_dim), F32),
                        pltpu.VMEM((SSM_GROUPS, SSM_STATE, inner // SSM_GROUPS), F32)],
        compiler_params=_params(("parallel", "arbitrary")),
        name="ssd_branch",
    )(p_main, p_main, p_main, p_dt, conv_w, conv_b.reshape(1, conv_dim), pad_heads(dt_bias), pad_heads(a_log),
      d_skip_e, norm_w.reshape(1, inner), expand)


def _swa_kernel(sink_ref, q_ref, kc_ref, kp_ref, vc_ref, vp_ref, qw_ref, kw_ref, seg_ref, o_ref):
    t = ATT_BLOCK
    hd = ATT_HEAD_DIM
    group = ATT_Q_HEADS // ATT_KV_HEADS
    pairs = group // 2
    blk = pl.program_id(1)
    q = q_ref[...]
    keys = jnp.concatenate([kp_ref[...], kc_ref[...]], axis=0)
    vals = jnp.concatenate([vp_ref[...], vc_ref[...]], axis=0)
    lane = lax.broadcasted_iota(jnp.int32, (2 * t, LANES), 1)
    low = lane < hd

    ssq = _dot_exact_rhs(q * q, seg_ref[...])
    q_scale = lax.rsqrt(ssq * (1.0 / hd) + NORM_EPS)

    k_sq = keys * keys
    k_ssq = jnp.where(low, jnp.sum(jnp.where(low, k_sq, 0.0), axis=-1, keepdims=True),
                      jnp.sum(jnp.where(low, 0.0, k_sq), axis=-1, keepdims=True))
    k_fold = keys * lax.rsqrt(k_ssq * (1.0 / hd) + NORM_EPS) * (kw_ref[...] * qw_ref[...] * (hd ** -0.5))

    qi = lax.broadcasted_iota(jnp.int32, (t, 2 * t), 0)
    kj = lax.broadcasted_iota(jnp.int32, (t, 2 * t), 1)
    first_key = jnp.where(blk > 0, 0, t)
    bias = jnp.where((kj > qi) & (kj <= qi + t) & (kj >= first_key), 0.0, -jnp.inf)

    out_pairs = [None] * (ATT_Q_HEADS // 2)
    for c in range(ATT_KV_HEADS):
        own_low = c % 2 == 0
        k_own = jnp.where(low if own_low else ~low, k_fold, 0.0)
        v_own = jnp.where(low if own_low else ~low, vals, 0.0)
        k_other = pltpu.roll(k_own, hd, axis=1)
        v_other = pltpu.roll(v_own, hd, axis=1)
        for parity in range(2):
            k_sel = (k_own if (parity == 0) == own_low else k_other).astype(BF16)
            v_sel = (v_own if (parity == 0) == own_low else v_other).astype(BF16)
            pair0 = c * pairs
            q_stack = jnp.concatenate([q[:, (pair0 + pp) * LANES:(pair0 + pp + 1) * LANES]
                                       for pp in range(pairs)], axis=0).astype(BF16)
            s_all = lax.dot_general(q_stack, k_sel, (((1,), (1,)), ((), ())), preferred_element_type=F32)
            e_parts, inv_parts = [], []
            for pp in range(pairs):
                h = c * group + 2 * pp + parity
                s = s_all[pp * t:(pp + 1) * t] * q_scale[:, h:h + 1] + bias
                sink = sink_ref[h]
                mx = jnp.maximum(jnp.max(s, axis=-1, keepdims=True), sink)
                e = jnp.exp(s - mx)
                e_parts.append(e)
                inv_parts.append(1.0 / (jnp.sum(e, axis=-1, keepdims=True) + jnp.exp(sink - mx)))
            pv = jnp.dot(jnp.concatenate(e_parts, axis=0).astype(BF16), v_sel, preferred_element_type=F32)
            for pp in range(pairs):
                half = pv[pp * t:(pp + 1) * t] * inv_parts[pp]
                idx = pair0 + pp
                out_pairs[idx] = half if out_pairs[idx] is None else out_pairs[idx] + half
    for idx, pair in enumerate(out_pairs):
        o_ref[:, idx * LANES:(idx + 1) * LANES] = pair.astype(o_ref.dtype)


def _swa_branch(p, q_norm_w, k_norm_w, sinks, batch, seq):
    m = p.shape[0]
    t = ATT_BLOCK
    steps = seq // t
    qw = ATT_Q_HEADS * ATT_HEAD_DIM
    kvw = ATT_KV_HEADS * ATT_HEAD_DIM

    def cur(width, col0):
        return pl.BlockSpec((t, width), lambda b, i: (b * steps + i, col0 // width))

    def prev(width, col0):
        return pl.BlockSpec((t, width), lambda b, i: (b * steps + jnp.maximum(i - 1, 0), col0 // width))

    def both_halves(w):
        return jnp.concatenate([w, w]).reshape(1, LANES)

    seg = (jnp.arange(qw)[:, None] // ATT_HEAD_DIM == jnp.arange(LANES)[None, :]).astype(BF16)

    return pl.pallas_call(
        _swa_kernel,
        grid=(batch, steps),
        in_specs=[pl.BlockSpec(memory_space=pltpu.SMEM),
                  cur(qw, COL_ATT_Q), cur(kvw, COL_ATT_K), prev(kvw, COL_ATT_K),
                  cur(kvw, COL_ATT_V), prev(kvw, COL_ATT_V),
                  pl.BlockSpec((1, LANES), lambda b, i: (0, 0)),
                  pl.BlockSpec((1, LANES), lambda b, i: (0, 0)),
                  pl.BlockSpec((qw, LANES), lambda b, i: (0, 0))],
        out_specs=pl.BlockSpec((t, qw), lambda b, i: (b * steps + i, 0)),
        out_shape=jax.ShapeDtypeStruct((m, qw), BF16),
        compiler_params=_params(("parallel", "parallel")),
        name="swa_branch",
    )(sinks, p, p, p, p, p, both_halves(q_norm_w), both_halves(k_norm_w), seg)


def _ffn(x, layer, norm_w, w_up, w_down):
    h = _rmsnorm(x, norm_w[layer])
    act, w_down_padded = _up_swiglu(h, w_up, w_down, layer)
    return _down_residual(act, w_down_padded, x)


def _mixer(x, layer, batch, seq, mix_norm, w_in_t, gm_ln_w, gm_ln_b, gm_w_s, gm_b_s, hg_lb_table, hg_norm,
           ssm_conv_w, ssm_conv_b, ssm_dt_bias, ssm_a_log, ssm_d, ssm_norm,
           att_q_norm, att_k_norm, att_sinks, w_branch, w_out):
    h = _rmsnorm(x, mix_norm)
    p_main = _in_projection(h, w_in_t, layer, ROW0_MAIN, D_IN_MAIN, 512, "mixer_in_projection_main")
    nb, kb, d = w_branch.shape[1:]
    p_gates, w_branch_b, w_out_b = _in_projection(
        h, w_in_t, layer, ROW0_GATES, D_IN_GATES, 512, "mixer_in_projection_gates", BF16,
        cast_weights=(w_branch.reshape(-1, nb * kb, d), w_out))
    p_att = _in_projection(h, w_in_t, layer, ROW0_ATT, D_IN_ATT, 256, "mixer_in_projection_att")
    p_dt = _in_projection(h, w_in_t, layer, ROW0_DT, LANES, LANES, "mixer_in_projection_dt")
    branches = (
        _gmlp_branch(p_main, gm_ln_w, gm_ln_b, gm_w_s, gm_b_s),
        _hgrn2_branch(p_main, hg_lb_table, hg_norm, layer, batch, seq),
        _ssd_branch(p_main, p_dt, ssm_conv_w, ssm_conv_b, ssm_dt_bias, ssm_a_log, ssm_d, ssm_norm, batch, seq),
        _swa_branch(p_att, att_q_norm, att_k_norm, att_sinks, batch, seq),
    )
    merged = _merge(branches, w_branch_b.reshape(nb, kb, d), p_gates)
    return _out_residual(merged, w_out_b, x)


def kernel(x, ffn1_norm, ffn1_up, ffn1_down, mix_norm, w_in, gm_ln_w, gm_ln_b, gm_w_s, gm_b_s, hg_lb_table, hg_norm, ssm_conv_w, ssm_conv_b, ssm_dt_bias, ssm_a_log, ssm_d, ssm_norm, att_q_norm, att_k_norm, att_sinks, w_branch, w_out, ffn2_norm, ffn2_up, ffn2_down):
    batch, seq, d = x.shape
    depth = ffn1_norm.shape[0]
    xf = x.reshape(batch * seq, d)
    w_in_t = jnp.transpose(w_in, (0, 2, 1))
    for l in range(depth):
        xf = _ffn(xf, l, ffn1_norm, ffn1_up, ffn1_down)
        xf = _mixer(xf, l, batch, seq, mix_norm[l], w_in_t, gm_ln_w[l], gm_ln_b[l], gm_w_s[l], gm_b_s[l],
                    hg_lb_table, hg_norm[l], ssm_conv_w[l], ssm_conv_b[l], ssm_dt_bias[l], ssm_a_log[l],
                    ssm_d[l], ssm_norm[l], att_q_norm[l], att_k_norm[l], att_sinks[l], w_branch, w_out)
        xf = _ffn(xf, l, ffn2_norm, ffn2_up, ffn2_down)
    return xf.reshape(batch, seq, d)
```

```python
import functools

import jax
import jax.numpy as jnp
from jax import lax
from jax.experimental import pallas as pl
from jax.experimental.pallas import tpu as pltpu

F32 = jnp.float32
BF16 = jnp.bfloat16

NORM_EPS = 1e-6
LN_EPS = 1e-5

V7X_VMEM_BYTES = 64 * 1024 * 1024
VMEM_LIMIT_BYTES = V7X_VMEM_BYTES - 8 * 1024 * 1024
LANES = 128
SUBLANES = 8

BRANCH_WIDTH = 1024
N_BRANCH = 4
GM_GROUPS = 8
GM_CHUNK = 128
HG_HEADS = 8
HG_DK = 128
HG_CHUNK = 64
HG_SUB = 8
HG_HEADS_PER_STEP = 8
SSM_HEADS = 16
SSM_HEADDIM = 64
SSM_GROUPS = 4
SSM_STATE = 128
SSM_CONV = 4
SSM_CHUNK = 128
SSM_TAIL = 8
ATT_HEAD_DIM = 64
ATT_Q_HEADS = 16
ATT_KV_HEADS = 2
ATT_BLOCK = 128

ROW0_MAIN = 0
D_IN_MAIN = 9216
COL_GM = 0
COL_HG_Q = 2048
COL_HG_F = 3072
COL_HG_I = 4096
COL_HG_G = 5120
COL_SSM_Z = 6144
COL_SSM_X = 7168
COL_SSM_BC = 8192
DT_LANE0 = LANES - SSM_HEADS
ROW0_DT = D_IN_MAIN - DT_LANE0
ROW0_ATT = D_IN_MAIN + SSM_HEADS
D_IN_ATT = (ATT_Q_HEADS + 2 * ATT_KV_HEADS) * ATT_HEAD_DIM
COL_ATT_Q = 0
COL_ATT_K = ATT_Q_HEADS * ATT_HEAD_DIM
COL_ATT_V = COL_ATT_K + ATT_KV_HEADS * ATT_HEAD_DIM
ROW0_GATES = ROW0_ATT + D_IN_ATT
D_IN_GATES = N_BRANCH * 4096

D_FF = 11008
D_FF_PADDED = 11264


def _params(semantics):
    return pltpu.CompilerParams(dimension_semantics=semantics, vmem_limit_bytes=VMEM_LIMIT_BYTES)


def _silu(x):
    return x * jax.nn.sigmoid(x)


def _dot(a, b):
    return jnp.dot(a.astype(BF16), b.astype(BF16), preferred_element_type=F32)


def _dot_nt(a, b):
    return lax.dot_general(a.astype(BF16), b.astype(BF16), (((1,), (1,)), ((), ())), preferred_element_type=F32)


def _dot_tn(a, b):
    return lax.dot_general(a.astype(BF16), b.astype(BF16), (((0,), (0,)), ((), ())), preferred_element_type=F32)


def _split3(x):
    hi = x.astype(BF16)
    r1 = x - hi.astype(F32)
    mid = r1.astype(BF16)
    lo = (r1 - mid.astype(F32)).astype(BF16)
    return hi, mid, lo


def _dot_exact_lhs(mask, x):
    m = mask.astype(BF16)
    hi, mid, lo = _split3(x)
    return (jnp.dot(m, hi, preferred_element_type=F32) + jnp.dot(m, mid, preferred_element_type=F32)
            + jnp.dot(m, lo, preferred_element_type=F32))


def _dot_exact_rhs(x, mask):
    m = mask.astype(BF16)
    hi, mid, lo = _split3(x)
    return (jnp.dot(hi, m, preferred_element_type=F32) + jnp.dot(mid, m, preferred_element_type=F32)
            + jnp.dot(lo, m, preferred_element_type=F32))


def _gelu_exact(x):
    return 0.5 * x * (1.0 + lax.erf(x * (2.0 ** -0.5)))


def _rmsnorm_kernel(x_ref, w_ref, o_ref):
    x = x_ref[...]
    ms = jnp.mean(x * x, axis=-1, keepdims=True)
    o_ref[...] = (x * lax.rsqrt(ms + NORM_EPS) * w_ref[...]).astype(o_ref.dtype)


def _rmsnorm(x, w):
    m, d = x.shape
    bm = 256
    return pl.pallas_call(
        _rmsnorm_kernel,
        grid=(m // bm,),
        in_specs=[pl.BlockSpec((bm, d), lambda i: (i, 0)),
                  pl.BlockSpec((1, d), lambda i: (0, 0))],
        out_specs=pl.BlockSpec((bm, d), lambda i: (i, 0)),
        out_shape=jax.ShapeDtypeStruct((m, d), BF16),
        compiler_params=_params(("parallel",)),
        name="rmsnorm",
    )(x, w.reshape(1, d))


def _up_swiglu_kernel(x_ref, wa_ref, wb_ref, wd_ref, o_ref, wdo_ref, *, real_blocks, wd_real_blocks, wd_blocks):
    i = pl.program_id(0)
    j = pl.program_id(1)
    nj = pl.num_programs(1)

    @pl.when(j < real_blocks)
    def _():
        x = x_ref[...]
        a = jnp.dot(x, wa_ref[...].astype(BF16), preferred_element_type=F32)
        b = jnp.dot(x, wb_ref[...].astype(BF16), preferred_element_type=F32)
        o_ref[...] = (_silu(a) * b).astype(o_ref.dtype)

    @pl.when(j >= real_blocks)
    def _():
        o_ref[...] = jnp.zeros_like(o_ref)

    t = i * nj + j

    @pl.when(t < wd_real_blocks)
    def _():
        wdo_ref[...] = wd_ref[...].astype(wdo_ref.dtype)

    @pl.when((t >= wd_real_blocks) & (t < wd_blocks))
    def _():
        wdo_ref[...] = jnp.zeros_like(wdo_ref)


def _up_swiglu(h, w_up, w_down, layer):
    m, k = h.shape
    n_out = w_down.shape[2]
    bm, bn = 2048, 256
    real_blocks = D_FF // bn
    last = real_blocks - 1
    nj = D_FF_PADDED // bn
    wd_rows = LANES
    wd_real_blocks = D_FF // wd_rows
    wd_blocks = D_FF_PADDED // wd_rows
    assert wd_blocks <= (m // bm) * nj

    def wd_step(i, j):
        return jnp.minimum(i * nj + j, wd_blocks - 1)

    return pl.pallas_call(
        functools.partial(_up_swiglu_kernel, real_blocks=real_blocks, wd_real_blocks=wd_real_blocks,
                          wd_blocks=wd_blocks),
        grid=(m // bm, nj),
        in_specs=[pl.BlockSpec((bm, k), lambda i, j: (i, 0), pipeline_mode=pl.Buffered(1)),
                  pl.BlockSpec((None, k, bn), lambda i, j: (layer, 0, jnp.minimum(j, last))),
                  pl.BlockSpec((None, k, bn), lambda i, j: (layer, 0, real_blocks + jnp.minimum(j, last))),
                  pl.BlockSpec((None, wd_rows, n_out),
                               lambda i, j: (layer, jnp.minimum(wd_step(i, j), wd_real_blocks - 1), 0))],
        out_specs=[pl.BlockSpec((bm, bn), lambda i, j: (i, j)),
                   pl.BlockSpec((wd_rows, n_out), lambda i, j: (wd_step(i, j), 0))],
        out_shape=[jax.ShapeDtypeStruct((m, D_FF_PADDED), BF16),
                   jax.ShapeDtypeStruct((D_FF_PADDED, n_out), BF16)],
        compiler_params=_params(("arbitrary", "arbitrary")),
        name="ffn_up_swiglu",
    )(h, w_up, w_up, w_down)


def _down_residual_kernel(a_ref, w_ref, res_ref, o_ref, acc_ref, *, nk):
    k = pl.program_id(2)

    @pl.when(k == 0)
    def _():
        acc_ref[...] = jnp.zeros_like(acc_ref)

    acc_ref[...] += jnp.dot(a_ref[...], w_ref[...], preferred_element_type=F32)

    @pl.when(k == nk - 1)
    def _():
        o_ref[...] = res_ref[...] + 0.5 * acc_ref[...]


def _down_residual(act, w, res):
    m, k = act.shape
    n = w.shape[1]
    bm, bn, nk = 1024, 1024, 4
    bk = k // nk
    return pl.pallas_call(
        functools.partial(_down_residual_kernel, nk=nk),
        grid=(m // bm, n // bn, nk),
        in_specs=[pl.BlockSpec((bm, bk), lambda i, j, kk: (i, kk)),
                  pl.BlockSpec((bk, bn), lambda i, j, kk: (kk, j)),
                  pl.BlockSpec((bm, bn), lambda i, j, kk: (i, j))],
        out_specs=pl.BlockSpec((bm, bn), lambda i, j, kk: (i, j)),
        out_shape=jax.ShapeDtypeStruct((m, n), F32),
        scratch_shapes=[pltpu.VMEM((bm, bn), F32)],
        compiler_params=_params(("parallel", "parallel", "arbitrary")),
        name="ffn_down_residual",
    )(act, w, res)


def _nt_matmul_kernel(x_ref, wt_ref, *refs, n_casts):
    cast_in, o_ref = refs[:n_casts], refs[n_casts]
    cast_out, w_scr = refs[n_casts + 1:-1], refs[-1]
    w_scr[...] = wt_ref[...].T.astype(BF16)
    o_ref[...] = jnp.dot(x_ref[...], w_scr[...], preferred_element_type=F32).astype(o_ref.dtype)
    for src, dst in zip(cast_in, cast_out):
        dst[...] = src[...].astype(dst.dtype)


def _in_projection(h, w_in_t, layer, row0, n, bn, name, out_dtype=F32, cast_weights=()):
    m, k = h.shape
    bm = 2048
    nj = n // bn
    steps = (m // bm) * nj
    cast_specs, cast_out_specs, cast_shapes = [], [], []
    for w in cast_weights:
        _, r, c = w.shape
        rows = r // steps
        assert rows * steps == r and rows % (2 * SUBLANES) == 0
        cast_specs.append(pl.BlockSpec((None, rows, c), lambda i, j: (layer, i * nj + j, 0)))
        cast_out_specs.append(pl.BlockSpec((rows, c), lambda i, j: (i * nj + j, 0)))
        cast_shapes.append(jax.ShapeDtypeStruct((r, c), BF16))
    out = pl.pallas_call(
        functools.partial(_nt_matmul_kernel, n_casts=len(cast_weights)),
        grid=(m // bm, nj),
        in_specs=[pl.BlockSpec((bm, k), lambda i, j: (i, 0), pipeline_mode=pl.Buffered(1)),
                  pl.BlockSpec((None, pl.Element(bn), pl.Element(k)),
                               lambda i, j: (layer, pl.multiple_of(row0 + j * bn, SUBLANES), 0))] + cast_specs,
        out_specs=[pl.BlockSpec((bm, bn), lambda i, j: (i, j))] + cast_out_specs,
        out_shape=[jax.ShapeDtypeStruct((m, n), out_dtype)] + cast_shapes,
        scratch_shapes=[pltpu.VMEM((k, bn), BF16)],
        compiler_params=_params(("arbitrary", "arbitrary")),
        name=name,
    )(h, w_in_t, *cast_weights)
    return out if cast_weights else out[0]


def _out_residual_kernel(x_ref, w_ref, res_ref, o_ref):
    o_ref[...] = res_ref[...] + jnp.dot(x_ref[...], w_ref[...], preferred_element_type=F32)


def _out_residual(merged, w, res):
    m, k = merged.shape
    n = w.shape[1]
    bm, bn = 1024, 1024
    return pl.pallas_call(
        _out_residual_kernel,
        grid=(m // bm, n // bn),
        in_specs=[pl.BlockSpec((bm, k), lambda i, j: (i, 0)),
                  pl.BlockSpec((k, bn), lambda i, j: (0, j)),
                  pl.BlockSpec((bm, bn), lambda i, j: (i, j))],
        out_specs=pl.BlockSpec((bm, bn), lambda i, j: (i, j)),
        out_shape=jax.ShapeDtypeStruct((m, n), F32),
        compiler_params=_params(("parallel", "arbitrary")),
        name="mixer_out_residual",
    )(merged, w, res)


def _merge_kernel(b0_ref, b1_ref, b2_ref, b3_ref, w_ref, g0_ref, g1_ref, g2_ref, g3_ref, o_ref):
    branches = (b0_ref, b1_ref, b2_ref, b3_ref)
    gates = (g0_ref, g1_ref, g2_ref, g3_ref)
    merged = None
    for b in range(N_BRANCH):
        y = jnp.dot(branches[b][...], w_ref[b], preferred_element_type=F32)
        term = jax.nn.sigmoid(gates[b][...].astype(F32)) * y
        merged = term if merged is None else merged + term
    o_ref[...] = merged.astype(o_ref.dtype)


def _merge(branches, w_branch, p_gates):
    m = p_gates.shape[0]
    nb, kb, n = w_branch.shape
    bm, bn = 512, 1024
    gate_blocks = n // bn

    def gate_spec(b):
        return pl.BlockSpec((bm, bn), lambda j, i: (i, b * gate_blocks + j))

    return pl.pallas_call(
        _merge_kernel,
        grid=(n // bn, m // bm),
        in_specs=[pl.BlockSpec((bm, kb), lambda j, i: (i, 0)) for _ in range(nb)]
        + [pl.BlockSpec((nb, kb, bn), lambda j, i: (0, 0, j))]
        + [gate_spec(b) for b in range(nb)],
        out_specs=pl.BlockSpec((bm, bn), lambda j, i: (i, j)),
        out_shape=jax.ShapeDtypeStruct((m, n), BF16),
        compiler_params=_params(("parallel", "arbitrary")),
        name="branch_merge",
    )(*branches, w_branch, p_gates, p_gates, p_gates, p_gates)


def _gmlp_kernel(p_ref, lnw_ref, lnb_ref, ws_ref, bst_ref, o_ref, *, chunks):
    t = GM_CHUNK
    width = BRANCH_WIDTH
    gw = width // GM_GROUPS
    row = lax.broadcasted_iota(jnp.int32, (t, t), 0)
    col = lax.broadcasted_iota(jnp.int32, (t, t), 1)
    causal = row >= col
    mixers = [jnp.where(causal, ws_ref[g], 0.0).astype(BF16) for g in range(GM_GROUPS)]
    for c in range(chunks):
        rows = slice(c * t, (c + 1) * t)
        act = _gelu_exact(p_ref[rows, :])
        u = act[:, :width]
        v = act[:, width:]
        mu = jnp.mean(v, axis=-1, keepdims=True)
        var = jnp.mean(jnp.square(v - mu), axis=-1, keepdims=True)
        vn = (v - mu) * lax.rsqrt(var + LN_EPS) * lnw_ref[...] + lnb_ref[...]
        for g in range(GM_GROUPS):
            cols = slice(g * gw, (g + 1) * gw)
            mixed = _dot(mixers[g], vn[:, cols]) + bst_ref[:, g:g + 1]
            o_ref[rows, cols] = (u[:, cols] * mixed).astype(o_ref.dtype)


def _gmlp_branch(p, ln_w, ln_b, w_s, b_s):
    m = p.shape[0]
    tb = 512
    width = BRANCH_WIDTH
    return pl.pallas_call(
        functools.partial(_gmlp_kernel, chunks=tb // GM_CHUNK),
        grid=(m // tb,),
        in_specs=[pl.BlockSpec((tb, 2 * width), lambda i: (i, COL_GM // (2 * width))),
                  pl.BlockSpec((1, width), lambda i: (0, 0)),
                  pl.BlockSpec((1, width), lambda i: (0, 0)),
                  pl.BlockSpec((GM_GROUPS, GM_CHUNK, GM_CHUNK), lambda i: (0, 0, 0)),
                  pl.BlockSpec((GM_CHUNK, GM_GROUPS), lambda i: (0, 0))],
        out_specs=pl.BlockSpec((tb, width), lambda i: (i, 0)),
        out_shape=jax.ShapeDtypeStruct((m, width), BF16),
        compiler_params=_params(("parallel",)),
        name="gmlp_branch",
    )(p, ln_w.reshape(1, width), ln_b.reshape(1, width), w_s, b_s.T)


def _rows_from(x, index_of_block, block_rows):
    blocks = x.shape[0] // block_rows
    parts = [jnp.broadcast_to(x[index_of_block(b):index_of_block(b) + 1], (block_rows, x.shape[1]))
             for b in range(blocks)]
    return jnp.concatenate(parts, axis=0)


def _hgrn2_intra_scores(q, k, g, seg_ref):
    c_len, dk = q.shape
    row = lax.broadcasted_iota(jnp.int32, (c_len, c_len), 0)
    col = lax.broadcasted_iota(jnp.int32, (c_len, c_len), 1)
    pos = lax.broadcasted_iota(jnp.int32, (c_len, dk), 0)
    sub = pos & (HG_SUB - 1)

    pieces = []
    for s in range(HG_SUB):
        g_s = _rows_from(g, lambda b: b * HG_SUB + s, HG_SUB)
        k_s = _rows_from(k, lambda b: b * HG_SUB + s, HG_SUB)
        decay = jnp.exp(jnp.where(sub >= s, g - g_s, -jnp.inf))
        pieces.append((q * k_s * decay).astype(BF16))
    diag = jnp.dot(jnp.concatenate(pieces, axis=1), seg_ref[...], preferred_element_type=F32)
    scores = jnp.where((row ^ col) < HG_SUB, diag, 0.0)

    width = HG_SUB
    while width < c_len:
        right = (pos & width) != 0
        g_ref = _rows_from(g, lambda b: b * 2 * width + width - 1, 2 * width)
        q_s = q * jnp.exp(jnp.where(right, g - g_ref, -jnp.inf))
        k_s = k * jnp.exp(jnp.where(right, -jnp.inf, g_ref - g))
        cross = _dot_nt(q_s, k_s)
        scores = scores + jnp.where((row ^ col) < 2 * width, cross, 0.0)
        width *= 2
    return scores


def _hgrn2_kernel(q_ref, f_ref, i_ref, g_ref, lbt_ref, nw_ref, seg_ref, o_ref, state_ref, *, layer, chunks):
    c_len = HG_CHUNK

    @pl.when(pl.program_id(2) == 0)
    def _():
        state_ref[...] = jnp.zeros_like(state_ref)

    table = lbt_ref[...]
    e = jnp.exp(table - jnp.max(table, axis=0, keepdims=True))
    probs = e / jnp.sum(e, axis=0, keepdims=True)
    lb_all = jnp.sum(probs[:layer + 1], axis=0, keepdims=True) - probs[0:1]

    row = lax.broadcasted_iota(jnp.int32, (c_len, c_len), 0)
    col = lax.broadcasted_iota(jnp.int32, (c_len, c_len), 1)
    cumsum_mat = row >= col

    def chunk_body(c, carry):
        r0 = pl.multiple_of(c * c_len, c_len)
        rows = pl.ds(r0, c_len)
        for h in range(HG_HEADS_PER_STEP):
            cols = slice(h * HG_DK, (h + 1) * HG_DK)
            lb = lb_all[:, cols]
            q = _silu(q_ref[rows, cols])
            f = lb + (1.0 - lb) * jax.nn.sigmoid(f_ref[rows, cols])
            log_f = jnp.log(f)
            k = 1.0 - f
            v = i_ref[rows, cols]
            g = _dot_exact_lhs(cumsum_mat, log_f)
            state_t = state_ref[h]
            o = _dot_nt(q * jnp.exp(g), state_t)
            o = o + _dot(_hgrn2_intra_scores(q, k, g, seg_ref), v)
            g_last = g[c_len - 1:c_len]
            k_s = k * jnp.exp(g_last - g)
            state_ref[h] = state_t * jnp.exp(g_last) + _dot_tn(v, k_s)
            ms = jnp.mean(o * o, axis=-1, keepdims=True)
            y = o * lax.rsqrt(ms + NORM_EPS) * nw_ref[:, cols]
            o_ref[rows, cols] = (y * _silu(g_ref[rows, cols])).astype(o_ref.dtype)
        return carry

    lax.fori_loop(0, chunks, chunk_body, 0, unroll=2)


def _hgrn2_branch(p, lb_table, norm_w, layer, batch, seq):
    m = p.shape[0]
    ts = 512
    steps = seq // ts
    depth = lb_table.shape[0]
    wblk = HG_HEADS_PER_STEP * HG_DK

    def in_spec(col0):
        return pl.BlockSpec((ts, wblk), lambda b, h, t: (b * steps + t, col0 // wblk + h))

    seg = (jnp.arange(HG_SUB * HG_DK)[:, None] // HG_DK == jnp.arange(HG_CHUNK)[None, :] % HG_SUB).astype(BF16)

    return pl.pallas_call(
        functools.partial(_hgrn2_kernel, layer=layer, chunks=ts // HG_CHUNK),
        grid=(batch, HG_HEADS // HG_HEADS_PER_STEP, steps),
        in_specs=[in_spec(COL_HG_Q), in_spec(COL_HG_F), in_spec(COL_HG_I), in_spec(COL_HG_G),
                  pl.BlockSpec((depth, wblk), lambda b, h, t: (0, h)),
                  pl.BlockSpec((1, wblk), lambda b, h, t: (0, h)),
                  pl.BlockSpec((HG_SUB * HG_DK, HG_CHUNK), lambda b, h, t: (0, 0))],
        out_specs=pl.BlockSpec((ts, wblk), lambda b, h, t: (b * steps + t, h)),
        out_shape=jax.ShapeDtypeStruct((m, BRANCH_WIDTH), BF16),
        scratch_shapes=[pltpu.VMEM((HG_HEADS_PER_STEP, HG_DK, HG_DK), F32)],
        compiler_params=_params(("parallel", "parallel", "arbitrary")),
        name="hgrn2_branch",
    )(p, p, p, p, lb_table, norm_w.reshape(1, BRANCH_WIDTH), seg)


def _ssd_kernel(z_ref, x_ref, bc_ref, dt_ref, cw_ref, cb_ref, dtb_ref, alog_ref, dsk_ref, nw_ref,
                expand_ref, o_ref, tail_ref, state_ref):
    length = SSM_CHUNK
    inner = BRANCH_WIDTH
    gw = inner // SSM_GROUPS
    n = SSM_STATE

    @pl.when(pl.program_id(1) == 0)
    def _():
        tail_ref[...] = jnp.zeros_like(tail_ref)
        state_ref[...] = jnp.zeros_like(state_ref)

    xbc = jnp.concatenate([x_ref[...], bc_ref[...]], axis=1)
    prev_rows = tail_ref[...]
    head_sub = lax.broadcasted_iota(jnp.int32, (SSM_TAIL, xbc.shape[1]), 0)
    conv = cb_ref[...] + xbc * cw_ref[SSM_CONV - 1:SSM_CONV, :]
    for shift in range(1, SSM_CONV):
        rolled = pltpu.roll(xbc, shift, axis=0)
        head = jnp.where(head_sub < shift, pltpu.roll(prev_rows, shift, axis=0), rolled[:SSM_TAIL])
        shifted = jnp.concatenate([head, rolled[SSM_TAIL:]], axis=0)
        conv = conv + shifted * cw_ref[SSM_CONV - 1 - shift:SSM_CONV - shift, :]
    tail_ref[...] = xbc[length - SSM_TAIL:length]
    xbc_act = _silu(conv)
    x_c = xbc_act[:, :inner]
    b_m = xbc_act[:, inner:inner + SSM_GROUPS * n]
    c_m = xbc_act[:, inner + SSM_GROUPS * n:]

    dt = jax.nn.softplus(dt_ref[...] + dtb_ref[...])
    a = dt * (-jnp.exp(alog_ref[...]))
    row = lax.broadcasted_iota(jnp.int32, (length, length), 0)
    col = lax.broadcasted_iota(jnp.int32, (length, length), 1)
    causal = row >= col
    cs = _dot_exact_lhs(causal, a)
    cs_t = cs.T
    expand = expand_ref[...]
    dt_e = _dot_exact_rhs(dt, expand)
    cs_e = _dot_exact_rhs(cs, expand)
    cs_last_e = cs_e[length - 1:length]
    xs = x_c * dt_e
    in_decay_e = jnp.exp(cs_e)
    state_decay_e = jnp.exp(cs_last_e - cs_e)
    chunk_decay_e = jnp.exp(cs_last_e)
    z = z_ref[...]

    heads_per_group = SSM_HEADS // SSM_GROUPS
    p = SSM_HEADDIM
    for g in range(SSM_GROUPS):
        gcols = slice(g * gw, (g + 1) * gw)
        b_g = b_m[:, g * n:(g + 1) * n]
        c_g = c_m[:, g * n:(g + 1) * n]
        cb = _dot_nt(c_g, b_g)
        prev_t = state_ref[g]
        y_off = _dot(c_g, prev_t) * in_decay_e[:, gcols]
        parts = []
        for r in range(heads_per_group):
            h = g * heads_per_group + r
            lane = DT_LANE0 + h
            decay = jnp.exp(jnp.where(causal, cs[:, lane:lane + 1] - cs_t[lane:lane + 1, :], -jnp.inf))
            parts.append(_dot(cb * decay, xs[:, h * p:(h + 1) * p]))
        y_diag = jnp.concatenate(parts, axis=1)
        state_ref[g] = prev_t * chunk_decay_e[:, gcols] + _dot_tn(b_g, xs[:, gcols] * state_decay_e[:, gcols])
        y = (y_diag + y_off) + x_c[:, gcols] * dsk_ref[:, gcols]
        y = y * _silu(z[:, gcols])
        ms = jnp.mean(y * y, axis=-1, keepdims=True)
        o_ref[:, gcols] = (y * lax.rsqrt(ms + NORM_EPS) * nw_ref[:, gcols]).astype(o_ref.dtype)


def _ssd_branch(p_main, p_dt, conv_w, conv_b, dt_bias, a_log, d_skip, norm_w, batch, seq):
    m = p_main.shape[0]
    length = SSM_CHUNK
    steps = seq // length
    inner = BRANCH_WIDTH
    conv_dim = conv_w.shape[1]

    def pad_heads(vec):
        return jnp.pad(vec, (DT_LANE0, 0)).reshape(1, LANES)

    expand = (jnp.arange(LANES)[:, None] == (DT_LANE0 + jnp.arange(inner) // SSM_HEADDIM)[None, :]).astype(BF16)
    d_skip_e = jnp.repeat(d_skip, SSM_HEADDIM).reshape(1, inner)

    def rows(width, col0):
        return pl.BlockSpec((length, width), lambda b, t: (b * steps + t, col0 // width))

    def whole(shape):
        return pl.BlockSpec(shape, lambda b, t: (0,) * len(shape))

    return pl.pallas_call(
        _ssd_kernel,
        grid=(batch, steps),
        in_specs=[rows(inner, COL_SSM_Z), rows(inner, COL_SSM_X), rows(inner, COL_SSM_BC),
                  rows(LANES, 0),
                  whole((SSM_CONV, conv_dim)), whole((1, conv_dim)),
                  whole((1, LANES)), whole((1, LANES)), whole((1, inner)), whole((1, inner)),
                  whole((LANES, inner))],
        out_specs=pl.BlockSpec((length, inner), lambda b, t: (b * steps + t, 0)),
        out_shape=jax.ShapeDtypeStruct((m, inner), BF16),
        scratch_shapes=[pltpu.VMEM((SSM_TAIL, conv_dim), F32),
                        pltpu.VMEM((SSM_GROUPS, SSM_STATE, inner // SSM_GROUPS), F32)],
        compiler_params=_params(("parallel", "arbitrary")),
        name="ssd_branch",
    )(p_main, p_main, p_main, p_dt, conv_w, conv_b.reshape(1, conv_dim), pad_heads(dt_bias), pad_heads(a_log),
      d_skip_e, norm_w.reshape(1, inner), expand)


def _swa_kernel(sink_ref, q_ref, kc_ref, kp_ref, vc_ref, vp_ref, qw_ref, kw_ref, seg_ref, o_ref):
    t = ATT_BLOCK
    hd = ATT_HEAD_DIM
    group = ATT_Q_HEADS // ATT_KV_HEADS
    pairs = group // 2
    blk = pl.program_id(1)
    q = q_ref[...]
    keys = jnp.concatenate([kp_ref[...], kc_ref[...]], axis=0)
    vals = jnp.concatenate([vp_ref[...], vc_ref[...]], axis=0)
    lane = lax.broadcasted_iota(jnp.int32, (2 * t, LANES), 1)
    low = lane < hd

    ssq = _dot_exact_rhs(q * q, seg_ref[...])
    q_scale = lax.rsqrt(ssq * (1.0 / hd) + NORM_EPS)

    k_sq = keys * keys
    k_ssq = jnp.where(low, jnp.sum(jnp.where(low, k_sq, 0.0), axis=-1, keepdims=True),
                      jnp.sum(jnp.where(low, 0.0, k_sq), axis=-1, keepdims=True))
    k_fold = keys * lax.rsqrt(k_ssq * (1.0 / hd) + NORM_EPS) * (kw_ref[...] * qw_ref[...] * (hd ** -0.5))

    qi = lax.broadcasted_iota(jnp.int32, (t, 2 * t), 0)
    kj = lax.broadcasted_iota(jnp.int32, (t, 2 * t), 1)
    first_key = jnp.where(blk > 0, 0, t)
    bias = jnp.where((kj > qi) & (kj <= qi + t) & (kj >= first_key), 0.0, -jnp.inf)

    out_pairs = [None] * (ATT_Q_HEADS // 2)
    for c in range(ATT_KV_HEADS):
        own_low = c % 2 == 0
        k_own = jnp.where(low if own_low else ~low, k_fold, 0.0)
        v_own = jnp.where(low if own_low else ~low, vals, 0.0)
        k_other = pltpu.roll(k_own, hd, axis=1)
        v_other = pltpu.roll(v_own, hd, axis=1)
        for parity in range(2):
            k_sel = (k_own if (parity == 0) == own_low else k_other).astype(BF16)
            v_sel = (v_own if (parity == 0) == own_low else v_other).astype(BF16)
            pair0 = c * pairs
            q_stack = jnp.concatenate([q[:, (pair0 + pp) * LANES:(pair0 + pp + 1) * LANES]
                                       for pp in range(pairs)], axis=0).astype(BF16)
            s_all = lax.dot_general(q_stack, k_sel, (((1,), (1,)), ((), ())), preferred_element_type=F32)
            e_parts, inv_parts = [], []
            for pp in range(pairs):
                h = c * group + 2 * pp + parity
                s = s_all[pp * t:(pp + 1) * t] * q_scale[:, h:h + 1] + bias
                sink = sink_ref[h]
                mx = jnp.maximum(jnp.max(s, axis=-1, keepdims=True), sink)
                e = jnp.exp(s - mx)
                e_parts.append(e)
                inv_parts.append(1.0 / (jnp.sum(e, axis=-1, keepdims=True) + jnp.exp(sink - mx)))
            pv = jnp.dot(jnp.concatenate(e_parts, axis=0).astype(BF16), v_sel, preferred_element_type=F32)
            for pp in range(pairs):
                half = pv[pp * t:(pp + 1) * t] * inv_parts[pp]
                idx = pair0 + pp
                out_pairs[idx] = half if out_pairs[idx] is None else out_pairs[idx] + half
    for idx, pair in enumerate(out_pairs):
        o_ref[:, idx * LANES:(idx + 1) * LANES] = pair.astype(o_ref.dtype)


def _swa_branch(p, q_norm_w, k_norm_w, sinks, batch, seq):
    m = p.shape[0]
    t = ATT_BLOCK
    steps = seq // t
    qw = ATT_Q_HEADS * ATT_HEAD_DIM
    kvw = ATT_KV_HEADS * ATT_HEAD_DIM

    def cur(width, col0):
        return pl.BlockSpec((t, width), lambda b, i: (b * steps + i, col0 // width))

    def prev(width, col0):
        return pl.BlockSpec((t, width), lambda b, i: (b * steps + jnp.maximum(i - 1, 0), col0 // width))

    def both_halves(w):
        return jnp.concatenate([w, w]).reshape(1, LANES)

    seg = (jnp.arange(qw)[:, None] // ATT_HEAD_DIM == jnp.arange(LANES)[None, :]).astype(BF16)

    return pl.pallas_call(
        _swa_kernel,
        grid=(batch, steps),
        in_specs=[pl.BlockSpec(memory_space=pltpu.SMEM),
                  cur(qw, COL_ATT_Q), cur(kvw, COL_ATT_K), prev(kvw, COL_ATT_K),
                  cur(kvw, COL_ATT_V), prev(kvw, COL_ATT_V),
                  pl.BlockSpec((1, LANES), lambda b, i: (0, 0)),
                  pl.BlockSpec((1, LANES), lambda b, i: (0, 0)),
                  pl.BlockSpec((qw, LANES), lambda b, i: (0, 0))],
        out_specs=pl.BlockSpec((t, qw), lambda b, i: (b * steps + i, 0)),
        out_shape=jax.ShapeDtypeStruct((m, qw), BF16),
        compiler_params=_params(("parallel", "parallel")),
        name="swa_branch",
    )(sinks, p, p, p, p, p, both_halves(q_norm_w), both_halves(k_norm_w), seg)


def _ffn(x, layer, norm_w, w_up, w_down):
    h = _rmsnorm(x, norm_w[layer])
    act, w_down_padded = _up_swiglu(h, w_up, w_down, layer)
    return _down_residual(act, w_down_padded, x)


def _mixer(x, layer, batch, seq, mix_norm, w_in_t, gm_ln_w, gm_ln_b, gm_w_s, gm_b_s, hg_lb_table, hg_norm,
           ssm_conv_w, ssm_conv_b, ssm_dt_bias, ssm_a_log, ssm_d, ssm_norm,
           att_q_norm, att_k_norm, att_sinks, w_branch, w_out):
    h = _rmsnorm(x, mix_norm)
    p_main = _in_projection(h, w_in_t, layer, ROW0_MAIN, D_IN_MAIN, 512, "mixer_in_projection_main")
    nb, kb, d = w_branch.shape[1:]
    p_gates, w_branch_b, w_out_b = _in_projection(
        h, w_in_t, layer, ROW0_GATES, D_IN_GATES, 512, "mixer_in_projection_gates", BF16,
        cast_weights=(w_branch.reshape(-1, nb * kb, d), w_out))
    p_att = _in_projection(h, w_in_t, layer, ROW0_ATT, D_IN_ATT, 256, "mixer_in_projection_att")
    p_dt = _in_projection(h, w_in_t, layer, ROW0_DT, LANES, LANES, "mixer_in_projection_dt")
    branches = (
        _gmlp_branch(p_main, gm_ln_w, gm_ln_b, gm_w_s, gm_b_s),
        _hgrn2_branch(p_main, hg_lb_table, hg_norm, layer, batch, seq),
        _ssd_branch(p_main, p_dt, ssm_conv_w, ssm_conv_b, ssm_dt_bias, ssm_a_log, ssm_d, ssm_norm, batch, seq),
        _swa_branch(p_att, att_q_norm, att_k_norm, att_sinks, batch, seq),
    )
    merged = _merge(branches, w_branch_b.reshape(nb, kb, d), p_gates)
    return _out_residual(merged, w_out_b, x)


def kernel(x, ffn1_norm, ffn1_up, ffn1_down, mix_norm, w_in, gm_ln_w, gm_ln_b, gm_w_s, gm_b_s, hg_lb_table, hg_norm, ssm_conv_w, ssm_conv_b, ssm_dt_bias, ssm_a_log, ssm_d, ssm_norm, att_q_norm, att_k_norm, att_sinks, w_branch, w_out, ffn2_norm, ffn2_up, ffn2_down):
    batch, seq, d = x.shape
    depth = ffn1_norm.shape[0]
    xf = x.reshape(batch * seq, d)
    w_in_t = jnp.transpose(w_in, (0, 2, 1))
    for l in range(depth):
        xf = _ffn(xf, l, ffn1_norm, ffn1_up, ffn1_down)
        xf = _mixer(xf, l, batch, seq, mix_norm[l], w_in_t, gm_ln_w[l], gm_ln_b[l], gm_w_s[l], gm_b_s[l],
                    hg_lb_table, hg_norm[l], ssm_conv_w[l], ssm_conv_b[l], ssm_dt_bias[l], ssm_a_log[l],
                    ssm_d[l], ssm_norm[l], att_q_norm[l], att_k_norm[l], att_sinks[l], w_branch, w_out)
        xf = _ffn(xf, l, ffn2_norm, ffn2_up, ffn2_down)
    return xf.reshape(batch, seq, d)
```

```python
import functools

import jax
import jax.numpy as jnp
from jax import lax
from jax.experimental import pallas as pl
from jax.experimental.pallas import tpu as pltpu

F32 = jnp.float32
BF16 = jnp.bfloat16

NORM_EPS = 1e-6
LN_EPS = 1e-5

V7X_VMEM_BYTES = 64 * 1024 * 1024
VMEM_LIMIT_BYTES = V7X_VMEM_BYTES - 8 * 1024 * 1024
LANES = 128
SUBLANES = 8

BRANCH_WIDTH = 1024
N_BRANCH = 4
GM_GROUPS = 8
GM_CHUNK = 128
HG_HEADS = 8
HG_DK = 128
HG_CHUNK = 64
HG_SUB = 8
HG_HEADS_PER_STEP = 8
SSM_HEADS = 16
SSM_HEADDIM = 64
SSM_GROUPS = 4
SSM_STATE = 128
SSM_CONV = 4
SSM_CHUNK = 128
SSM_TAIL = 8
ATT_HEAD_DIM = 64
ATT_Q_HEADS = 16
ATT_KV_HEADS = 2
ATT_BLOCK = 128

ROW0_MAIN = 0
D_IN_MAIN = 9216
COL_GM = 0
COL_HG_Q = 2048
COL_HG_F = 3072
COL_HG_I = 4096
COL_HG_G = 5120
COL_SSM_Z = 6144
COL_SSM_X = 7168
COL_SSM_BC = 8192
DT_LANE0 = LANES - SSM_HEADS
ROW0_DT = D_IN_MAIN - DT_LANE0
ROW0_ATT = D_IN_MAIN + SSM_HEADS
D_IN_ATT = (ATT_Q_HEADS + 2 * ATT_KV_HEADS) * ATT_HEAD_DIM
COL_ATT_Q = 0
COL_ATT_K = ATT_Q_HEADS * ATT_HEAD_DIM
COL_ATT_V = COL_ATT_K + ATT_KV_HEADS * ATT_HEAD_DIM
ROW0_GATES = ROW0_ATT + D_IN_ATT
D_IN_GATES = N_BRANCH * 4096

D_FF = 11008
D_FF_PADDED = 11264


def _params(semantics):
    return pltpu.CompilerParams(dimension_semantics=semantics, vmem_limit_bytes=VMEM_LIMIT_BYTES)


def _silu(x):
    return x * jax.nn.sigmoid(x)


def _dot(a, b):
    return jnp.dot(a.astype(BF16), b.astype(BF16), preferred_element_type=F32)


def _dot_nt(a, b):
    return lax.dot_general(a.astype(BF16), b.astype(BF16), (((1,), (1,)), ((), ())), preferred_element_type=F32)


def _dot_tn(a, b):
    return lax.dot_general(a.astype(BF16), b.astype(BF16), (((0,), (0,)), ((), ())), preferred_element_type=F32)


def _split3(x):
    hi = x.astype(BF16)
    r1 = x - hi.astype(F32)
    mid = r1.astype(BF16)
    lo = (r1 - mid.astype(F32)).astype(BF16)
    return hi, mid, lo


def _dot_exact_lhs(mask, x):
    m = mask.astype(BF16)
    hi, mid, lo = _split3(x)
    return (jnp.dot(m, hi, preferred_element_type=F32) + jnp.dot(m, mid, preferred_element_type=F32)
            + jnp.dot(m, lo, preferred_element_type=F32))


def _dot_exact_rhs(x, mask):
    m = mask.astype(BF16)
    hi, mid, lo = _split3(x)
    return (jnp.dot(hi, m, preferred_element_type=F32) + jnp.dot(mid, m, preferred_element_type=F32)
            + jnp.dot(lo, m, preferred_element_type=F32))


def _gelu_exact(x):
    return 0.5 * x * (1.0 + lax.erf(x * (2.0 ** -0.5)))


def _cast_job(w, layer, steps, step_of):
    _, r, c = w.shape
    rows = r // steps
    assert rows * steps == r and rows % (2 * SUBLANES) == 0
    return (pl.BlockSpec((None, rows, c), lambda *g: (layer, step_of(*g), 0)),
            pl.BlockSpec((rows, c), lambda *g: (step_of(*g), 0)),
            jax.ShapeDtypeStruct((r, c), BF16))


def _with_cast_job(body, n_in, n_out):
    def kernel(*refs):
        ins, src = refs[:n_in], refs[n_in]
        outs, dst = refs[n_in + 1:n_in + 1 + n_out], refs[n_in + 1 + n_out]
        body(*ins, *outs, *refs[n_in + 2 + n_out:])
        dst[...] = src[...].astype(dst.dtype)
    return kernel


def _call_with_optional_cast(body, cast, grid, step_of, in_specs, out_spec, out_shape, args, **kwargs):
    if cast is None:
        return pl.pallas_call(body, grid=grid, in_specs=in_specs, out_specs=out_spec, out_shape=out_shape,
                              **kwargs)(*args), None
    w, layer = cast
    steps = 1
    for g in grid:
        steps *= g
    src_spec, dst_spec, dst_shape = _cast_job(w, layer, steps, step_of)
    out, w_bf16 = pl.pallas_call(
        _with_cast_job(body, len(in_specs), 1), grid=grid, in_specs=list(in_specs) + [src_spec],
        out_specs=[out_spec, dst_spec], out_shape=[out_shape, dst_shape], **kwargs)(*args, w)
    return out, w_bf16


def _rmsnorm_kernel(x_ref, w_ref, o_ref):
    x = x_ref[...]
    ms = jnp.mean(x * x, axis=-1, keepdims=True)
    o_ref[...] = (x * lax.rsqrt(ms + NORM_EPS) * w_ref[...]).astype(o_ref.dtype)


def _rmsnorm(x, w):
    m, d = x.shape
    bm = 256
    return pl.pallas_call(
        _rmsnorm_kernel,
        grid=(m // bm,),
        in_specs=[pl.BlockSpec((bm, d), lambda i: (i, 0)),
                  pl.BlockSpec((1, d), lambda i: (0, 0))],
        out_specs=pl.BlockSpec((bm, d), lambda i: (i, 0)),
        out_shape=jax.ShapeDtypeStruct((m, d), BF16),
        compiler_params=_params(("parallel",)),
        name="rmsnorm",
    )(x, w.reshape(1, d))


def _up_swiglu_kernel(x_ref, wa_ref, wb_ref, wd_ref, o_ref, wdo_ref, *, real_blocks, wd_real_blocks, wd_blocks):
    i = pl.program_id(0)
    j = pl.program_id(1)
    nj = pl.num_programs(1)

    @pl.when(j < real_blocks)
    def _():
        x = x_ref[...]
        a = jnp.dot(x, wa_ref[...].astype(BF16), preferred_element_type=F32)
        b = jnp.dot(x, wb_ref[...].astype(BF16), preferred_element_type=F32)
        o_ref[...] = (_silu(a) * b).astype(o_ref.dtype)

    @pl.when(j >= real_blocks)
    def _():
        o_ref[...] = jnp.zeros_like(o_ref)

    t = i * nj + j

    @pl.when(t < wd_real_blocks)
    def _():
        wdo_ref[...] = wd_ref[...].astype(wdo_ref.dtype)

    @pl.when((t >= wd_real_blocks) & (t < wd_blocks))
    def _():
        wdo_ref[...] = jnp.zeros_like(wdo_ref)


def _up_swiglu(h, w_up, w_down, layer):
    m, k = h.shape
    n_out = w_down.shape[2]
    bm, bn = 2048, 256
    real_blocks = D_FF // bn
    last = real_blocks - 1
    nj = D_FF_PADDED // bn
    wd_rows = LANES
    wd_real_blocks = D_FF // wd_rows
    wd_blocks = D_FF_PADDED // wd_rows
    assert wd_blocks <= (m // bm) * nj

    def wd_step(i, j):
        return jnp.minimum(i * nj + j, wd_blocks - 1)

    def w_spec(first_block):
        if w_up.ndim == 3:
            return pl.BlockSpec((None, k, bn), lambda i, j: (layer, 0, first_block + jnp.minimum(j, last)))
        return pl.BlockSpec((k, bn), lambda i, j: (0, first_block + jnp.minimum(j, last)))

    return pl.pallas_call(
        functools.partial(_up_swiglu_kernel, real_blocks=real_blocks, wd_real_blocks=wd_real_blocks,
                          wd_blocks=wd_blocks),
        grid=(m // bm, nj),
        in_specs=[pl.BlockSpec((bm, k), lambda i, j: (i, 0), pipeline_mode=pl.Buffered(1)),
                  w_spec(0), w_spec(real_blocks),
                  pl.BlockSpec((None, wd_rows, n_out),
                               lambda i, j: (layer, jnp.minimum(wd_step(i, j), wd_real_blocks - 1), 0))],
        out_specs=[pl.BlockSpec((bm, bn), lambda i, j: (i, j)),
                   pl.BlockSpec((wd_rows, n_out), lambda i, j: (wd_step(i, j), 0))],
        out_shape=[jax.ShapeDtypeStruct((m, D_FF_PADDED), BF16),
                   jax.ShapeDtypeStruct((D_FF_PADDED, n_out), BF16)],
        compiler_params=_params(("arbitrary", "arbitrary")),
        name="ffn_up_swiglu",
    )(h, w_up, w_up, w_down)


def _down_residual_kernel(a_ref, w_ref, res_ref, o_ref, acc_ref, *, nk):
    k = pl.program_id(2)

    @pl.when(k == 0)
    def _():
        acc_ref[...] = jnp.zeros_like(acc_ref)

    acc_ref[...] += jnp.dot(a_ref[...], w_ref[...], preferred_element_type=F32)

    @pl.when(k == nk - 1)
    def _():
        o_ref[...] = res_ref[...] + 0.5 * acc_ref[...]


def _down_residual(act, w, res):
    m, k = act.shape
    n = w.shape[1]
    bm, bn, nk = 1024, 1024, 4
    bk = k // nk
    return pl.pallas_call(
        functools.partial(_down_residual_kernel, nk=nk),
        grid=(m // bm, n // bn, nk),
        in_specs=[pl.BlockSpec((bm, bk), lambda i, j, kk: (i, kk)),
                  pl.BlockSpec((bk, bn), lambda i, j, kk: (kk, j)),
                  pl.BlockSpec((bm, bn), lambda i, j, kk: (i, j))],
        out_specs=pl.BlockSpec((bm, bn), lambda i, j, kk: (i, j)),
        out_shape=jax.ShapeDtypeStruct((m, n), F32),
        scratch_shapes=[pltpu.VMEM((bm, bn), F32)],
        compiler_params=_params(("parallel", "parallel", "arbitrary")),
        name="ffn_down_residual",
    )(act, w, res)


def _nt_matmul_kernel(x_ref, wt_ref, *refs, n_casts):
    cast_in, o_ref, cast_out = refs[:n_casts], refs[n_casts], refs[n_casts + 1:]
    w = wt_ref[...].astype(BF16)
    o_ref[...] = lax.dot_general(x_ref[...], w, (((1,), (1,)), ((), ())),
                                 preferred_element_type=F32).astype(o_ref.dtype)
    for src, dst in zip(cast_in, cast_out):
        dst[...] = src[...].astype(dst.dtype)


def _in_projection(h, w_in_t, layer, row0, n, bn, name, out_dtype=F32, cast_weights=()):
    m, k = h.shape
    bm = 2048
    nj = n // bn
    steps = (m // bm) * nj
    cast_specs, cast_out_specs, cast_shapes = [], [], []
    for w in cast_weights:
        _, r, c = w.shape
        rows = r // steps
        assert rows * steps == r and rows % (2 * SUBLANES) == 0
        cast_specs.append(pl.BlockSpec((None, rows, c), lambda i, j: (layer, i * nj + j, 0)))
        cast_out_specs.append(pl.BlockSpec((rows, c), lambda i, j: (i * nj + j, 0)))
        cast_shapes.append(jax.ShapeDtypeStruct((r, c), BF16))
    out = pl.pallas_call(
        functools.partial(_nt_matmul_kernel, n_casts=len(cast_weights)),
        grid=(m // bm, nj),
        in_specs=[pl.BlockSpec((bm, k), lambda i, j: (i, 0), pipeline_mode=pl.Buffered(1)),
                  pl.BlockSpec((None, pl.Element(bn), pl.Element(k)),
                               lambda i, j: (layer, pl.multiple_of(row0 + j * bn, SUBLANES), 0))] + cast_specs,
        out_specs=[pl.BlockSpec((bm, bn), lambda i, j: (i, j))] + cast_out_specs,
        out_shape=[jax.ShapeDtypeStruct((m, n), out_dtype)] + cast_shapes,
        compiler_params=_params(("arbitrary", "arbitrary")),
        name=name,
    )(h, w_in_t, *cast_weights)
    return out if cast_weights else out[0]


def _out_residual_kernel(x_ref, w_ref, res_ref, o_ref):
    o_ref[...] = res_ref[...] + jnp.dot(x_ref[...], w_ref[...], preferred_element_type=F32)


def _out_residual(merged, w, res):
    m, k = merged.shape
    n = w.shape[1]
    bm, bn = 1024, 1024
    return pl.pallas_call(
        _out_residual_kernel,
        grid=(m // bm, n // bn),
        in_specs=[pl.BlockSpec((bm, k), lambda i, j: (i, 0)),
                  pl.BlockSpec((k, bn), lambda i, j: (0, j)),
                  pl.BlockSpec((bm, bn), lambda i, j: (i, j))],
        out_specs=pl.BlockSpec((bm, bn), lambda i, j: (i, j)),
        out_shape=jax.ShapeDtypeStruct((m, n), F32),
        compiler_params=_params(("parallel", "arbitrary")),
        name="mixer_out_residual",
    )(merged, w, res)


def _merge_kernel(b0_ref, b1_ref, b2_ref, b3_ref, w_ref, g0_ref, g1_ref, g2_ref, g3_ref, o_ref):
    branches = (b0_ref, b1_ref, b2_ref, b3_ref)
    gates = (g0_ref, g1_ref, g2_ref, g3_ref)
    merged = None
    for b in range(N_BRANCH):
        y = jnp.dot(branches[b][...], w_ref[b], preferred_element_type=F32)
        term = jax.nn.sigmoid(gates[b][...].astype(F32)) * y
        merged = term if merged is None else merged + term
    o_ref[...] = merged.astype(o_ref.dtype)


def _merge(branches, w_branch, p_gates):
    m = p_gates.shape[0]
    nb, kb, n = w_branch.shape
    bm, bn = 512, 1024
    gate_blocks = n // bn

    def gate_spec(b):
        return pl.BlockSpec((bm, bn), lambda j, i: (i, b * gate_blocks + j))

    return pl.pallas_call(
        _merge_kernel,
        grid=(n // bn, m // bm),
        in_specs=[pl.BlockSpec((bm, kb), lambda j, i: (i, 0)) for _ in range(nb)]
        + [pl.BlockSpec((nb, kb, bn), lambda j, i: (0, 0, j))]
        + [gate_spec(b) for b in range(nb)],
        out_specs=pl.BlockSpec((bm, bn), lambda j, i: (i, j)),
        out_shape=jax.ShapeDtypeStruct((m, n), BF16),
        compiler_params=_params(("parallel", "arbitrary")),
        name="branch_merge",
    )(*branches, w_branch, p_gates, p_gates, p_gates, p_gates)


def _gmlp_kernel(p_ref, lnw_ref, lnb_ref, ws_ref, bst_ref, o_ref, *, chunks):
    t = GM_CHUNK
    width = BRANCH_WIDTH
    gw = width // GM_GROUPS
    row = lax.broadcasted_iota(jnp.int32, (t, t), 0)
    col = lax.broadcasted_iota(jnp.int32, (t, t), 1)
    causal = row >= col
    mixers = [jnp.where(causal, ws_ref[g], 0.0).astype(BF16) for g in range(GM_GROUPS)]
    for c in range(chunks):
        rows = slice(c * t, (c + 1) * t)
        act = _gelu_exact(p_ref[rows, :])
        u = act[:, :width]
        v = act[:, width:]
        mu = jnp.mean(v, axis=-1, keepdims=True)
        var = jnp.mean(jnp.square(v - mu), axis=-1, keepdims=True)
        vn = (v - mu) * lax.rsqrt(var + LN_EPS) * lnw_ref[...] + lnb_ref[...]
        for g in range(GM_GROUPS):
            cols = slice(g * gw, (g + 1) * gw)
            mixed = _dot(mixers[g], vn[:, cols]) + bst_ref[:, g:g + 1]
            o_ref[rows, cols] = (u[:, cols] * mixed).astype(o_ref.dtype)


def _gmlp_branch(p, ln_w, ln_b, w_s, b_s):
    m = p.shape[0]
    tb = 512
    width = BRANCH_WIDTH
    return pl.pallas_call(
        functools.partial(_gmlp_kernel, chunks=tb // GM_CHUNK),
        grid=(m // tb,),
        in_specs=[pl.BlockSpec((tb, 2 * width), lambda i: (i, COL_GM // (2 * width))),
                  pl.BlockSpec((1, width), lambda i: (0, 0)),
                  pl.BlockSpec((1, width), lambda i: (0, 0)),
                  pl.BlockSpec((GM_GROUPS, GM_CHUNK, GM_CHUNK), lambda i: (0, 0, 0)),
                  pl.BlockSpec((GM_CHUNK, GM_GROUPS), lambda i: (0, 0))],
        out_specs=pl.BlockSpec((tb, width), lambda i: (i, 0)),
        out_shape=jax.ShapeDtypeStruct((m, width), BF16),
        compiler_params=_params(("parallel",)),
        name="gmlp_branch",
    )(p, ln_w.reshape(1, width), ln_b.reshape(1, width), w_s, b_s.T)


def _rows_from(x, index_of_block, block_rows):
    blocks = x.shape[0] // block_rows
    parts = [jnp.broadcast_to(x[index_of_block(b):index_of_block(b) + 1], (block_rows, x.shape[1]))
             for b in range(blocks)]
    return jnp.concatenate(parts, axis=0)


def _hgrn2_intra_scores(q, k, g, seg_ref):
    c_len, dk = q.shape
    row = lax.broadcasted_iota(jnp.int32, (c_len, c_len), 0)
    col = lax.broadcasted_iota(jnp.int32, (c_len, c_len), 1)
    pos = lax.broadcasted_iota(jnp.int32, (c_len, dk), 0)
    sub = pos & (HG_SUB - 1)

    pieces = []
    for s in range(HG_SUB):
        g_s = _rows_from(g, lambda b: b * HG_SUB + s, HG_SUB)
        k_s = _rows_from(k, lambda b: b * HG_SUB + s, HG_SUB)
        decay = jnp.exp(jnp.where(sub >= s, g - g_s, -jnp.inf))
        pieces.append((q * k_s * decay).astype(BF16))
    diag = jnp.dot(jnp.concatenate(pieces, axis=1), seg_ref[...], preferred_element_type=F32)
    scores = jnp.where((row ^ col) < HG_SUB, diag, 0.0)

    width = HG_SUB
    while width < c_len:
        right = (pos & width) != 0
        g_ref = _rows_from(g, lambda b: b * 2 * width + width - 1, 2 * width)
        q_s = q * jnp.exp(jnp.where(right, g - g_ref, -jnp.inf))
        k_s = k * jnp.exp(jnp.where(right, -jnp.inf, g_ref - g))
        cross = _dot_nt(q_s, k_s)
        scores = scores + jnp.where((row ^ col) < 2 * width, cross, 0.0)
        width *= 2
    return scores


def _hgrn2_kernel(q_ref, f_ref, i_ref, g_ref, lbt_ref, nw_ref, seg_ref, o_ref, state_ref, *, layer, chunks):
    c_len = HG_CHUNK

    @pl.when(pl.program_id(2) == 0)
    def _():
        state_ref[...] = jnp.zeros_like(state_ref)

    table = lbt_ref[...]
    e = jnp.exp(table - jnp.max(table, axis=0, keepdims=True))
    probs = e / jnp.sum(e, axis=0, keepdims=True)
    lb_all = jnp.sum(probs[:layer + 1], axis=0, keepdims=True) - probs[0:1]

    row = lax.broadcasted_iota(jnp.int32, (c_len, c_len), 0)
    col = lax.broadcasted_iota(jnp.int32, (c_len, c_len), 1)
    cumsum_mat = row >= col

    def chunk_body(c, carry):
        r0 = pl.multiple_of(c * c_len, c_len)
        rows = pl.ds(r0, c_len)
        for h in range(HG_HEADS_PER_STEP):
            cols = slice(h * HG_DK, (h + 1) * HG_DK)
            lb = lb_all[:, cols]
            q = _silu(q_ref[rows, cols])
            f = lb + (1.0 - lb) * jax.nn.sigmoid(f_ref[rows, cols])
            log_f = jnp.log(f)
            k = 1.0 - f
            v = i_ref[rows, cols]
            g = _dot_exact_lhs(cumsum_mat, log_f)
            state_t = state_ref[h]
            o = _dot_nt(q * jnp.exp(g), state_t)
            o = o + _dot(_hgrn2_intra_scores(q, k, g, seg_ref), v)
            g_last = g[c_len - 1:c_len]
            k_s = k * jnp.exp(g_last - g)
            state_ref[h] = state_t * jnp.exp(g_last) + _dot_tn(v, k_s)
            ms = jnp.mean(o * o, axis=-1, keepdims=True)
            y = o * lax.rsqrt(ms + NORM_EPS) * nw_ref[:, cols]
            o_ref[rows, cols] = (y * _silu(g_ref[rows, cols])).astype(o_ref.dtype)
        return carry

    lax.fori_loop(0, chunks, chunk_body, 0, unroll=2)


def _hgrn2_branch(p, lb_table, norm_w, layer, batch, seq):
    m = p.shape[0]
    ts = 512
    steps = seq // ts
    depth = lb_table.shape[0]
    wblk = HG_HEADS_PER_STEP * HG_DK

    def in_spec(col0):
        return pl.BlockSpec((ts, wblk), lambda b, h, t: (b * steps + t, col0 // wblk + h))

    seg = (jnp.arange(HG_SUB * HG_DK)[:, None] // HG_DK == jnp.arange(HG_CHUNK)[None, :] % HG_SUB).astype(BF16)

    return pl.pallas_call(
        functools.partial(_hgrn2_kernel, layer=layer, chunks=ts // HG_CHUNK),
        grid=(batch, HG_HEADS // HG_HEADS_PER_STEP, steps),
        in_specs=[in_spec(COL_HG_Q), in_spec(COL_HG_F), in_spec(COL_HG_I), in_spec(COL_HG_G),
                  pl.BlockSpec((depth, wblk), lambda b, h, t: (0, h)),
                  pl.BlockSpec((1, wblk), lambda b, h, t: (0, h)),
                  pl.BlockSpec((HG_SUB * HG_DK, HG_CHUNK), lambda b, h, t: (0, 0))],
        out_specs=pl.BlockSpec((ts, wblk), lambda b, h, t: (b * steps + t, h)),
        out_shape=jax.ShapeDtypeStruct((m, BRANCH_WIDTH), BF16),
        scratch_shapes=[pltpu.VMEM((HG_HEADS_PER_STEP, HG_DK, HG_DK), F32)],
        compiler_params=_params(("parallel", "parallel", "arbitrary")),
        name="hgrn2_branch",
    )(p, p, p, p, lb_table, norm_w.reshape(1, BRANCH_WIDTH), seg)


def _ssd_kernel(z_ref, x_ref, bc_ref, dt_ref, cw_ref, cb_ref, dtb_ref, alog_ref, dsk_ref, nw_ref,
                expand_ref, o_ref, tail_ref, state_ref):
    length = SSM_CHUNK
    inner = BRANCH_WIDTH
    gw = inner // SSM_GROUPS
    n = SSM_STATE

    @pl.when(pl.program_id(1) == 0)
    def _():
        tail_ref[...] = jnp.zeros_like(tail_ref)
        state_ref[...] = jnp.zeros_like(state_ref)

    xbc = jnp.concatenate([x_ref[...], bc_ref[...]], axis=1)
    prev_rows = tail_ref[...]
    head_sub = lax.broadcasted_iota(jnp.int32, (SSM_TAIL, xbc.shape[1]), 0)
    conv = cb_ref[...] + xbc * cw_ref[SSM_CONV - 1:SSM_CONV, :]
    for shift in range(1, SSM_CONV):
        rolled = pltpu.roll(xbc, shift, axis=0)
        head = jnp.where(head_sub < shift, pltpu.roll(prev_rows, shift, axis=0), rolled[:SSM_TAIL])
        shifted = jnp.concatenate([head, rolled[SSM_TAIL:]], axis=0)
        conv = conv + shifted * cw_ref[SSM_CONV - 1 - shift:SSM_CONV - shift, :]
    tail_ref[...] = xbc[length - SSM_TAIL:length]
    xbc_act = _silu(conv)
    x_c = xbc_act[:, :inner]
    b_m = xbc_act[:, inner:inner + SSM_GROUPS * n]
    c_m = xbc_act[:, inner + SSM_GROUPS * n:]

    dt = jax.nn.softplus(dt_ref[...] + dtb_ref[...])
    a = dt * (-jnp.exp(alog_ref[...]))
    row = lax.broadcasted_iota(jnp.int32, (length, length), 0)
    col = lax.broadcasted_iota(jnp.int32, (length, length), 1)
    causal = row >= col
    cs = _dot_exact_lhs(causal, a)
    cs_t = cs.T
    expand = expand_ref[...]
    dt_e = _dot_exact_rhs(dt, expand)
    cs_e = _dot_exact_rhs(cs, expand)
    cs_last_e = cs_e[length - 1:length]
    xs = x_c * dt_e
    in_decay_e = jnp.exp(cs_e)
    state_decay_e = jnp.exp(cs_last_e - cs_e)
    chunk_decay_e = jnp.exp(cs_last_e)
    z = z_ref[...]

    heads_per_group = SSM_HEADS // SSM_GROUPS
    p = SSM_HEADDIM
    for g in range(SSM_GROUPS):
        gcols = slice(g * gw, (g + 1) * gw)
        b_g = b_m[:, g * n:(g + 1) * n]
        c_g = c_m[:, g * n:(g + 1) * n]
        cb = _dot_nt(c_g, b_g)
        prev_t = state_ref[g]
        y_off = _dot(c_g, prev_t) * in_decay_e[:, gcols]
        parts = []
        for r in range(heads_per_group):
            h = g * heads_per_group + r
            lane = DT_LANE0 + h
            decay = jnp.exp(jnp.where(causal, cs[:, lane:lane + 1] - cs_t[lane:lane + 1, :], -jnp.inf))
            parts.append(_dot(cb * decay, xs[:, h * p:(h + 1) * p]))
        y_diag = jnp.concatenate(parts, axis=1)
        state_ref[g] = prev_t * chunk_decay_e[:, gcols] + _dot_tn(b_g, xs[:, gcols] * state_decay_e[:, gcols])
        y = (y_diag + y_off) + x_c[:, gcols] * dsk_ref[:, gcols]
        y = y * _silu(z[:, gcols])
        ms = jnp.mean(y * y, axis=-1, keepdims=True)
        o_ref[:, gcols] = (y * lax.rsqrt(ms + NORM_EPS) * nw_ref[:, gcols]).astype(o_ref.dtype)


def _ssd_branch(p_main, p_dt, conv_w, conv_b, dt_bias, a_log, d_skip, norm_w, batch, seq, cast=None):
    m = p_main.shape[0]
    length = SSM_CHUNK
    steps = seq // length
    inner = BRANCH_WIDTH
    conv_dim = conv_w.shape[1]

    def pad_heads(vec):
        return jnp.pad(vec, (DT_LANE0, 0)).reshape(1, LANES)

    expand = (jnp.arange(LANES)[:, None] == (DT_LANE0 + jnp.arange(inner) // SSM_HEADDIM)[None, :]).astype(BF16)
    d_skip_e = jnp.repeat(d_skip, SSM_HEADDIM).reshape(1, inner)

    def rows(width, col0):
        return pl.BlockSpec((length, width), lambda b, t: (b * steps + t, col0 // width))

    def whole(shape):
        return pl.BlockSpec(shape, lambda b, t: (0,) * len(shape))

    return _call_with_optional_cast(
        _ssd_kernel, cast, (batch, steps), lambda b, t: b * steps + t,
        [rows(inner, COL_SSM_Z), rows(inner, COL_SSM_X), rows(inner, COL_SSM_BC),
         rows(LANES, 0),
         whole((SSM_CONV, conv_dim)), whole((1, conv_dim)),
         whole((1, LANES)), whole((1, LANES)), whole((1, inner)), whole((1, inner)),
         whole((LANES, inner))],
        pl.BlockSpec((length, inner), lambda b, t: (b * steps + t, 0)),
        jax.ShapeDtypeStruct((m, inner), BF16),
        (p_main, p_main, p_main, p_dt, conv_w, conv_b.reshape(1, conv_dim), pad_heads(dt_bias),
         pad_heads(a_log), d_skip_e, norm_w.reshape(1, inner), expand),
        scratch_shapes=[pltpu.VMEM((SSM_TAIL, conv_dim), F32),
                        pltpu.VMEM((SSM_GROUPS, SSM_STATE, inner // SSM_GROUPS), F32)],
        compiler_params=_params(("arbitrary", "arbitrary")),
        name="ssd_branch")


def _swa_kernel(sink_ref, q_ref, kc_ref, kp_ref, vc_ref, vp_ref, qw_ref, kw_ref, seg_ref, o_ref):
    t = ATT_BLOCK
    hd = ATT_HEAD_DIM
    group = ATT_Q_HEADS // ATT_KV_HEADS
    pairs = group // 2
    blk = pl.program_id(1)
    q = q_ref[...]
    keys = jnp.concatenate([kp_ref[...], kc_ref[...]], axis=0)
    vals = jnp.concatenate([vp_ref[...], vc_ref[...]], axis=0)
    lane = lax.broadcasted_iota(jnp.int32, (2 * t, LANES), 1)
    low = lane < hd

    ssq = _dot_exact_rhs(q * q, seg_ref[...])
    q_scale = lax.rsqrt(ssq * (1.0 / hd) + NORM_EPS)

    k_sq = keys * keys
    k_ssq = jnp.where(low, jnp.sum(jnp.where(low, k_sq, 0.0), axis=-1, keepdims=True),
                      jnp.sum(jnp.where(low, 0.0, k_sq), axis=-1, keepdims=True))
    k_fold = keys * lax.rsqrt(k_ssq * (1.0 / hd) + NORM_EPS) * (kw_ref[...] * qw_ref[...] * (hd ** -0.5))

    qi = lax.broadcasted_iota(jnp.int32, (t, 2 * t), 0)
    kj = lax.broadcasted_iota(jnp.int32, (t, 2 * t), 1)
    first_key = jnp.where(blk > 0, 0, t)
    bias = jnp.where((kj > qi) & (kj <= qi + t) & (kj >= first_key), 0.0, -jnp.inf)

    out_pairs = [None] * (ATT_Q_HEADS // 2)
    for c in range(ATT_KV_HEADS):
        own_low = c % 2 == 0
        k_own = jnp.where(low if own_low else ~low, k_fold, 0.0)
        v_own = jnp.where(low if own_low else ~low, vals, 0.0)
        k_other = pltpu.roll(k_own, hd, axis=1)
        v_other = pltpu.roll(v_own, hd, axis=1)
        for parity in range(2):
            k_sel = (k_own if (parity == 0) == own_low else k_other).astype(BF16)
            v_sel = (v_own if (parity == 0) == own_low else v_other).astype(BF16)
            pair0 = c * pairs
            q_stack = jnp.concatenate([q[:, (pair0 + pp) * LANES:(pair0 + pp + 1) * LANES]
                                       for pp in range(pairs)], axis=0).astype(BF16)
            s_all = lax.dot_general(q_stack, k_sel, (((1,), (1,)), ((), ())), preferred_element_type=F32)
            e_parts, inv_parts = [], []
            for pp in range(pairs):
                h = c * group + 2 * pp + parity
                s = s_all[pp * t:(pp + 1) * t] * q_scale[:, h:h + 1] + bias
                sink = sink_ref[h]
                mx = jnp.maximum(jnp.max(s, axis=-1, keepdims=True), sink)
                e = jnp.exp(s - mx)
                e_parts.append(e)
                inv_parts.append(1.0 / (jnp.sum(e, axis=-1, keepdims=True) + jnp.exp(sink - mx)))
            pv = jnp.dot(jnp.concatenate(e_parts, axis=0).astype(BF16), v_sel, preferred_element_type=F32)
            for pp in range(pairs):
                half = pv[pp * t:(pp + 1) * t] * inv_parts[pp]
                idx = pair0 + pp
                out_pairs[idx] = half if out_pairs[idx] is None else out_pairs[idx] + half
    for idx, pair in enumerate(out_pairs):
        o_ref[:, idx * LANES:(idx + 1) * LANES] = pair.astype(o_ref.dtype)


def _swa_branch(p, q_norm_w, k_norm_w, sinks, batch, seq, cast=None):
    m = p.shape[0]
    t = ATT_BLOCK
    steps = seq // t
    qw = ATT_Q_HEADS * ATT_HEAD_DIM
    kvw = ATT_KV_HEADS * ATT_HEAD_DIM

    def cur(width, col0):
        return pl.BlockSpec((t, width), lambda b, i: (b * steps + i, col0 // width))

    def prev(width, col0):
        return pl.BlockSpec((t, width), lambda b, i: (b * steps + jnp.maximum(i - 1, 0), col0 // width))

    def both_halves(w):
        return jnp.concatenate([w, w]).reshape(1, LANES)

    seg = (jnp.arange(qw)[:, None] // ATT_HEAD_DIM == jnp.arange(LANES)[None, :]).astype(BF16)

    return _call_with_optional_cast(
        _swa_kernel, cast, (batch, steps), lambda b, i: b * steps + i,
        [pl.BlockSpec(memory_space=pltpu.SMEM),
         cur(qw, COL_ATT_Q), cur(kvw, COL_ATT_K), prev(kvw, COL_ATT_K),
         cur(kvw, COL_ATT_V), prev(kvw, COL_ATT_V),
         pl.BlockSpec((1, LANES), lambda b, i: (0, 0)),
         pl.BlockSpec((1, LANES), lambda b, i: (0, 0)),
         pl.BlockSpec((qw, LANES), lambda b, i: (0, 0))],
        pl.BlockSpec((t, qw), lambda b, i: (b * steps + i, 0)),
        jax.ShapeDtypeStruct((m, qw), BF16),
        (sinks, p, p, p, p, p, both_halves(q_norm_w), both_halves(k_norm_w), seg),
        compiler_params=_params(("arbitrary", "arbitrary")),
        name="swa_branch")


def _ffn(x, layer, norm_w, w_up, w_down):
    h = _rmsnorm(x, norm_w[layer])
    act, w_down_padded = _up_swiglu(h, w_up, w_down, layer)
    return _down_residual(act, w_down_padded, x)


def _mixer(x, layer, batch, seq, mix_norm, w_in_t, gm_ln_w, gm_ln_b, gm_w_s, gm_b_s, hg_lb_table, hg_norm,
           ssm_conv_w, ssm_conv_b, ssm_dt_bias, ssm_a_log, ssm_d, ssm_norm,
           att_q_norm, att_k_norm, att_sinks, w_branch, w_out, ssd_cast, swa_cast):
    h = _rmsnorm(x, mix_norm)
    p_main = _in_projection(h, w_in_t, layer, ROW0_MAIN, D_IN_MAIN, 512, "mixer_in_projection_main")
    nb, kb, d = w_branch.shape[1:]
    p_gates, w_branch_b, w_out_b = _in_projection(
        h, w_in_t, layer, ROW0_GATES, D_IN_GATES, 512, "mixer_in_projection_gates", BF16,
        cast_weights=(w_branch.reshape(-1, nb * kb, d), w_out))
    p_att = _in_projection(h, w_in_t, layer, ROW0_ATT, D_IN_ATT, 256, "mixer_in_projection_att")
    p_dt = _in_projection(h, w_in_t, layer, ROW0_DT, LANES, LANES, "mixer_in_projection_dt")
    ssd_out, ssd_w = _ssd_branch(p_main, p_dt, ssm_conv_w, ssm_conv_b, ssm_dt_bias, ssm_a_log, ssm_d, ssm_norm,
                                 batch, seq, cast=ssd_cast)
    swa_out, swa_w = _swa_branch(p_att, att_q_norm, att_k_norm, att_sinks, batch, seq, cast=swa_cast)
    branches = (
        _gmlp_branch(p_main, gm_ln_w, gm_ln_b, gm_w_s, gm_b_s),
        _hgrn2_branch(p_main, hg_lb_table, hg_norm, layer, batch, seq),
        ssd_out,
        swa_out,
    )
    merged = _merge(branches, w_branch_b.reshape(nb, kb, d), p_gates)
    return _out_residual(merged, w_out_b, x), ssd_w, swa_w


def kernel(x, ffn1_norm, ffn1_up, ffn1_down, mix_norm, w_in, gm_ln_w, gm_ln_b, gm_w_s, gm_b_s, hg_lb_table, hg_norm, ssm_conv_w, ssm_conv_b, ssm_dt_bias, ssm_a_log, ssm_d, ssm_norm, att_q_norm, att_k_norm, att_sinks, w_branch, w_out, ffn2_norm, ffn2_up, ffn2_down):
    batch, seq, d = x.shape
    depth = ffn1_norm.shape[0]
    xf = x.reshape(batch * seq, d)
    w_in_t = jnp.transpose(w_in, (0, 2, 1))
    ffn1_up_l = ffn1_up
    for l in range(depth):
        xf = _ffn(xf, l, ffn1_norm, ffn1_up_l, ffn1_down)
        xf, ffn2_up_l, ffn1_up_next = _mixer(
            xf, l, batch, seq, mix_norm[l], w_in_t, gm_ln_w[l], gm_ln_b[l], gm_w_s[l], gm_b_s[l],
            hg_lb_table, hg_norm[l], ssm_conv_w[l], ssm_conv_b[l], ssm_dt_bias[l], ssm_a_log[l],
            ssm_d[l], ssm_norm[l], att_q_norm[l], att_k_norm[l], att_sinks[l], w_branch, w_out,
            ssd_cast=(ffn2_up, l), swa_cast=(ffn1_up, l + 1) if l + 1 < depth else None)
        xf = _ffn(xf, l, ffn2_norm, ffn2_up_l, ffn2_down)
        ffn1_up_l = ffn1_up_next
    return xf.reshape(batch, seq, d)
```

```python
import functools

import jax
import jax.numpy as jnp
from jax import lax
from jax.experimental import pallas as pl
from jax.experimental.pallas import tpu as pltpu

F32 = jnp.float32
BF16 = jnp.bfloat16

NORM_EPS = 1e-6
LN_EPS = 1e-5

V7X_VMEM_BYTES = 64 * 1024 * 1024
VMEM_LIMIT_BYTES = V7X_VMEM_BYTES - 8 * 1024 * 1024
LANES = 128
SUBLANES = 8

BRANCH_WIDTH = 1024
N_BRANCH = 4
GM_GROUPS = 8
GM_CHUNK = 128
HG_HEADS = 8
HG_DK = 128
HG_CHUNK = 64
HG_SUB = 8
HG_HEADS_PER_STEP = 8
SSM_HEADS = 16
SSM_HEADDIM = 64
SSM_GROUPS = 4
SSM_STATE = 128
SSM_CONV = 4
SSM_CHUNK = 128
SSM_TAIL = 8
ATT_HEAD_DIM = 64
ATT_Q_HEADS = 16
ATT_KV_HEADS = 2
ATT_BLOCK = 128

ROW0_MAIN = 0
D_IN_MAIN = 9216
COL_GM = 0
COL_HG_Q = 2048
COL_HG_F = 3072
COL_HG_I = 4096
COL_HG_G = 5120
COL_SSM_Z = 6144
COL_SSM_X = 7168
COL_SSM_BC = 8192
DT_LANE0 = LANES - SSM_HEADS
ROW0_DT = D_IN_MAIN - DT_LANE0
ROW0_ATT = D_IN_MAIN + SSM_HEADS
D_IN_ATT = (ATT_Q_HEADS + 2 * ATT_KV_HEADS) * ATT_HEAD_DIM
COL_ATT_Q = 0
COL_ATT_K = ATT_Q_HEADS * ATT_HEAD_DIM
COL_ATT_V = COL_ATT_K + ATT_KV_HEADS * ATT_HEAD_DIM
ROW0_GATES = ROW0_ATT + D_IN_ATT
D_IN_GATES = N_BRANCH * 4096

D_FF = 11008
D_FF_PADDED = 11264


def _params(semantics):
    return pltpu.CompilerParams(dimension_semantics=semantics, vmem_limit_bytes=VMEM_LIMIT_BYTES)


def _silu(x):
    return x * jax.nn.sigmoid(x)


def _dot(a, b):
    return jnp.dot(a.astype(BF16), b.astype(BF16), preferred_element_type=F32)


def _dot_nt(a, b):
    return lax.dot_general(a.astype(BF16), b.astype(BF16), (((1,), (1,)), ((), ())), preferred_element_type=F32)


def _dot_tn(a, b):
    return lax.dot_general(a.astype(BF16), b.astype(BF16), (((0,), (0,)), ((), ())), preferred_element_type=F32)


def _split3(x):
    hi = x.astype(BF16)
    r1 = x - hi.astype(F32)
    mid = r1.astype(BF16)
    lo = (r1 - mid.astype(F32)).astype(BF16)
    return hi, mid, lo


def _dot_exact_lhs(mask, x):
    m = mask.astype(BF16)
    hi, mid, lo = _split3(x)
    return (jnp.dot(m, hi, preferred_element_type=F32) + jnp.dot(m, mid, preferred_element_type=F32)
            + jnp.dot(m, lo, preferred_element_type=F32))


def _dot_exact_rhs(x, mask):
    m = mask.astype(BF16)
    hi, mid, lo = _split3(x)
    return (jnp.dot(hi, m, preferred_element_type=F32) + jnp.dot(mid, m, preferred_element_type=F32)
            + jnp.dot(lo, m, preferred_element_type=F32))


def _gelu_exact(x):
    return 0.5 * x * (1.0 + lax.erf(x * (2.0 ** -0.5)))


def _rmsnorm_kernel(x_ref, w_ref, o_ref):
    x = x_ref[...]
    ms = jnp.mean(x * x, axis=-1, keepdims=True)
    o_ref[...] = (x * lax.rsqrt(ms + NORM_EPS) * w_ref[...]).astype(o_ref.dtype)


def _rmsnorm(x, w):
    m, d = x.shape
    bm = 256
    return pl.pallas_call(
        _rmsnorm_kernel,
        grid=(m // bm,),
        in_specs=[pl.BlockSpec((bm, d), lambda i: (i, 0)),
                  pl.BlockSpec((1, d), lambda i: (0, 0))],
        out_specs=pl.BlockSpec((bm, d), lambda i: (i, 0)),
        out_shape=jax.ShapeDtypeStruct((m, d), BF16),
        compiler_params=_params(("parallel",)),
        name="rmsnorm",
    )(x, w.reshape(1, d))


def _up_swiglu_kernel(x_ref, wa_ref, wb_ref, wd_ref, o_ref, wdo_ref, *, real_blocks, wd_real_blocks, wd_blocks):
    i = pl.program_id(0)
    j = pl.program_id(1)
    nj = pl.num_programs(1)

    @pl.when(j < real_blocks)
    def _():
        x = x_ref[...]
        a = jnp.dot(x, wa_ref[...].astype(BF16), preferred_element_type=F32)
        b = jnp.dot(x, wb_ref[...].astype(BF16), preferred_element_type=F32)
        o_ref[...] = (_silu(a) * b).astype(o_ref.dtype)

    @pl.when(j >= real_blocks)
    def _():
        o_ref[...] = jnp.zeros_like(o_ref)

    t = i * nj + j

    @pl.when(t < wd_real_blocks)
    def _():
        wdo_ref[...] = wd_ref[...].astype(wdo_ref.dtype)

    @pl.when((t >= wd_real_blocks) & (t < wd_blocks))
    def _():
        wdo_ref[...] = jnp.zeros_like(wdo_ref)


def _up_swiglu(h, w_up, w_down, layer):
    m, k = h.shape
    n_out = w_down.shape[2]
    bm, bn = 2048, 256
    real_blocks = D_FF // bn
    last = real_blocks - 1
    nj = D_FF_PADDED // bn
    wd_rows = LANES
    wd_real_blocks = D_FF // wd_rows
    wd_blocks = D_FF_PADDED // wd_rows
    assert wd_blocks <= (m // bm) * nj

    def wd_step(i, j):
        return jnp.minimum(i * nj + j, wd_blocks - 1)

    return pl.pallas_call(
        functools.partial(_up_swiglu_kernel, real_blocks=real_blocks, wd_real_blocks=wd_real_blocks,
                          wd_blocks=wd_blocks),
        grid=(m // bm, nj),
        in_specs=[pl.BlockSpec((bm, k), lambda i, j: (i, 0), pipeline_mode=pl.Buffered(1)),
                  pl.BlockSpec((None, k, bn), lambda i, j: (layer, 0, jnp.minimum(j, last))),
                  pl.BlockSpec((None, k, bn), lambda i, j: (layer, 0, real_blocks + jnp.minimum(j, last))),
                  pl.BlockSpec((None, wd_rows, n_out),
                               lambda i, j: (layer, jnp.minimum(wd_step(i, j), wd_real_blocks - 1), 0))],
        out_specs=[pl.BlockSpec((bm, bn), lambda i, j: (i, j)),
                   pl.BlockSpec((wd_rows, n_out), lambda i, j: (wd_step(i, j), 0))],
        out_shape=[jax.ShapeDtypeStruct((m, D_FF_PADDED), BF16),
                   jax.ShapeDtypeStruct((D_FF_PADDED, n_out), BF16)],
        compiler_params=_params(("arbitrary", "arbitrary")),
        name="ffn_up_swiglu",
    )(h, w_up, w_up, w_down)


def _down_residual_kernel(a_ref, w_ref, res_ref, o_ref, acc_ref, *, nk):
    k = pl.program_id(2)

    @pl.when(k == 0)
    def _():
        acc_ref[...] = jnp.zeros_like(acc_ref)

    acc_ref[...] += jnp.dot(a_ref[...], w_ref[...], preferred_element_type=F32)

    @pl.when(k == nk - 1)
    def _():
        o_ref[...] = res_ref[...] + 0.5 * acc_ref[...]


def _down_residual(act, w, res):
    m, k = act.shape
    n = w.shape[1]
    bm, bn, nk = 1024, 1024, 4
    bk = k // nk
    return pl.pallas_call(
        functools.partial(_down_residual_kernel, nk=nk),
        grid=(m // bm, n // bn, nk),
        in_specs=[pl.BlockSpec((bm, bk), lambda i, j, kk: (i, kk)),
                  pl.BlockSpec((bk, bn), lambda i, j, kk: (kk, j)),
                  pl.BlockSpec((bm, bn), lambda i, j, kk: (i, j))],
        out_specs=pl.BlockSpec((bm, bn), lambda i, j, kk: (i, j)),
        out_shape=jax.ShapeDtypeStruct((m, n), F32),
        scratch_shapes=[pltpu.VMEM((bm, bn), F32)],
        compiler_params=_params(("parallel", "parallel", "arbitrary")),
        name="ffn_down_residual",
    )(act, w, res)


def _nt_matmul_kernel(x_ref, wt_ref, *refs, n_casts):
    cast_in, o_ref, cast_out = refs[:n_casts], refs[n_casts], refs[n_casts + 1:]
    w = wt_ref[...].astype(BF16)
    o_ref[...] = lax.dot_general(x_ref[...], w, (((1,), (1,)), ((), ())),
                                 preferred_element_type=F32).astype(o_ref.dtype)
    for src, dst in zip(cast_in, cast_out):
        dst[...] = src[...].astype(dst.dtype)


def _in_projection(h, w_in_t, layer, row0, n, bn, name, out_dtype=F32, cast_weights=()):
    m, k = h.shape
    bm = 2048
    nj = n // bn
    steps = (m // bm) * nj
    cast_specs, cast_out_specs, cast_shapes = [], [], []
    for w in cast_weights:
        _, r, c = w.shape
        rows = r // steps
        assert rows * steps == r and rows % (2 * SUBLANES) == 0
        cast_specs.append(pl.BlockSpec((None, rows, c), lambda i, j: (layer, i * nj + j, 0)))
        cast_out_specs.append(pl.BlockSpec((rows, c), lambda i, j: (i * nj + j, 0)))
        cast_shapes.append(jax.ShapeDtypeStruct((r, c), BF16))
    out = pl.pallas_call(
        functools.partial(_nt_matmul_kernel, n_casts=len(cast_weights)),
        grid=(m // bm, nj),
        in_specs=[pl.BlockSpec((bm, k), lambda i, j: (i, 0), pipeline_mode=pl.Buffered(1)),
                  pl.BlockSpec((None, pl.Element(bn), pl.Element(k)),
                               lambda i, j: (layer, pl.multiple_of(row0 + j * bn, SUBLANES), 0))] + cast_specs,
        out_specs=[pl.BlockSpec((bm, bn), lambda i, j: (i, j))] + cast_out_specs,
        out_shape=[jax.ShapeDtypeStruct((m, n), out_dtype)] + cast_shapes,
        compiler_params=_params(("arbitrary", "arbitrary")),
        name=name,
    )(h, w_in_t, *cast_weights)
    return out if cast_weights else out[0]


def _out_residual_kernel(x_ref, w_ref, res_ref, o_ref):
    o_ref[...] = res_ref[...] + jnp.dot(x_ref[...], w_ref[...], preferred_element_type=F32)


def _out_residual(merged, w, res):
    m, k = merged.shape
    n = w.shape[1]
    bm, bn = 1024, 1024
    return pl.pallas_call(
        _out_residual_kernel,
        grid=(m // bm, n // bn),
        in_specs=[pl.BlockSpec((bm, k), lambda i, j: (i, 0)),
                  pl.BlockSpec((k, bn), lambda i, j: (0, j)),
                  pl.BlockSpec((bm, bn), lambda i, j: (i, j))],
        out_specs=pl.BlockSpec((bm, bn), lambda i, j: (i, j)),
        out_shape=jax.ShapeDtypeStruct((m, n), F32),
        compiler_params=_params(("parallel", "arbitrary")),
        name="mixer_out_residual",
    )(merged, w, res)


def _merge_kernel(b0_ref, b1_ref, b2_ref, b3_ref, w_ref, g0_ref, g1_ref, g2_ref, g3_ref, o_ref):
    branches = (b0_ref, b1_ref, b2_ref, b3_ref)
    gates = (g0_ref, g1_ref, g2_ref, g3_ref)
    merged = None
    for b in range(N_BRANCH):
        y = jnp.dot(branches[b][...], w_ref[b], preferred_element_type=F32)
        term = jax.nn.sigmoid(gates[b][...].astype(F32)) * y
        merged = term if merged is None else merged + term
    o_ref[...] = merged.astype(o_ref.dtype)


def _merge(branches, w_branch, p_gates):
    m = p_gates.shape[0]
    nb, kb, n = w_branch.shape
    bm, bn = 512, 1024
    gate_blocks = n // bn

    def gate_spec(b):
        return pl.BlockSpec((bm, bn), lambda j, i: (i, b * gate_blocks + j))

    return pl.pallas_call(
        _merge_kernel,
        grid=(n // bn, m // bm),
        in_specs=[pl.BlockSpec((bm, kb), lambda j, i: (i, 0)) for _ in range(nb)]
        + [pl.BlockSpec((nb, kb, bn), lambda j, i: (0, 0, j))]
        + [gate_spec(b) for b in range(nb)],
        out_specs=pl.BlockSpec((bm, bn), lambda j, i: (i, j)),
        out_shape=jax.ShapeDtypeStruct((m, n), BF16),
        compiler_params=_params(("parallel", "arbitrary")),
        name="branch_merge",
    )(*branches, w_branch, p_gates, p_gates, p_gates, p_gates)


def _gmlp_kernel(p_ref, lnw_ref, lnb_ref, ws_ref, bst_ref, o_ref, *, chunks):
    t = GM_CHUNK
    width = BRANCH_WIDTH
    gw = width // GM_GROUPS
    row = lax.broadcasted_iota(jnp.int32, (t, t), 0)
    col = lax.broadcasted_iota(jnp.int32, (t, t), 1)
    causal = row >= col
    mixers = [jnp.where(causal, ws_ref[g], 0.0).astype(BF16) for g in range(GM_GROUPS)]
    for c in range(chunks):
        rows = slice(c * t, (c + 1) * t)
        act = _gelu_exact(p_ref[rows, :])
        u = act[:, :width]
        v = act[:, width:]
        mu = jnp.mean(v, axis=-1, keepdims=True)
        var = jnp.mean(jnp.square(v - mu), axis=-1, keepdims=True)
        vn = (v - mu) * lax.rsqrt(var + LN_EPS) * lnw_ref[...] + lnb_ref[...]
        for g in range(GM_GROUPS):
            cols = slice(g * gw, (g + 1) * gw)
            mixed = _dot(mixers[g], vn[:, cols]) + bst_ref[:, g:g + 1]
            o_ref[rows, cols] = (u[:, cols] * mixed).astype(o_ref.dtype)


def _gmlp_branch(p, ln_w, ln_b, w_s, b_s):
    m = p.shape[0]
    tb = 512
    width = BRANCH_WIDTH
    return pl.pallas_call(
        functools.partial(_gmlp_kernel, chunks=tb // GM_CHUNK),
        grid=(m // tb,),
        in_specs=[pl.BlockSpec((tb, 2 * width), lambda i: (i, COL_GM // (2 * width))),
                  pl.BlockSpec((1, width), lambda i: (0, 0)),
                  pl.BlockSpec((1, width), lambda i: (0, 0)),
                  pl.BlockSpec((GM_GROUPS, GM_CHUNK, GM_CHUNK), lambda i: (0, 0, 0)),
                  pl.BlockSpec((GM_CHUNK, GM_GROUPS), lambda i: (0, 0))],
        out_specs=pl.BlockSpec((tb, width), lambda i: (i, 0)),
        out_shape=jax.ShapeDtypeStruct((m, width), BF16),
        compiler_params=_params(("parallel",)),
        name="gmlp_branch",
    )(p, ln_w.reshape(1, width), ln_b.reshape(1, width), w_s, b_s.T)


def _rows_from(x, index_of_block, block_rows):
    blocks = x.shape[0] // block_rows
    parts = [jnp.broadcast_to(x[index_of_block(b):index_of_block(b) + 1], (block_rows, x.shape[1]))
             for b in range(blocks)]
    return jnp.concatenate(parts, axis=0)


def _hgrn2_intra_scores(q, k, g, seg_ref):
    c_len, dk = q.shape
    row = lax.broadcasted_iota(jnp.int32, (c_len, c_len), 0)
    col = lax.broadcasted_iota(jnp.int32, (c_len, c_len), 1)
    pos = lax.broadcasted_iota(jnp.int32, (c_len, dk), 0)
    sub = pos & (HG_SUB - 1)

    pieces = []
    for s in range(HG_SUB):
        g_s = _rows_from(g, lambda b: b * HG_SUB + s, HG_SUB)
        k_s = _rows_from(k, lambda b: b * HG_SUB + s, HG_SUB)
        decay = jnp.exp(jnp.where(sub >= s, g - g_s, -jnp.inf))
        pieces.append((q * k_s * decay).astype(BF16))
    diag = jnp.dot(jnp.concatenate(pieces, axis=1), seg_ref[...], preferred_element_type=F32)
    scores = jnp.where((row ^ col) < HG_SUB, diag, 0.0)

    width = HG_SUB
    while width < c_len:
        right = (pos & width) != 0
        g_ref = _rows_from(g, lambda b: b * 2 * width + width - 1, 2 * width)
        q_s = q * jnp.exp(jnp.where(right, g - g_ref, -jnp.inf))
        k_s = k * jnp.exp(jnp.where(right, -jnp.inf, g_ref - g))
        cross = _dot_nt(q_s, k_s)
        scores = scores + jnp.where((row ^ col) < 2 * width, cross, 0.0)
        width *= 2
    return scores


def _hgrn2_kernel(q_ref, f_ref, i_ref, g_ref, lbt_ref, nw_ref, seg_ref, o_ref, state_ref, *, layer, chunks):
    c_len = HG_CHUNK

    @pl.when(pl.program_id(2) == 0)
    def _():
        state_ref[...] = jnp.zeros_like(state_ref)

    table = lbt_ref[...]
    e = jnp.exp(table - jnp.max(table, axis=0, keepdims=True))
    probs = e / jnp.sum(e, axis=0, keepdims=True)
    lb_all = jnp.sum(probs[:layer + 1], axis=0, keepdims=True) - probs[0:1]

    row = lax.broadcasted_iota(jnp.int32, (c_len, c_len), 0)
    col = lax.broadcasted_iota(jnp.int32, (c_len, c_len), 1)
    cumsum_mat = row >= col

    def chunk_body(c, carry):
        r0 = pl.multiple_of(c * c_len, c_len)
        rows = pl.ds(r0, c_len)
        for h in range(HG_HEADS_PER_STEP):
            cols = slice(h * HG_DK, (h + 1) * HG_DK)
            lb = lb_all[:, cols]
            q = _silu(q_ref[rows, cols])
            f = lb + (1.0 - lb) * jax.nn.sigmoid(f_ref[rows, cols])
            log_f = jnp.log(f)
            k = 1.0 - f
            v = i_ref[rows, cols]
            g = _dot_exact_lhs(cumsum_mat, log_f)
            state_t = state_ref[h]
            o = _dot_nt(q * jnp.exp(g), state_t)
            o = o + _dot(_hgrn2_intra_scores(q, k, g, seg_ref), v)
            g_last = g[c_len - 1:c_len]
            k_s = k * jnp.exp(g_last - g)
            state_ref[h] = state_t * jnp.exp(g_last) + _dot_tn(v, k_s)
            ms = jnp.mean(o * o, axis=-1, keepdims=True)
            y = o * lax.rsqrt(ms + NORM_EPS) * nw_ref[:, cols]
            o_ref[rows, cols] = (y * _silu(g_ref[rows, cols])).astype(o_ref.dtype)
        return carry

    lax.fori_loop(0, chunks, chunk_body, 0, unroll=2)


def _hgrn2_branch(p, lb_table, norm_w, layer, batch, seq):
    m = p.shape[0]
    ts = 512
    steps = seq // ts
    depth = lb_table.shape[0]
    wblk = HG_HEADS_PER_STEP * HG_DK

    def in_spec(col0):
        return pl.BlockSpec((ts, wblk), lambda b, h, t: (b * steps + t, col0 // wblk + h))

    seg = (jnp.arange(HG_SUB * HG_DK)[:, None] // HG_DK == jnp.arange(HG_CHUNK)[None, :] % HG_SUB).astype(BF16)

    return pl.pallas_call(
        functools.partial(_hgrn2_kernel, layer=layer, chunks=ts // HG_CHUNK),
        grid=(batch, HG_HEADS // HG_HEADS_PER_STEP, steps),
        in_specs=[in_spec(COL_HG_Q), in_spec(COL_HG_F), in_spec(COL_HG_I), in_spec(COL_HG_G),
                  pl.BlockSpec((depth, wblk), lambda b, h, t: (0, h)),
                  pl.BlockSpec((1, wblk), lambda b, h, t: (0, h)),
                  pl.BlockSpec((HG_SUB * HG_DK, HG_CHUNK), lambda b, h, t: (0, 0))],
        out_specs=pl.BlockSpec((ts, wblk), lambda b, h, t: (b * steps + t, h)),
        out_shape=jax.ShapeDtypeStruct((m, BRANCH_WIDTH), BF16),
        scratch_shapes=[pltpu.VMEM((HG_HEADS_PER_STEP, HG_DK, HG_DK), F32)],
        compiler_params=_params(("parallel", "parallel", "arbitrary")),
        name="hgrn2_branch",
    )(p, p, p, p, lb_table, norm_w.reshape(1, BRANCH_WIDTH), seg)


def _ssd_kernel(z_ref, x_ref, bc_ref, dt_ref, cw_ref, cb_ref, dtb_ref, alog_ref, dsk_ref, nw_ref,
                expand_ref, o_ref, tail_ref, state_ref):
    length = SSM_CHUNK
    inner = BRANCH_WIDTH
    gw = inner // SSM_GROUPS
    n = SSM_STATE

    @pl.when(pl.program_id(1) == 0)
    def _():
        tail_ref[...] = jnp.zeros_like(tail_ref)
        state_ref[...] = jnp.zeros_like(state_ref)

    xbc = jnp.concatenate([x_ref[...], bc_ref[...]], axis=1)
    prev_rows = tail_ref[...]
    head_sub = lax.broadcasted_iota(jnp.int32, (SSM_TAIL, xbc.shape[1]), 0)
    conv = cb_ref[...] + xbc * cw_ref[SSM_CONV - 1:SSM_CONV, :]
    for shift in range(1, SSM_CONV):
        rolled = pltpu.roll(xbc, shift, axis=0)
        head = jnp.where(head_sub < shift, pltpu.roll(prev_rows, shift, axis=0), rolled[:SSM_TAIL])
        shifted = jnp.concatenate([head, rolled[SSM_TAIL:]], axis=0)
        conv = conv + shifted * cw_ref[SSM_CONV - 1 - shift:SSM_CONV - shift, :]
    tail_ref[...] = xbc[length - SSM_TAIL:length]
    xbc_act = _silu(conv)
    x_c = xbc_act[:, :inner]
    b_m = xbc_act[:, inner:inner + SSM_GROUPS * n]
    c_m = xbc_act[:, inner + SSM_GROUPS * n:]

    dt = jax.nn.softplus(dt_ref[...] + dtb_ref[...])
    a = dt * (-jnp.exp(alog_ref[...]))
    row = lax.broadcasted_iota(jnp.int32, (length, length), 0)
    col = lax.broadcasted_iota(jnp.int32, (length, length), 1)
    causal = row >= col
    cs = _dot_exact_lhs(causal, a)
    cs_t = cs.T
    expand = expand_ref[...]
    dt_e = _dot_exact_rhs(dt, expand)
    cs_e = _dot_exact_rhs(cs, expand)
    cs_last_e = cs_e[length - 1:length]
    xs = x_c * dt_e
    in_decay_e = jnp.exp(cs_e)
    state_decay_e = jnp.exp(cs_last_e - cs_e)
    chunk_decay_e = jnp.exp(cs_last_e)
    z = z_ref[...]

    heads_per_group = SSM_HEADS // SSM_GROUPS
    p = SSM_HEADDIM
    for g in range(SSM_GROUPS):
        gcols = slice(g * gw, (g + 1) * gw)
        b_g = b_m[:, g * n:(g + 1) * n]
        c_g = c_m[:, g * n:(g + 1) * n]
        cb = _dot_nt(c_g, b_g)
        prev_t = state_ref[g]
        y_off = _dot(c_g, prev_t) * in_decay_e[:, gcols]
        parts = []
        for r in range(heads_per_group):
            h = g * heads_per_group + r
            lane = DT_LANE0 + h
            decay = jnp.exp(jnp.where(causal, cs[:, lane:lane + 1] - cs_t[lane:lane + 1, :], -jnp.inf))
            parts.append(_dot(cb * decay, xs[:, h * p:(h + 1) * p]))
        y_diag = jnp.concatenate(parts, axis=1)
        state_ref[g] = prev_t * chunk_decay_e[:, gcols] + _dot_tn(b_g, xs[:, gcols] * state_decay_e[:, gcols])
        y = (y_diag + y_off) + x_c[:, gcols] * dsk_ref[:, gcols]
        y = y * _silu(z[:, gcols])
        ms = jnp.mean(y * y, axis=-1, keepdims=True)
        o_ref[:, gcols] = (y * lax.rsqrt(ms + NORM_EPS) * nw_ref[:, gcols]).astype(o_ref.dtype)


def _ssd_branch(p_main, p_dt, conv_w, conv_b, dt_bias, a_log, d_skip, norm_w, batch, seq):
    m = p_main.shape[0]
    length = SSM_CHUNK
    steps = seq // length
    inner = BRANCH_WIDTH
    conv_dim = conv_w.shape[1]

    def pad_heads(vec):
        return jnp.pad(vec, (DT_LANE0, 0)).reshape(1, LANES)

    expand = (jnp.arange(LANES)[:, None] == (DT_LANE0 + jnp.arange(inner) // SSM_HEADDIM)[None, :]).astype(BF16)
    d_skip_e = jnp.repeat(d_skip, SSM_HEADDIM).reshape(1, inner)

    def rows(width, col0):
        return pl.BlockSpec((length, width), lambda b, t: (b * steps + t, col0 // width))

    def whole(shape):
        return pl.BlockSpec(shape, lambda b, t: (0,) * len(shape))

    return pl.pallas_call(
        _ssd_kernel,
        grid=(batch, steps),
        in_specs=[rows(inner, COL_SSM_Z), rows(inner, COL_SSM_X), rows(inner, COL_SSM_BC),
                  rows(LANES, 0),
                  whole((SSM_CONV, conv_dim)), whole((1, conv_dim)),
                  whole((1, LANES)), whole((1, LANES)), whole((1, inner)), whole((1, inner)),
                  whole((LANES, inner))],
        out_specs=pl.BlockSpec((length, inner), lambda b, t: (b * steps + t, 0)),
        out_shape=jax.ShapeDtypeStruct((m, inner), BF16),
        scratch_shapes=[pltpu.VMEM((SSM_TAIL, conv_dim), F32),
                        pltpu.VMEM((SSM_GROUPS, SSM_STATE, inner // SSM_GROUPS), F32)],
        compiler_params=_params(("parallel", "arbitrary")),
        name="ssd_branch",
    )(p_main, p_main, p_main, p_dt, conv_w, conv_b.reshape(1, conv_dim), pad_heads(dt_bias), pad_heads(a_log),
      d_skip_e, norm_w.reshape(1, inner), expand)


def _swa_kernel(sink_ref, q_ref, kc_ref, kp_ref, vc_ref, vp_ref, qw_ref, kw_ref, seg_ref, o_ref):
    t = ATT_BLOCK
    hd = ATT_HEAD_DIM
    group = ATT_Q_HEADS // ATT_KV_HEADS
    pairs = group // 2
    blk = pl.program_id(1)
    q = q_ref[...]
    keys = jnp.concatenate([kp_ref[...], kc_ref[...]], axis=0)
    vals = jnp.concatenate([vp_ref[...], vc_ref[...]], axis=0)
    lane = lax.broadcasted_iota(jnp.int32, (2 * t, LANES), 1)
    low = lane < hd

    ssq = _dot_exact_rhs(q * q, seg_ref[...])
    q_scale = lax.rsqrt(ssq * (1.0 / hd) + NORM_EPS)

    k_sq = keys * keys
    k_ssq = jnp.where(low, jnp.sum(jnp.where(low, k_sq, 0.0), axis=-1, keepdims=True),
                      jnp.sum(jnp.where(low, 0.0, k_sq), axis=-1, keepdims=True))
    k_fold = keys * lax.rsqrt(k_ssq * (1.0 / hd) + NORM_EPS) * (kw_ref[...] * qw_ref[...] * (hd ** -0.5))

    qi = lax.broadcasted_iota(jnp.int32, (t, 2 * t), 0)
    kj = lax.broadcasted_iota(jnp.int32, (t, 2 * t), 1)
    first_key = jnp.where(blk > 0, 0, t)
    bias = jnp.where((kj > qi) & (kj <= qi + t) & (kj >= first_key), 0.0, -jnp.inf)

    out_pairs = [None] * (ATT_Q_HEADS // 2)
    for c in range(ATT_KV_HEADS):
        own_low = c % 2 == 0
        k_own = jnp.where(low if own_low else ~low, k_fold, 0.0)
        v_own = jnp.where(low if own_low else ~low, vals, 0.0)
        k_other = pltpu.roll(k_own, hd, axis=1)
        v_other = pltpu.roll(v_own, hd, axis=1)
        for parity in range(2):
            k_sel = (k_own if (parity == 0) == own_low else k_other).astype(BF16)
            v_sel = (v_own if (parity == 0) == own_low else v_other).astype(BF16)
            pair0 = c * pairs
            q_stack = jnp.concatenate([q[:, (pair0 + pp) * LANES:(pair0 + pp + 1) * LANES]
                                       for pp in range(pairs)], axis=0).astype(BF16)
            s_all = lax.dot_general(q_stack, k_sel, (((1,), (1,)), ((), ())), preferred_element_type=F32)
            e_parts, inv_parts = [], []
            for pp in range(pairs):
                h = c * group + 2 * pp + parity
                s = s_all[pp * t:(pp + 1) * t] * q_scale[:, h:h + 1] + bias
                sink = sink_ref[h]
                mx = jnp.maximum(jnp.max(s, axis=-1, keepdims=True), sink)
                e = jnp.exp(s - mx)
                e_parts.append(e)
                inv_parts.append(1.0 / (jnp.sum(e, axis=-1, keepdims=True) + jnp.exp(sink - mx)))
            pv = jnp.dot(jnp.concatenate(e_parts, axis=0).astype(BF16), v_sel, preferred_element_type=F32)
            for pp in range(pairs):
                half = pv[pp * t:(pp + 1) * t] * inv_parts[pp]
                idx = pair0 + pp
                out_pairs[idx] = half if out_pairs[idx] is None else out_pairs[idx] + half
    for idx, pair in enumerate(out_pairs):
        o_ref[:, idx * LANES:(idx + 1) * LANES] = pair.astype(o_ref.dtype)


def _swa_branch(p, q_norm_w, k_norm_w, sinks, batch, seq):
    m = p.shape[0]
    t = ATT_BLOCK
    steps = seq // t
    qw = ATT_Q_HEADS * ATT_HEAD_DIM
    kvw = ATT_KV_HEADS * ATT_HEAD_DIM

    def cur(width, col0):
        return pl.BlockSpec((t, width), lambda b, i: (b * steps + i, col0 // width))

    def prev(width, col0):
        return pl.BlockSpec((t, width), lambda b, i: (b * steps + jnp.maximum(i - 1, 0), col0 // width))

    def both_halves(w):
        return jnp.concatenate([w, w]).reshape(1, LANES)

    seg = (jnp.arange(qw)[:, None] // ATT_HEAD_DIM == jnp.arange(LANES)[None, :]).astype(BF16)

    return pl.pallas_call(
        _swa_kernel,
        grid=(batch, steps),
        in_specs=[pl.BlockSpec(memory_space=pltpu.SMEM),
                  cur(qw, COL_ATT_Q), cur(kvw, COL_ATT_K), prev(kvw, COL_ATT_K),
                  cur(kvw, COL_ATT_V), prev(kvw, COL_ATT_V),
                  pl.BlockSpec((1, LANES), lambda b, i: (0, 0)),
                  pl.BlockSpec((1, LANES), lambda b, i: (0, 0)),
                  pl.BlockSpec((qw, LANES), lambda b, i: (0, 0))],
        out_specs=pl.BlockSpec((t, qw), lambda b, i: (b * steps + i, 0)),
        out_shape=jax.ShapeDtypeStruct((m, qw), BF16),
        compiler_params=_params(("parallel", "parallel")),
        name="swa_branch",
    )(sinks, p, p, p, p, p, both_halves(q_norm_w), both_halves(k_norm_w), seg)


def _ffn(x, layer, norm_w, w_up, w_down):
    h = _rmsnorm(x, norm_w[layer])
    act, w_down_padded = _up_swiglu(h, w_up, w_down, layer)
    return _down_residual(act, w_down_padded, x)


def _mixer(x, layer, batch, seq, mix_norm, w_in_t, gm_ln_w, gm_ln_b, gm_w_s, gm_b_s, hg_lb_table, hg_norm,
           ssm_conv_w, ssm_conv_b, ssm_dt_bias, ssm_a_log, ssm_d, ssm_norm,
           att_q_norm, att_k_norm, att_sinks, w_branch, w_out):
    h = _rmsnorm(x, mix_norm)
    p_main = _in_projection(h, w_in_t, layer, ROW0_MAIN, D_IN_MAIN, 512, "mixer_in_projection_main")
    nb, kb, d = w_branch.shape[1:]
    p_gates, w_branch_b, w_out_b = _in_projection(
        h, w_in_t, layer, ROW0_GATES, D_IN_GATES, 512, "mixer_in_projection_gates", BF16,
        cast_weights=(w_branch.reshape(-1, nb * kb, d), w_out))
    p_att = _in_projection(h, w_in_t, layer, ROW0_ATT, D_IN_ATT, 256, "mixer_in_projection_att")
    p_dt = _in_projection(h, w_in_t, layer, ROW0_DT, LANES, LANES, "mixer_in_projection_dt")
    branches = (
        _gmlp_branch(p_main, gm_ln_w, gm_ln_b, gm_w_s, gm_b_s),
        _hgrn2_branch(p_main, hg_lb_table, hg_norm, layer, batch, seq),
        _ssd_branch(p_main, p_dt, ssm_conv_w, ssm_conv_b, ssm_dt_bias, ssm_a_log, ssm_d, ssm_norm, batch, seq),
        _swa_branch(p_att, att_q_norm, att_k_norm, att_sinks, batch, seq),
    )
    merged = _merge(branches, w_branch_b.reshape(nb, kb, d), p_gates)
    return _out_residual(merged, w_out_b, x)


def kernel(x, ffn1_norm, ffn1_up, ffn1_down, mix_norm, w_in, gm_ln_w, gm_ln_b, gm_w_s, gm_b_s, hg_lb_table, hg_norm, ssm_conv_w, ssm_conv_b, ssm_dt_bias, ssm_a_log, ssm_d, ssm_norm, att_q_norm, att_k_norm, att_sinks, w_branch, w_out, ffn2_norm, ffn2_up, ffn2_down):
    batch, seq, d = x.shape
    depth = ffn1_norm.shape[0]
    xf = x.reshape(batch * seq, d)
    w_in_t = jnp.transpose(w_in, (0, 2, 1))
    for l in range(depth):
        xf = _ffn(xf, l, ffn1_norm, ffn1_up, ffn1_down)
        xf = _mixer(xf, l, batch, seq, mix_norm[l], w_in_t, gm_ln_w[l], gm_ln_b[l], gm_w_s[l], gm_b_s[l],
                    hg_lb_table, hg_norm[l], ssm_conv_w[l], ssm_conv_b[l], ssm_dt_bias[l], ssm_a_log[l],
                    ssm_d[l], ssm_norm[l], att_q_norm[l], att_k_norm[l], att_sinks[l], w_branch, w_out)
        xf = _ffn(xf, l, ffn2_norm, ffn2_up, ffn2_down)
    return xf.reshape(batch, seq, d)
```

```python
import functools

import jax
import jax.numpy as jnp
from jax import lax
from jax.experimental import pallas as pl
from jax.experimental.pallas import tpu as pltpu

F32 = jnp.float32
BF16 = jnp.bfloat16

NORM_EPS = 1e-6
LN_EPS = 1e-5

V7X_VMEM_BYTES = 64 * 1024 * 1024
VMEM_LIMIT_BYTES = V7X_VMEM_BYTES - 8 * 1024 * 1024
LANES = 128
SUBLANES = 8

BRANCH_WIDTH = 1024
N_BRANCH = 4
GM_GROUPS = 8
GM_CHUNK = 128
HG_HEADS = 8
HG_DK = 128
HG_CHUNK = 64
HG_SUB = 8
HG_HEADS_PER_STEP = 8
SSM_HEADS = 16
SSM_HEADDIM = 64
SSM_GROUPS = 4
SSM_STATE = 128
SSM_CONV = 4
SSM_CHUNK = 128
SSM_TAIL = 8
ATT_HEAD_DIM = 64
ATT_Q_HEADS = 16
ATT_KV_HEADS = 2
ATT_BLOCK = 128

ROW0_MAIN = 0
D_IN_MAIN = 9216
COL_GM = 0
COL_HG_Q = 2048
COL_HG_F = 3072
COL_HG_I = 4096
COL_HG_G = 5120
COL_SSM_Z = 6144
COL_SSM_X = 7168
COL_SSM_BC = 8192
DT_LANE0 = LANES - SSM_HEADS
ROW0_DT = D_IN_MAIN - DT_LANE0
ROW0_ATT = D_IN_MAIN + SSM_HEADS
D_IN_ATT = (ATT_Q_HEADS + 2 * ATT_KV_HEADS) * ATT_HEAD_DIM
COL_ATT_Q = 0
COL_ATT_K = ATT_Q_HEADS * ATT_HEAD_DIM
COL_ATT_V = COL_ATT_K + ATT_KV_HEADS * ATT_HEAD_DIM
ROW0_GATES = ROW0_ATT + D_IN_ATT
D_IN_GATES = N_BRANCH * 4096

D_FF = 11008
D_FF_PADDED = 11264


def _params(semantics):
    return pltpu.CompilerParams(dimension_semantics=semantics, vmem_limit_bytes=VMEM_LIMIT_BYTES)


def _silu(x):
    return x * jax.nn.sigmoid(x)


def _dot(a, b):
    return jnp.dot(a.astype(BF16), b.astype(BF16), preferred_element_type=F32)


def _dot_nt(a, b):
    return lax.dot_general(a.astype(BF16), b.astype(BF16), (((1,), (1,)), ((), ())), preferred_element_type=F32)


def _dot_tn(a, b):
    return lax.dot_general(a.astype(BF16), b.astype(BF16), (((0,), (0,)), ((), ())), preferred_element_type=F32)


def _split3(x):
    hi = x.astype(BF16)
    r1 = x - hi.astype(F32)
    mid = r1.astype(BF16)
    lo = (r1 - mid.astype(F32)).astype(BF16)
    return hi, mid, lo


def _dot_exact_lhs(mask, x):
    m = mask.astype(BF16)
    hi, mid, lo = _split3(x)
    return (jnp.dot(m, hi, preferred_element_type=F32) + jnp.dot(m, mid, preferred_element_type=F32)
            + jnp.dot(m, lo, preferred_element_type=F32))


def _dot_exact_rhs(x, mask):
    m = mask.astype(BF16)
    hi, mid, lo = _split3(x)
    return (jnp.dot(hi, m, preferred_element_type=F32) + jnp.dot(mid, m, preferred_element_type=F32)
            + jnp.dot(lo, m, preferred_element_type=F32))


def _gelu_exact(x):
    return 0.5 * x * (1.0 + lax.erf(x * (2.0 ** -0.5)))


def _rmsnorm_kernel(x_ref, w_ref, o_ref):
    x = x_ref[...]
    ms = jnp.mean(x * x, axis=-1, keepdims=True)
    o_ref[...] = (x * lax.rsqrt(ms + NORM_EPS) * w_ref[...]).astype(o_ref.dtype)


def _rmsnorm(x, w):
    m, d = x.shape
    bm = 512
    return pl.pallas_call(
        _rmsnorm_kernel,
        grid=(m // bm,),
        in_specs=[pl.BlockSpec((bm, d), lambda i: (i, 0)),
                  pl.BlockSpec((1, d), lambda i: (0, 0))],
        out_specs=pl.BlockSpec((bm, d), lambda i: (i, 0)),
        out_shape=jax.ShapeDtypeStruct((m, d), BF16),
        compiler_params=_params(("parallel",)),
        name="rmsnorm",
    )(x, w.reshape(1, d))


def _up_swiglu_kernel(x_ref, wa_ref, wb_ref, wd_ref, o_ref, wdo_ref, *, real_blocks, wd_real_blocks, wd_blocks):
    i = pl.program_id(0)
    j = pl.program_id(1)
    nj = pl.num_programs(1)

    @pl.when(j < real_blocks)
    def _():
        x = x_ref[...]
        a = jnp.dot(x, wa_ref[...].astype(BF16), preferred_element_type=F32)
        b = jnp.dot(x, wb_ref[...].astype(BF16), preferred_element_type=F32)
        o_ref[...] = (_silu(a) * b).astype(o_ref.dtype)

    @pl.when(j >= real_blocks)
    def _():
        o_ref[...] = jnp.zeros_like(o_ref)

    t = i * nj + j

    @pl.when(t < wd_real_blocks)
    def _():
        wdo_ref[...] = wd_ref[...].astype(wdo_ref.dtype)

    @pl.when((t >= wd_real_blocks) & (t < wd_blocks))
    def _():
        wdo_ref[...] = jnp.zeros_like(wdo_ref)


def _up_swiglu(h, w_up, w_down, layer):
    m, k = h.shape
    n_out = w_down.shape[2]
    bm, bn = 2048, 256
    real_blocks = D_FF // bn
    last = real_blocks - 1
    nj = D_FF_PADDED // bn
    wd_rows = LANES
    wd_real_blocks = D_FF // wd_rows
    wd_blocks = D_FF_PADDED // wd_rows
    assert wd_blocks <= (m // bm) * nj

    def wd_step(i, j):
        return jnp.minimum(i * nj + j, wd_blocks - 1)

    return pl.pallas_call(
        functools.partial(_up_swiglu_kernel, real_blocks=real_blocks, wd_real_blocks=wd_real_blocks,
                          wd_blocks=wd_blocks),
        grid=(m // bm, nj),
        in_specs=[pl.BlockSpec((bm, k), lambda i, j: (i, 0), pipeline_mode=pl.Buffered(1)),
                  pl.BlockSpec((None, k, bn), lambda i, j: (layer, 0, jnp.minimum(j, last))),
                  pl.BlockSpec((None, k, bn), lambda i, j: (layer, 0, real_blocks + jnp.minimum(j, last))),
                  pl.BlockSpec((None, wd_rows, n_out),
                               lambda i, j: (layer, jnp.minimum(wd_step(i, j), wd_real_blocks - 1), 0))],
        out_specs=[pl.BlockSpec((bm, bn), lambda i, j: (i, j)),
                   pl.BlockSpec((wd_rows, n_out), lambda i, j: (wd_step(i, j), 0))],
        out_shape=[jax.ShapeDtypeStruct((m, D_FF_PADDED), BF16),
                   jax.ShapeDtypeStruct((D_FF_PADDED, n_out), BF16)],
        compiler_params=_params(("arbitrary", "arbitrary")),
        name="ffn_up_swiglu",
    )(h, w_up, w_up, w_down)


def _down_residual_kernel(a_ref, w_ref, res_ref, o_ref, acc_ref, *, nk):
    k = pl.program_id(2)

    @pl.when(k == 0)
    def _():
        acc_ref[...] = jnp.zeros_like(acc_ref)

    acc_ref[...] += jnp.dot(a_ref[...], w_ref[...], preferred_element_type=F32)

    @pl.when(k == nk - 1)
    def _():
        o_ref[...] = res_ref[...] + 0.5 * acc_ref[...]


def _down_residual(act, w, res):
    m, k = act.shape
    n = w.shape[1]
    bm, bn, nk = 1024, 1024, 4
    bk = k // nk
    return pl.pallas_call(
        functools.partial(_down_residual_kernel, nk=nk),
        grid=(m // bm, n // bn, nk),
        in_specs=[pl.BlockSpec((bm, bk), lambda i, j, kk: (i, kk)),
                  pl.BlockSpec((bk, bn), lambda i, j, kk: (kk, j)),
                  pl.BlockSpec((bm, bn), lambda i, j, kk: (i, j))],
        out_specs=pl.BlockSpec((bm, bn), lambda i, j, kk: (i, j)),
        out_shape=jax.ShapeDtypeStruct((m, n), F32),
        scratch_shapes=[pltpu.VMEM((bm, bn), F32)],
        compiler_params=_params(("parallel", "parallel", "arbitrary")),
        name="ffn_down_residual",
    )(act, w, res)


def _nt_matmul_kernel(x_ref, wt_ref, *refs, n_casts):
    cast_in, o_ref, cast_out = refs[:n_casts], refs[n_casts], refs[n_casts + 1:]
    w = wt_ref[...].astype(BF16)
    o_ref[...] = lax.dot_general(x_ref[...], w, (((1,), (1,)), ((), ())),
                                 preferred_element_type=F32).astype(o_ref.dtype)
    for src, dst in zip(cast_in, cast_out):
        dst[...] = src[...].astype(dst.dtype)


def _in_projection(h, w_in_t, layer, row0, n, bn, name, out_dtype=F32, cast_weights=(), bm=2048,
                   single_buffer_h=True):
    m, k = h.shape
    x_mode = dict(pipeline_mode=pl.Buffered(1)) if single_buffer_h else {}
    nj = n // bn
    steps = (m // bm) * nj
    cast_specs, cast_out_specs, cast_shapes = [], [], []
    for w in cast_weights:
        _, r, c = w.shape
        rows = r // steps
        assert rows * steps == r and rows % (2 * SUBLANES) == 0
        cast_specs.append(pl.BlockSpec((None, rows, c), lambda i, j: (layer, i * nj + j, 0)))
        cast_out_specs.append(pl.BlockSpec((rows, c), lambda i, j: (i * nj + j, 0)))
        cast_shapes.append(jax.ShapeDtypeStruct((r, c), BF16))
    out = pl.pallas_call(
        functools.partial(_nt_matmul_kernel, n_casts=len(cast_weights)),
        grid=(m // bm, nj),
        in_specs=[pl.BlockSpec((bm, k), lambda i, j: (i, 0), **x_mode),
                  pl.BlockSpec((None, pl.Element(bn), pl.Element(k)),
                               lambda i, j: (layer, pl.multiple_of(row0 + j * bn, SUBLANES), 0))] + cast_specs,
        out_specs=[pl.BlockSpec((bm, bn), lambda i, j: (i, j))] + cast_out_specs,
        out_shape=[jax.ShapeDtypeStruct((m, n), out_dtype)] + cast_shapes,
        compiler_params=_params(("arbitrary", "arbitrary")),
        name=name,
    )(h, w_in_t, *cast_weights)
    return out if cast_weights else out[0]


def _out_residual_kernel(x_ref, w_ref, res_ref, o_ref):
    o_ref[...] = res_ref[...] + jnp.dot(x_ref[...], w_ref[...], preferred_element_type=F32)


def _out_residual(merged, w, res):
    m, k = merged.shape
    n = w.shape[1]
    bm, bn = 1024, 1024
    return pl.pallas_call(
        _out_residual_kernel,
        grid=(m // bm, n // bn),
        in_specs=[pl.BlockSpec((bm, k), lambda i, j: (i, 0)),
                  pl.BlockSpec((k, bn), lambda i, j: (0, j)),
                  pl.BlockSpec((bm, bn), lambda i, j: (i, j))],
        out_specs=pl.BlockSpec((bm, bn), lambda i, j: (i, j)),
        out_shape=jax.ShapeDtypeStruct((m, n), F32),
        compiler_params=_params(("parallel", "arbitrary")),
        name="mixer_out_residual",
    )(merged, w, res)


def _merge_kernel(b0_ref, b1_ref, b2_ref, b3_ref, w_ref, g0_ref, g1_ref, g2_ref, g3_ref, o_ref):
    branches = (b0_ref, b1_ref, b2_ref, b3_ref)
    gates = (g0_ref, g1_ref, g2_ref, g3_ref)
    merged = None
    for b in range(N_BRANCH):
        y = jnp.dot(branches[b][...], w_ref[b], preferred_element_type=F32)
        term = jax.nn.sigmoid(gates[b][...].astype(F32)) * y
        merged = term if merged is None else merged + term
    o_ref[...] = merged.astype(o_ref.dtype)


def _merge(branches, w_branch, p_gates):
    m = p_gates.shape[0]
    nb, kb, n = w_branch.shape
    bm, bn = 512, 1024
    gate_blocks = n // bn

    def gate_spec(b):
        return pl.BlockSpec((bm, bn), lambda j, i: (i, b * gate_blocks + j))

    return pl.pallas_call(
        _merge_kernel,
        grid=(n // bn, m // bm),
        in_specs=[pl.BlockSpec((bm, kb), lambda j, i: (i, 0)) for _ in range(nb)]
        + [pl.BlockSpec((nb, kb, bn), lambda j, i: (0, 0, j))]
        + [gate_spec(b) for b in range(nb)],
        out_specs=pl.BlockSpec((bm, bn), lambda j, i: (i, j)),
        out_shape=jax.ShapeDtypeStruct((m, n), BF16),
        compiler_params=_params(("parallel", "arbitrary")),
        name="branch_merge",
    )(*branches, w_branch, p_gates, p_gates, p_gates, p_gates)


def _gmlp_kernel(p_ref, lnw_ref, lnb_ref, ws_ref, bst_ref, o_ref, *, chunks):
    t = GM_CHUNK
    width = BRANCH_WIDTH
    gw = width // GM_GROUPS
    row = lax.broadcasted_iota(jnp.int32, (t, t), 0)
    col = lax.broadcasted_iota(jnp.int32, (t, t), 1)
    causal = row >= col
    mixers = [jnp.where(causal, ws_ref[g], 0.0).astype(BF16) for g in range(GM_GROUPS)]
    for c in range(chunks):
        rows = slice(c * t, (c + 1) * t)
        act = _gelu_exact(p_ref[rows, :])
        u = act[:, :width]
        v = act[:, width:]
        mu = jnp.mean(v, axis=-1, keepdims=True)
        var = jnp.mean(jnp.square(v - mu), axis=-1, keepdims=True)
        vn = (v - mu) * lax.rsqrt(var + LN_EPS) * lnw_ref[...] + lnb_ref[...]
        for g in range(GM_GROUPS):
            cols = slice(g * gw, (g + 1) * gw)
            mixed = _dot(mixers[g], vn[:, cols]) + bst_ref[:, g:g + 1]
            o_ref[rows, cols] = (u[:, cols] * mixed).astype(o_ref.dtype)


def _gmlp_branch(p, ln_w, ln_b, w_s, b_s):
    m = p.shape[0]
    tb = 512
    width = BRANCH_WIDTH
    return pl.pallas_call(
        functools.partial(_gmlp_kernel, chunks=tb // GM_CHUNK),
        grid=(m // tb,),
        in_specs=[pl.BlockSpec((tb, 2 * width), lambda i: (i, COL_GM // (2 * width))),
                  pl.BlockSpec((1, width), lambda i: (0, 0)),
                  pl.BlockSpec((1, width), lambda i: (0, 0)),
                  pl.BlockSpec((GM_GROUPS, GM_CHUNK, GM_CHUNK), lambda i: (0, 0, 0)),
                  pl.BlockSpec((GM_CHUNK, GM_GROUPS), lambda i: (0, 0))],
        out_specs=pl.BlockSpec((tb, width), lambda i: (i, 0)),
        out_shape=jax.ShapeDtypeStruct((m, width), BF16),
        compiler_params=_params(("parallel",)),
        name="gmlp_branch",
    )(p, ln_w.reshape(1, width), ln_b.reshape(1, width), w_s, b_s.T)


def _rows_from(x, index_of_block, block_rows):
    blocks = x.shape[0] // block_rows
    parts = [jnp.broadcast_to(x[index_of_block(b):index_of_block(b) + 1], (block_rows, x.shape[1]))
             for b in range(blocks)]
    return jnp.concatenate(parts, axis=0)


def _hgrn2_intra_scores(q, k, g, seg_ref):
    c_len, dk = q.shape
    row = lax.broadcasted_iota(jnp.int32, (c_len, c_len), 0)
    col = lax.broadcasted_iota(jnp.int32, (c_len, c_len), 1)
    pos = lax.broadcasted_iota(jnp.int32, (c_len, dk), 0)
    sub = pos & (HG_SUB - 1)

    pieces = []
    for s in range(HG_SUB):
        g_s = _rows_from(g, lambda b: b * HG_SUB + s, HG_SUB)
        k_s = _rows_from(k, lambda b: b * HG_SUB + s, HG_SUB)
        decay = jnp.exp(jnp.where(sub >= s, g - g_s, -jnp.inf))
        pieces.append((q * k_s * decay).astype(BF16))
    diag = jnp.dot(jnp.concatenate(pieces, axis=1), seg_ref[...], preferred_element_type=F32)
    scores = jnp.where((row ^ col) < HG_SUB, diag, 0.0)

    width = HG_SUB
    while width < c_len:
        right = (pos & width) != 0
        g_ref = _rows_from(g, lambda b: b * 2 * width + width - 1, 2 * width)
        q_s = q * jnp.exp(jnp.where(right, g - g_ref, -jnp.inf))
        k_s = k * jnp.exp(jnp.where(right, -jnp.inf, g_ref - g))
        cross = _dot_nt(q_s, k_s)
        scores = scores + jnp.where((row ^ col) < 2 * width, cross, 0.0)
        width *= 2
    return scores


def _hgrn2_kernel(q_ref, f_ref, i_ref, g_ref, lbt_ref, nw_ref, seg_ref, o_ref, state_ref, *, layer, chunks):
    c_len = HG_CHUNK

    @pl.when(pl.program_id(2) == 0)
    def _():
        state_ref[...] = jnp.zeros_like(state_ref)

    table = lbt_ref[...]
    e = jnp.exp(table - jnp.max(table, axis=0, keepdims=True))
    probs = e / jnp.sum(e, axis=0, keepdims=True)
    lb_all = jnp.sum(probs[:layer + 1], axis=0, keepdims=True) - probs[0:1]

    row = lax.broadcasted_iota(jnp.int32, (c_len, c_len), 0)
    col = lax.broadcasted_iota(jnp.int32, (c_len, c_len), 1)
    cumsum_mat = row >= col

    def chunk_body(c, carry):
        r0 = pl.multiple_of(c * c_len, c_len)
        rows = pl.ds(r0, c_len)
        for h in range(HG_HEADS_PER_STEP):
            cols = slice(h * HG_DK, (h + 1) * HG_DK)
            lb = lb_all[:, cols]
            q = _silu(q_ref[rows, cols])
            f = lb + (1.0 - lb) * jax.nn.sigmoid(f_ref[rows, cols])
            log_f = jnp.log(f)
            k = 1.0 - f
            v = i_ref[rows, cols]
            g = _dot_exact_lhs(cumsum_mat, log_f)
            state_t = state_ref[h]
            o = _dot_nt(q * jnp.exp(g), state_t)
            o = o + _dot(_hgrn2_intra_scores(q, k, g, seg_ref), v)
            g_last = g[c_len - 1:c_len]
            k_s = k * jnp.exp(g_last - g)
            state_ref[h] = state_t * jnp.exp(g_last) + _dot_tn(v, k_s)
            ms = jnp.mean(o * o, axis=-1, keepdims=True)
            y = o * lax.rsqrt(ms + NORM_EPS) * nw_ref[:, cols]
            o_ref[rows, cols] = (y * _silu(g_ref[rows, cols])).astype(o_ref.dtype)
        return carry

    lax.fori_loop(0, chunks, chunk_body, 0, unroll=2)


def _hgrn2_branch(p, lb_table, norm_w, layer, batch, seq):
    m = p.shape[0]
    ts = 512
    steps = seq // ts
    depth = lb_table.shape[0]
    wblk = HG_HEADS_PER_STEP * HG_DK

    def in_spec(col0):
        return pl.BlockSpec((ts, wblk), lambda b, h, t: (b * steps + t, col0 // wblk + h))

    seg = (jnp.arange(HG_SUB * HG_DK)[:, None] // HG_DK == jnp.arange(HG_CHUNK)[None, :] % HG_SUB).astype(BF16)

    return pl.pallas_call(
        functools.partial(_hgrn2_kernel, layer=layer, chunks=ts // HG_CHUNK),
        grid=(batch, HG_HEADS // HG_HEADS_PER_STEP, steps),
        in_specs=[in_spec(COL_HG_Q), in_spec(COL_HG_F), in_spec(COL_HG_I), in_spec(COL_HG_G),
                  pl.BlockSpec((depth, wblk), lambda b, h, t: (0, h)),
                  pl.BlockSpec((1, wblk), lambda b, h, t: (0, h)),
                  pl.BlockSpec((HG_SUB * HG_DK, HG_CHUNK), lambda b, h, t: (0, 0))],
        out_specs=pl.BlockSpec((ts, wblk), lambda b, h, t: (b * steps + t, h)),
        out_shape=jax.ShapeDtypeStruct((m, BRANCH_WIDTH), BF16),
        scratch_shapes=[pltpu.VMEM((HG_HEADS_PER_STEP, HG_DK, HG_DK), F32)],
        compiler_params=_params(("parallel", "parallel", "arbitrary")),
        name="hgrn2_branch",
    )(p, p, p, p, lb_table, norm_w.reshape(1, BRANCH_WIDTH), seg)


def _ssd_kernel(z_ref, x_ref, bc_ref, dt_ref, cw_ref, cb_ref, dtb_ref, alog_ref, dsk_ref, nw_ref,
                expand_ref, o_ref, tail_ref, state_ref):
    length = SSM_CHUNK
    inner = BRANCH_WIDTH
    gw = inner // SSM_GROUPS
    n = SSM_STATE

    @pl.when(pl.program_id(1) == 0)
    def _():
        tail_ref[...] = jnp.zeros_like(tail_ref)
        state_ref[...] = jnp.zeros_like(state_ref)

    xbc = jnp.concatenate([x_ref[...], bc_ref[...]], axis=1)
    prev_rows = tail_ref[...]
    head_sub = lax.broadcasted_iota(jnp.int32, (SSM_TAIL, xbc.shape[1]), 0)
    conv = cb_ref[...] + xbc * cw_ref[SSM_CONV - 1:SSM_CONV, :]
    for shift in range(1, SSM_CONV):
        rolled = pltpu.roll(xbc, shift, axis=0)
        head = jnp.where(head_sub < shift, pltpu.roll(prev_rows, shift, axis=0), rolled[:SSM_TAIL])
        shifted = jnp.concatenate([head, rolled[SSM_TAIL:]], axis=0)
        conv = conv + shifted * cw_ref[SSM_CONV - 1 - shift:SSM_CONV - shift, :]
    tail_ref[...] = xbc[length - SSM_TAIL:length]
    xbc_act = _silu(conv)
    x_c = xbc_act[:, :inner]
    b_m = xbc_act[:, inner:inner + SSM_GROUPS * n]
    c_m = xbc_act[:, inner + SSM_GROUPS * n:]

    dt = jax.nn.softplus(dt_ref[...] + dtb_ref[...])
    a = dt * (-jnp.exp(alog_ref[...]))
    row = lax.broadcasted_iota(jnp.int32, (length, length), 0)
    col = lax.broadcasted_iota(jnp.int32, (length, length), 1)
    causal = row >= col
    cs = _dot_exact_lhs(causal, a)
    cs_t = cs.T
    expand = expand_ref[...]
    dt_e = _dot_exact_rhs(dt, expand)
    cs_e = _dot_exact_rhs(cs, expand)
    cs_last_e = cs_e[length - 1:length]
    xs = x_c * dt_e
    in_decay_e = jnp.exp(cs_e)
    state_decay_e = jnp.exp(cs_last_e - cs_e)
    chunk_decay_e = jnp.exp(cs_last_e)
    z = z_ref[...]

    heads_per_group = SSM_HEADS // SSM_GROUPS
    p = SSM_HEADDIM
    for g in range(SSM_GROUPS):
        gcols = slice(g * gw, (g + 1) * gw)
        b_g = b_m[:, g * n:(g + 1) * n]
        c_g = c_m[:, g * n:(g + 1) * n]
        cb = _dot_nt(c_g, b_g)
        prev_t = state_ref[g]
        y_off = _dot(c_g, prev_t) * in_decay_e[:, gcols]
        parts = []
        for r in range(heads_per_group):
            h = g * heads_per_group + r
            lane = DT_LANE0 + h
            decay = jnp.exp(jnp.where(causal, cs[:, lane:lane + 1] - cs_t[lane:lane + 1, :], -jnp.inf))
            parts.append(_dot(cb * decay, xs[:, h * p:(h + 1) * p]))
        y_diag = jnp.concatenate(parts, axis=1)
        state_ref[g] = prev_t * chunk_decay_e[:, gcols] + _dot_tn(b_g, xs[:, gcols] * state_decay_e[:, gcols])
        y = (y_diag + y_off) + x_c[:, gcols] * dsk_ref[:, gcols]
        y = y * _silu(z[:, gcols])
        ms = jnp.mean(y * y, axis=-1, keepdims=True)
        o_ref[:, gcols] = (y * lax.rsqrt(ms + NORM_EPS) * nw_ref[:, gcols]).astype(o_ref.dtype)


def _ssd_branch(p_main, p_dt, conv_w, conv_b, dt_bias, a_log, d_skip, norm_w, batch, seq):
    m = p_main.shape[0]
    length = SSM_CHUNK
    steps = seq // length
    inner = BRANCH_WIDTH
    conv_dim = conv_w.shape[1]

    def pad_heads(vec):
        return jnp.pad(vec, (DT_LANE0, 0)).reshape(1, LANES)

    expand = (jnp.arange(LANES)[:, None] == (DT_LANE0 + jnp.arange(inner) // SSM_HEADDIM)[None, :]).astype(BF16)
    d_skip_e = jnp.repeat(d_skip, SSM_HEADDIM).reshape(1, inner)

    def rows(width, col0):
        return pl.BlockSpec((length, width), lambda b, t: (b * steps + t, col0 // width))

    def whole(shape):
        return pl.BlockSpec(shape, lambda b, t: (0,) * len(shape))

    return pl.pallas_call(
        _ssd_kernel,
        grid=(batch, steps),
        in_specs=[rows(inner, COL_SSM_Z), rows(inner, COL_SSM_X), rows(inner, COL_SSM_BC),
                  rows(LANES, 0),
                  whole((SSM_CONV, conv_dim)), whole((1, conv_dim)),
                  whole((1, LANES)), whole((1, LANES)), whole((1, inner)), whole((1, inner)),
                  whole((LANES, inner))],
        out_specs=pl.BlockSpec((length, inner), lambda b, t: (b * steps + t, 0)),
        out_shape=jax.ShapeDtypeStruct((m, inner), BF16),
        scratch_shapes=[pltpu.VMEM((SSM_TAIL, conv_dim), F32),
                        pltpu.VMEM((SSM_GROUPS, SSM_STATE, inner // SSM_GROUPS), F32)],
        compiler_params=_params(("parallel", "arbitrary")),
        name="ssd_branch",
    )(p_main, p_main, p_main, p_dt, conv_w, conv_b.reshape(1, conv_dim), pad_heads(dt_bias), pad_heads(a_log),
      d_skip_e, norm_w.reshape(1, inner), expand)


def _swa_kernel(sink_ref, q_ref, kc_ref, kp_ref, vc_ref, vp_ref, qw_ref, kw_ref, seg_ref, o_ref):
    t = ATT_BLOCK
    hd = ATT_HEAD_DIM
    group = ATT_Q_HEADS // ATT_KV_HEADS
    pairs = group // 2
    blk = pl.program_id(1)
    q = q_ref[...]
    keys = jnp.concatenate([kp_ref[...], kc_ref[...]], axis=0)
    vals = jnp.concatenate([vp_ref[...], vc_ref[...]], axis=0)
    lane = lax.broadcasted_iota(jnp.int32, (2 * t, LANES), 1)
    low = lane < hd

    ssq = _dot_exact_rhs(q * q, seg_ref[...])
    q_scale = lax.rsqrt(ssq * (1.0 / hd) + NORM_EPS)

    k_sq = keys * keys
    k_ssq = jnp.where(low, jnp.sum(jnp.where(low, k_sq, 0.0), axis=-1, keepdims=True),
                      jnp.sum(jnp.where(low, 0.0, k_sq), axis=-1, keepdims=True))
    k_fold = keys * lax.rsqrt(k_ssq * (1.0 / hd) + NORM_EPS) * (kw_ref[...] * qw_ref[...] * (hd ** -0.5))

    qi = lax.broadcasted_iota(jnp.int32, (t, 2 * t), 0)
    kj = lax.broadcasted_iota(jnp.int32, (t, 2 * t), 1)
    first_key = jnp.where(blk > 0, 0, t)
    bias = jnp.where((kj > qi) & (kj <= qi + t) & (kj >= first_key), 0.0, -jnp.inf)

    out_pairs = [None] * (ATT_Q_HEADS // 2)
    for c in range(ATT_KV_HEADS):
        own_low = c % 2 == 0
        k_own = jnp.where(low if own_low else ~low, k_fold, 0.0)
        v_own = jnp.where(low if own_low else ~low, vals, 0.0)
        k_other = pltpu.roll(k_own, hd, axis=1)
        v_other = pltpu.roll(v_own, hd, axis=1)
        for parity in range(2):
            k_sel = (k_own if (parity == 0) == own_low else k_other).astype(BF16)
            v_sel = (v_own if (parity == 0) == own_low else v_other).astype(BF16)
            pair0 = c * pairs
            q_stack = jnp.concatenate([q[:, (pair0 + pp) * LANES:(pair0 + pp + 1) * LANES]
                                       for pp in range(pairs)], axis=0).astype(BF16)
            s_all = lax.dot_general(q_stack, k_sel, (((1,), (1,)), ((), ())), preferred_element_type=F32)
            e_parts, inv_parts = [], []
            for pp in range(pairs):
                h = c * group + 2 * pp + parity
                s = s_all[pp * t:(pp + 1) * t] * q_scale[:, h:h + 1] + bias
                sink = sink_ref[h]
                mx = jnp.maximum(jnp.max(s, axis=-1, keepdims=True), sink)
                e = jnp.exp(s - mx)
                e_parts.append(e)
                inv_parts.append(1.0 / (jnp.sum(e, axis=-1, keepdims=True) + jnp.exp(sink - mx)))
            pv = jnp.dot(jnp.concatenate(e_parts, axis=0).astype(BF16), v_sel, preferred_element_type=F32)
            for pp in range(pairs):
                half = pv[pp * t:(pp + 1) * t] * inv_parts[pp]
                idx = pair0 + pp
                out_pairs[idx] = half if out_pairs[idx] is None else out_pairs[idx] + half
    for idx, pair in enumerate(out_pairs):
        o_ref[:, idx * LANES:(idx + 1) * LANES] = pair.astype(o_ref.dtype)


def _swa_branch(p, q_norm_w, k_norm_w, sinks, batch, seq):
    m = p.shape[0]
    t = ATT_BLOCK
    steps = seq // t
    qw = ATT_Q_HEADS * ATT_HEAD_DIM
    kvw = ATT_KV_HEADS * ATT_HEAD_DIM

    def cur(width, col0):
        return pl.BlockSpec((t, width), lambda b, i: (b * steps + i, col0 // width))

    def prev(width, col0):
        return pl.BlockSpec((t, width), lambda b, i: (b * steps + jnp.maximum(i - 1, 0), col0 // width))

    def both_halves(w):
        return jnp.concatenate([w, w]).reshape(1, LANES)

    seg = (jnp.arange(qw)[:, None] // ATT_HEAD_DIM == jnp.arange(LANES)[None, :]).astype(BF16)

    return pl.pallas_call(
        _swa_kernel,
        grid=(batch, steps),
        in_specs=[pl.BlockSpec(memory_space=pltpu.SMEM),
                  cur(qw, COL_ATT_Q), cur(kvw, COL_ATT_K), prev(kvw, COL_ATT_K),
                  cur(kvw, COL_ATT_V), prev(kvw, COL_ATT_V),
                  pl.BlockSpec((1, LANES), lambda b, i: (0, 0)),
                  pl.BlockSpec((1, LANES), lambda b, i: (0, 0)),
                  pl.BlockSpec((qw, LANES), lambda b, i: (0, 0))],
        out_specs=pl.BlockSpec((t, qw), lambda b, i: (b * steps + i, 0)),
        out_shape=jax.ShapeDtypeStruct((m, qw), BF16),
        compiler_params=_params(("parallel", "parallel")),
        name="swa_branch",
    )(sinks, p, p, p, p, p, both_halves(q_norm_w), both_halves(k_norm_w), seg)


def _ffn(x, layer, norm_w, w_up, w_down):
    h = _rmsnorm(x, norm_w[layer])
    act, w_down_padded = _up_swiglu(h, w_up, w_down, layer)
    return _down_residual(act, w_down_padded, x)


def _mixer(x, layer, batch, seq, mix_norm, w_in_t, gm_ln_w, gm_ln_b, gm_w_s, gm_b_s, hg_lb_table, hg_norm,
           ssm_conv_w, ssm_conv_b, ssm_dt_bias, ssm_a_log, ssm_d, ssm_norm,
           att_q_norm, att_k_norm, att_sinks, w_branch, w_out):
    h = _rmsnorm(x, mix_norm)
    p_main = _in_projection(h, w_in_t, layer, ROW0_MAIN, D_IN_MAIN, 512, "mixer_in_projection_main")
    nb, kb, d = w_branch.shape[1:]
    p_gates, w_branch_b, w_out_b = _in_projection(
        h, w_in_t, layer, ROW0_GATES, D_IN_GATES, 512, "mixer_in_projection_gates", BF16,
        cast_weights=(w_branch.reshape(-1, nb * kb, d), w_out))
    p_att = _in_projection(h, w_in_t, layer, ROW0_ATT, D_IN_ATT, 256, "mixer_in_projection_att",
                           bm=1024, single_buffer_h=False)
    p_dt = _in_projection(h, w_in_t, layer, ROW0_DT, LANES, LANES, "mixer_in_projection_dt",
                          bm=1024, single_buffer_h=False)
    branches = (
        _gmlp_branch(p_main, gm_ln_w, gm_ln_b, gm_w_s, gm_b_s),
        _hgrn2_branch(p_main, hg_lb_table, hg_norm, layer, batch, seq),
        _ssd_branch(p_main, p_dt, ssm_conv_w, ssm_conv_b, ssm_dt_bias, ssm_a_log, ssm_d, ssm_norm, batch, seq),
        _swa_branch(p_att, att_q_norm, att_k_norm, att_sinks, batch, seq),
    )
    merged = _merge(branches, w_branch_b.reshape(nb, kb, d), p_gates)
    return _out_residual(merged, w_out_b, x)


def kernel(x, ffn1_norm, ffn1_up, ffn1_down, mix_norm, w_in, gm_ln_w, gm_ln_b, gm_w_s, gm_b_s, hg_lb_table, hg_norm, ssm_conv_w, ssm_conv_b, ssm_dt_bias, ssm_a_log, ssm_d, ssm_norm, att_q_norm, att_k_norm, att_sinks, w_branch, w_out, ffn2_norm, ffn2_up, ffn2_down):
    batch, seq, d = x.shape
    depth = ffn1_norm.shape[0]
    xf = x.reshape(batch * seq, d)
    w_in_t = jnp.transpose(w_in, (0, 2, 1))
    for l in range(depth):
        xf = _ffn(xf, l, ffn1_norm, ffn1_up, ffn1_down)
        xf = _mixer(xf, l, batch, seq, mix_norm[l], w_in_t, gm_ln_w[l], gm_ln_b[l], gm_w_s[l], gm_b_s[l],
                    hg_lb_table, hg_norm[l], ssm_conv_w[l], ssm_conv_b[l], ssm_dt_bias[l], ssm_a_log[l],
                    ssm_d[l], ssm_norm[l], att_q_norm[l], att_k_norm[l], att_sinks[l], w_branch, w_out)
        xf = _ffn(xf, l, ffn2_norm, ffn2_up, ffn2_down)
    return xf.reshape(batch, seq, d)
```

```python
import functools

import jax
import jax.numpy as jnp
from jax import lax
from jax.experimental import pallas as pl
from jax.experimental.pallas import tpu as pltpu

F32 = jnp.float32
BF16 = jnp.bfloat16

NORM_EPS = 1e-6
LN_EPS = 1e-5

V7X_VMEM_BYTES = 64 * 1024 * 1024
VMEM_LIMIT_BYTES = V7X_VMEM_BYTES - 8 * 1024 * 1024
LANES = 128
SUBLANES = 8

BRANCH_WIDTH = 1024
N_BRANCH = 4
GM_GROUPS = 8
GM_CHUNK = 128
HG_HEADS = 8
HG_DK = 128
HG_CHUNK = 64
HG_SUB = 8
HG_HEADS_PER_STEP = 8
SSM_HEADS = 16
SSM_HEADDIM = 64
SSM_GROUPS = 4
SSM_STATE = 128
SSM_CONV = 4
SSM_CHUNK = 128
SSM_TAIL = 8
ATT_HEAD_DIM = 64
ATT_Q_HEADS = 16
ATT_KV_HEADS = 2
ATT_BLOCK = 128

ROW0_MAIN = 0
D_IN_MAIN = 9216
COL_GM = 0
COL_HG_Q = 2048
COL_HG_F = 3072
COL_HG_I = 4096
COL_HG_G = 5120
COL_SSM_Z = 6144
COL_SSM_X = 7168
COL_SSM_BC = 8192
DT_LANE0 = LANES - SSM_HEADS
ROW0_DT = D_IN_MAIN - DT_LANE0
ROW0_ATT = D_IN_MAIN + SSM_HEADS
D_IN_ATT = (ATT_Q_HEADS + 2 * ATT_KV_HEADS) * ATT_HEAD_DIM
COL_ATT_Q = 0
COL_ATT_K = ATT_Q_HEADS * ATT_HEAD_DIM
COL_ATT_V = COL_ATT_K + ATT_KV_HEADS * ATT_HEAD_DIM
ROW0_GATES = ROW0_ATT + D_IN_ATT
D_IN_GATES = N_BRANCH * 4096

D_FF = 11008
D_FF_PADDED = 11264


def _params(semantics):
    return pltpu.CompilerParams(dimension_semantics=semantics, vmem_limit_bytes=VMEM_LIMIT_BYTES)


def _silu(x):
    return x * jax.nn.sigmoid(x)


def _dot(a, b):
    return jnp.dot(a.astype(BF16), b.astype(BF16), preferred_element_type=F32)


def _dot_nt(a, b):
    return lax.dot_general(a.astype(BF16), b.astype(BF16), (((1,), (1,)), ((), ())), preferred_element_type=F32)


def _dot_tn(a, b):
    return lax.dot_general(a.astype(BF16), b.astype(BF16), (((0,), (0,)), ((), ())), preferred_element_type=F32)


def _split3(x):
    hi = x.astype(BF16)
    r1 = x - hi.astype(F32)
    mid = r1.astype(BF16)
    lo = (r1 - mid.astype(F32)).astype(BF16)
    return hi, mid, lo


def _dot_exact_lhs(mask, x):
    m = mask.astype(BF16)
    hi, mid, lo = _split3(x)
    return (jnp.dot(m, hi, preferred_element_type=F32) + jnp.dot(m, mid, preferred_element_type=F32)
            + jnp.dot(m, lo, preferred_element_type=F32))


def _dot_exact_rhs(x, mask):
    m = mask.astype(BF16)
    hi, mid, lo = _split3(x)
    return (jnp.dot(hi, m, preferred_element_type=F32) + jnp.dot(mid, m, preferred_element_type=F32)
            + jnp.dot(lo, m, preferred_element_type=F32))


def _gelu_exact(x):
    return 0.5 * x * (1.0 + lax.erf(x * (2.0 ** -0.5)))


def _rmsnorm_kernel(x_ref, w_ref, o_ref):
    x = x_ref[...]
    ms = jnp.mean(x * x, axis=-1, keepdims=True)
    o_ref[...] = (x * lax.rsqrt(ms + NORM_EPS) * w_ref[...]).astype(o_ref.dtype)


def _rmsnorm(x, w):
    m, d = x.shape
    bm = 512
    return pl.pallas_call(
        _rmsnorm_kernel,
        grid=(m // bm,),
        in_specs=[pl.BlockSpec((bm, d), lambda i: (i, 0)),
                  pl.BlockSpec((1, d), lambda i: (0, 0))],
        out_specs=pl.BlockSpec((bm, d), lambda i: (i, 0)),
        out_shape=jax.ShapeDtypeStruct((m, d), BF16),
        compiler_params=_params(("parallel",)),
        name="rmsnorm",
    )(x, w.reshape(1, d))


def _up_swiglu_kernel(x_ref, wa_ref, wb_ref, wd_ref, o_ref, wdo_ref, *, real_blocks, wd_real_blocks, wd_blocks):
    i = pl.program_id(0)
    j = pl.program_id(1)
    nj = pl.num_programs(1)

    @pl.when(j < real_blocks)
    def _():
        x = x_ref[...]
        a = jnp.dot(x, wa_ref[...].astype(BF16), preferred_element_type=F32)
        b = jnp.dot(x, wb_ref[...].astype(BF16), preferred_element_type=F32)
        o_ref[...] = (_silu(a) * b).astype(o_ref.dtype)

    @pl.when(j >= real_blocks)
    def _():
        o_ref[...] = jnp.zeros_like(o_ref)

    t = i * nj + j

    @pl.when(t < wd_real_blocks)
    def _():
        wdo_ref[...] = wd_ref[...].astype(wdo_ref.dtype)

    @pl.when((t >= wd_real_blocks) & (t < wd_blocks))
    def _():
        wdo_ref[...] = jnp.zeros_like(wdo_ref)


def _up_swiglu(h, w_up, w_down, layer):
    m, k = h.shape
    n_out = w_down.shape[2]
    bm, bn = 2048, 256
    real_blocks = D_FF // bn
    last = real_blocks - 1
    nj = D_FF_PADDED // bn
    wd_rows = LANES
    wd_real_blocks = D_FF // wd_rows
    wd_blocks = D_FF_PADDED // wd_rows
    assert wd_blocks <= (m // bm) * nj

    def wd_step(i, j):
        return jnp.minimum(i * nj + j, wd_blocks - 1)

    return pl.pallas_call(
        functools.partial(_up_swiglu_kernel, real_blocks=real_blocks, wd_real_blocks=wd_real_blocks,
                          wd_blocks=wd_blocks),
        grid=(m // bm, nj),
        in_specs=[pl.BlockSpec((bm, k), lambda i, j: (i, 0), pipeline_mode=pl.Buffered(1)),
                  pl.BlockSpec((None, k, bn), lambda i, j: (layer, 0, jnp.minimum(j, last))),
                  pl.BlockSpec((None, k, bn), lambda i, j: (layer, 0, real_blocks + jnp.minimum(j, last))),
                  pl.BlockSpec((None, wd_rows, n_out),
                               lambda i, j: (layer, jnp.minimum(wd_step(i, j), wd_real_blocks - 1), 0))],
        out_specs=[pl.BlockSpec((bm, bn), lambda i, j: (i, j)),
                   pl.BlockSpec((wd_rows, n_out), lambda i, j: (wd_step(i, j), 0))],
        out_shape=[jax.ShapeDtypeStruct((m, D_FF_PADDED), BF16),
                   jax.ShapeDtypeStruct((D_FF_PADDED, n_out), BF16)],
        compiler_params=_params(("arbitrary", "arbitrary")),
        name="ffn_up_swiglu",
    )(h, w_up, w_up, w_down)


def _down_residual_kernel(a_ref, w_ref, res_ref, o_ref, acc_ref, *, nk):
    k = pl.program_id(2)

    def partial_product():
        return jnp.dot(a_ref[...], w_ref[...], preferred_element_type=F32)

    @pl.when(k == 0)
    def _():
        acc_ref[...] = partial_product()

    @pl.when((k > 0) & (k < nk - 1))
    def _():
        acc_ref[...] += partial_product()

    @pl.when(k == nk - 1)
    def _():
        o_ref[...] = res_ref[...] + 0.5 * (acc_ref[...] + partial_product())


def _down_residual(act, w, res):
    m, k = act.shape
    n = w.shape[1]
    bm, bn, nk = 1024, 1024, 4
    bk = k // nk
    return pl.pallas_call(
        functools.partial(_down_residual_kernel, nk=nk),
        grid=(m // bm, n // bn, nk),
        in_specs=[pl.BlockSpec((bm, bk), lambda i, j, kk: (i, kk)),
                  pl.BlockSpec((bk, bn), lambda i, j, kk: (kk, j)),
                  pl.BlockSpec((bm, bn), lambda i, j, kk: (i, j))],
        out_specs=pl.BlockSpec((bm, bn), lambda i, j, kk: (i, j)),
        out_shape=jax.ShapeDtypeStruct((m, n), F32),
        scratch_shapes=[pltpu.VMEM((bm, bn), F32)],
        compiler_params=_params(("parallel", "parallel", "arbitrary")),
        name="ffn_down_residual",
    )(act, w, res)


def _nt_matmul_kernel(x_ref, wt_ref, *refs, n_casts):
    cast_in, o_ref, cast_out = refs[:n_casts], refs[n_casts], refs[n_casts + 1:]
    w = wt_ref[...].astype(BF16)
    o_ref[...] = lax.dot_general(x_ref[...], w, (((1,), (1,)), ((), ())),
                                 preferred_element_type=F32).astype(o_ref.dtype)
    for src, dst in zip(cast_in, cast_out):
        dst[...] = src[...].astype(dst.dtype)


def _in_projection(h, w_in_t, layer, row0, n, bn, name, out_dtype=F32, cast_weights=(), bm=2048,
                   single_buffer_h=True):
    m, k = h.shape
    x_mode = dict(pipeline_mode=pl.Buffered(1)) if single_buffer_h else {}
    nj = n // bn
    steps = (m // bm) * nj
    cast_specs, cast_out_specs, cast_shapes = [], [], []
    for w in cast_weights:
        _, r, c = w.shape
        rows = r // steps
        assert rows * steps == r and rows % (2 * SUBLANES) == 0
        cast_specs.append(pl.BlockSpec((None, rows, c), lambda i, j: (layer, i * nj + j, 0)))
        cast_out_specs.append(pl.BlockSpec((rows, c), lambda i, j: (i * nj + j, 0)))
        cast_shapes.append(jax.ShapeDtypeStruct((r, c), BF16))
    out = pl.pallas_call(
        functools.partial(_nt_matmul_kernel, n_casts=len(cast_weights)),
        grid=(m // bm, nj),
        in_specs=[pl.BlockSpec((bm, k), lambda i, j: (i, 0), **x_mode),
                  pl.BlockSpec((None, pl.Element(bn), pl.Element(k)),
                               lambda i, j: (layer, pl.multiple_of(row0 + j * bn, SUBLANES), 0))] + cast_specs,
        out_specs=[pl.BlockSpec((bm, bn), lambda i, j: (i, j))] + cast_out_specs,
        out_shape=[jax.ShapeDtypeStruct((m, n), out_dtype)] + cast_shapes,
        compiler_params=_params(("arbitrary", "arbitrary")),
        name=name,
    )(h, w_in_t, *cast_weights)
    return out if cast_weights else out[0]


def _out_residual_kernel(x_ref, w_ref, res_ref, o_ref):
    o_ref[...] = res_ref[...] + jnp.dot(x_ref[...], w_ref[...], preferred_element_type=F32)


def _out_residual(merged, w, res):
    m, k = merged.shape
    n = w.shape[1]
    bm, bn = 1024, 1024
    return pl.pallas_call(
        _out_residual_kernel,
        grid=(m // bm, n // bn),
        in_specs=[pl.BlockSpec((bm, k), lambda i, j: (i, 0)),
                  pl.BlockSpec((k, bn), lambda i, j: (0, j)),
                  pl.BlockSpec((bm, bn), lambda i, j: (i, j))],
        out_specs=pl.BlockSpec((bm, bn), lambda i, j: (i, j)),
        out_shape=jax.ShapeDtypeStruct((m, n), F32),
        compiler_params=_params(("parallel", "arbitrary")),
        name="mixer_out_residual",
    )(merged, w, res)


def _merge_kernel(b0_ref, b1_ref, b2_ref, b3_ref, w_ref, g0_ref, g1_ref, g2_ref, g3_ref, o_ref):
    branches = (b0_ref, b1_ref, b2_ref, b3_ref)
    gates = (g0_ref, g1_ref, g2_ref, g3_ref)
    merged = None
    for b in range(N_BRANCH):
        y = jnp.dot(branches[b][...], w_ref[b], preferred_element_type=F32)
        term = jax.nn.sigmoid(gates[b][...].astype(F32)) * y
        merged = term if merged is None else merged + term
    o_ref[...] = merged.astype(o_ref.dtype)


def _merge(branches, w_branch, p_gates):
    m = p_gates.shape[0]
    nb, kb, n = w_branch.shape
    bm, bn = 512, 1024
    gate_blocks = n // bn

    def gate_spec(b):
        return pl.BlockSpec((bm, bn), lambda j, i: (i, b * gate_blocks + j))

    return pl.pallas_call(
        _merge_kernel,
        grid=(n // bn, m // bm),
        in_specs=[pl.BlockSpec((bm, kb), lambda j, i: (i, 0)) for _ in range(nb)]
        + [pl.BlockSpec((nb, kb, bn), lambda j, i: (0, 0, j))]
        + [gate_spec(b) for b in range(nb)],
        out_specs=pl.BlockSpec((bm, bn), lambda j, i: (i, j)),
        out_shape=jax.ShapeDtypeStruct((m, n), BF16),
        compiler_params=_params(("parallel", "arbitrary")),
        name="branch_merge",
    )(*branches, w_branch, p_gates, p_gates, p_gates, p_gates)


def _gmlp_kernel(p_ref, lnw_ref, lnb_ref, ws_ref, bst_ref, o_ref, *, chunks):
    t = GM_CHUNK
    width = BRANCH_WIDTH
    gw = width // GM_GROUPS
    row = lax.broadcasted_iota(jnp.int32, (t, t), 0)
    col = lax.broadcasted_iota(jnp.int32, (t, t), 1)
    causal = row >= col
    mixers = [jnp.where(causal, ws_ref[g], 0.0).astype(BF16) for g in range(GM_GROUPS)]
    for c in range(chunks):
        rows = slice(c * t, (c + 1) * t)
        act = _gelu_exact(p_ref[rows, :])
        u = act[:, :width]
        v = act[:, width:]
        mu = jnp.mean(v, axis=-1, keepdims=True)
        var = jnp.mean(jnp.square(v - mu), axis=-1, keepdims=True)
        vn = (v - mu) * lax.rsqrt(var + LN_EPS) * lnw_ref[...] + lnb_ref[...]
        for g in range(GM_GROUPS):
            cols = slice(g * gw, (g + 1) * gw)
            mixed = _dot(mixers[g], vn[:, cols]) + bst_ref[:, g:g + 1]
            o_ref[rows, cols] = (u[:, cols] * mixed).astype(o_ref.dtype)


def _gmlp_branch(p, ln_w, ln_b, w_s, b_s):
    m = p.shape[0]
    tb = 512
    width = BRANCH_WIDTH
    return pl.pallas_call(
        functools.partial(_gmlp_kernel, chunks=tb // GM_CHUNK),
        grid=(m // tb,),
        in_specs=[pl.BlockSpec((tb, 2 * width), lambda i: (i, COL_GM // (2 * width))),
                  pl.BlockSpec((1, width), lambda i: (0, 0)),
                  pl.BlockSpec((1, width), lambda i: (0, 0)),
                  pl.BlockSpec((GM_GROUPS, GM_CHUNK, GM_CHUNK), lambda i: (0, 0, 0)),
                  pl.BlockSpec((GM_CHUNK, GM_GROUPS), lambda i: (0, 0))],
        out_specs=pl.BlockSpec((tb, width), lambda i: (i, 0)),
        out_shape=jax.ShapeDtypeStruct((m, width), BF16),
        compiler_params=_params(("parallel",)),
        name="gmlp_branch",
    )(p, ln_w.reshape(1, width), ln_b.reshape(1, width), w_s, b_s.T)


def _rows_from(x, index_of_block, block_rows):
    blocks = x.shape[0] // block_rows
    parts = [jnp.broadcast_to(x[index_of_block(b):index_of_block(b) + 1], (block_rows, x.shape[1]))
             for b in range(blocks)]
    return jnp.concatenate(parts, axis=0)


def _hgrn2_intra_scores(q, k, g, seg_ref):
    c_len, dk = q.shape
    row = lax.broadcasted_iota(jnp.int32, (c_len, c_len), 0)
    col = lax.broadcasted_iota(jnp.int32, (c_len, c_len), 1)
    pos = lax.broadcasted_iota(jnp.int32, (c_len, dk), 0)
    sub = pos & (HG_SUB - 1)

    pieces = []
    for s in range(HG_SUB):
        g_s = _rows_from(g, lambda b: b * HG_SUB + s, HG_SUB)
        k_s = _rows_from(k, lambda b: b * HG_SUB + s, HG_SUB)
        decay = jnp.exp(jnp.where(sub >= s, g - g_s, -jnp.inf))
        pieces.append((q * k_s * decay).astype(BF16))
    diag = jnp.dot(jnp.concatenate(pieces, axis=1), seg_ref[...], preferred_element_type=F32)
    scores = jnp.where((row ^ col) < HG_SUB, diag, 0.0)

    width = HG_SUB
    while width < c_len:
        right = (pos & width) != 0
        g_ref = _rows_from(g, lambda b: b * 2 * width + width - 1, 2 * width)
        q_s = q * jnp.exp(jnp.where(right, g - g_ref, -jnp.inf))
        k_s = k * jnp.exp(jnp.where(right, -jnp.inf, g_ref - g))
        cross = _dot_nt(q_s, k_s)
        scores = scores + jnp.where((row ^ col) < 2 * width, cross, 0.0)
        width *= 2
    return scores


def _hgrn2_kernel(q_ref, f_ref, i_ref, g_ref, lbt_ref, nw_ref, seg_ref, o_ref, state_ref, *, layer, chunks):
    c_len = HG_CHUNK

    @pl.when(pl.program_id(2) == 0)
    def _():
        state_ref[...] = jnp.zeros_like(state_ref)

    table = lbt_ref[...]
    e = jnp.exp(table - jnp.max(table, axis=0, keepdims=True))
    probs = e / jnp.sum(e, axis=0, keepdims=True)
    lb_all = jnp.sum(probs[:layer + 1], axis=0, keepdims=True) - probs[0:1]

    row = lax.broadcasted_iota(jnp.int32, (c_len, c_len), 0)
    col = lax.broadcasted_iota(jnp.int32, (c_len, c_len), 1)
    cumsum_mat = row >= col

    def chunk_body(c, carry):
        r0 = pl.multiple_of(c * c_len, c_len)
        rows = pl.ds(r0, c_len)
        for h in range(HG_HEADS_PER_STEP):
            cols = slice(h * HG_DK, (h + 1) * HG_DK)
            lb = lb_all[:, cols]
            q = _silu(q_ref[rows, cols])
            f = lb + (1.0 - lb) * jax.nn.sigmoid(f_ref[rows, cols])
            log_f = jnp.log(f)
            k = 1.0 - f
            v = i_ref[rows, cols]
            g = _dot_exact_lhs(cumsum_mat, log_f)
            state_t = state_ref[h]
            o = _dot_nt(q * jnp.exp(g), state_t)
            o = o + _dot(_hgrn2_intra_scores(q, k, g, seg_ref), v)
            g_last = g[c_len - 1:c_len]
            k_s = k * jnp.exp(g_last - g)
            state_ref[h] = state_t * jnp.exp(g_last) + _dot_tn(v, k_s)
            ms = jnp.mean(o * o, axis=-1, keepdims=True)
            y = o * lax.rsqrt(ms + NORM_EPS) * nw_ref[:, cols]
            o_ref[rows, cols] = (y * _silu(g_ref[rows, cols])).astype(o_ref.dtype)
        return carry

    lax.fori_loop(0, chunks, chunk_body, 0, unroll=2)


def _hgrn2_branch(p, lb_table, norm_w, layer, batch, seq):
    m = p.shape[0]
    ts = 512
    steps = seq // ts
    depth = lb_table.shape[0]
    wblk = HG_HEADS_PER_STEP * HG_DK

    def in_spec(col0):
        return pl.BlockSpec((ts, wblk), lambda b, h, t: (b * steps + t, col0 // wblk + h))

    seg = (jnp.arange(HG_SUB * HG_DK)[:, None] // HG_DK == jnp.arange(HG_CHUNK)[None, :] % HG_SUB).astype(BF16)

    return pl.pallas_call(
        functools.partial(_hgrn2_kernel, layer=layer, chunks=ts // HG_CHUNK),
        grid=(batch, HG_HEADS // HG_HEADS_PER_STEP, steps),
        in_specs=[in_spec(COL_HG_Q), in_spec(COL_HG_F), in_spec(COL_HG_I), in_spec(COL_HG_G),
                  pl.BlockSpec((depth, wblk), lambda b, h, t: (0, h)),
                  pl.BlockSpec((1, wblk), lambda b, h, t: (0, h)),
                  pl.BlockSpec((HG_SUB * HG_DK, HG_CHUNK), lambda b, h, t: (0, 0))],
        out_specs=pl.BlockSpec((ts, wblk), lambda b, h, t: (b * steps + t, h)),
        out_shape=jax.ShapeDtypeStruct((m, BRANCH_WIDTH), BF16),
        scratch_shapes=[pltpu.VMEM((HG_HEADS_PER_STEP, HG_DK, HG_DK), F32)],
        compiler_params=_params(("parallel", "parallel", "arbitrary")),
        name="hgrn2_branch",
    )(p, p, p, p, lb_table, norm_w.reshape(1, BRANCH_WIDTH), seg)


def _ssd_kernel(z_ref, x_ref, bc_ref, dt_ref, cw_ref, cb_ref, dtb_ref, alog_ref, dsk_ref, nw_ref,
                expand_ref, o_ref, tail_ref, state_ref):
    length = SSM_CHUNK
    inner = BRANCH_WIDTH
    gw = inner // SSM_GROUPS
    n = SSM_STATE

    @pl.when(pl.program_id(1) == 0)
    def _():
        tail_ref[...] = jnp.zeros_like(tail_ref)
        state_ref[...] = jnp.zeros_like(state_ref)

    xbc = jnp.concatenate([x_ref[...], bc_ref[...]], axis=1)
    prev_rows = tail_ref[...]
    head_sub = lax.broadcasted_iota(jnp.int32, (SSM_TAIL, xbc.shape[1]), 0)
    conv = cb_ref[...] + xbc * cw_ref[SSM_CONV - 1:SSM_CONV, :]
    for shift in range(1, SSM_CONV):
        rolled = pltpu.roll(xbc, shift, axis=0)
        head = jnp.where(head_sub < shift, pltpu.roll(prev_rows, shift, axis=0), rolled[:SSM_TAIL])
        shifted = jnp.concatenate([head, rolled[SSM_TAIL:]], axis=0)
        conv = conv + shifted * cw_ref[SSM_CONV - 1 - shift:SSM_CONV - shift, :]
    tail_ref[...] = xbc[length - SSM_TAIL:length]
    xbc_act = _silu(conv)
    x_c = xbc_act[:, :inner]
    b_m = xbc_act[:, inner:inner + SSM_GROUPS * n]
    c_m = xbc_act[:, inner + SSM_GROUPS * n:]

    dt = jax.nn.softplus(dt_ref[...] + dtb_ref[...])
    a = dt * (-jnp.exp(alog_ref[...]))
    row = lax.broadcasted_iota(jnp.int32, (length, length), 0)
    col = lax.broadcasted_iota(jnp.int32, (length, length), 1)
    causal = row >= col
    cs = _dot_exact_lhs(causal, a)
    cs_t = cs.T
    expand = expand_ref[...]
    dt_e = _dot_exact_rhs(dt, expand)
    cs_e = _dot_exact_rhs(cs, expand)
    cs_last_e = cs_e[length - 1:length]
    xs = x_c * dt_e
    in_decay_e = jnp.exp(cs_e)
    state_decay_e = jnp.exp(cs_last_e - cs_e)
    chunk_decay_e = jnp.exp(cs_last_e)
    z = z_ref[...]

    heads_per_group = SSM_HEADS // SSM_GROUPS
    p = SSM_HEADDIM
    for g in range(SSM_GROUPS):
        gcols = slice(g * gw, (g + 1) * gw)
        b_g = b_m[:, g * n:(g + 1) * n]
        c_g = c_m[:, g * n:(g + 1) * n]
        cb = _dot_nt(c_g, b_g)
        prev_t = state_ref[g]
        y_off = _dot(c_g, prev_t) * in_decay_e[:, gcols]
        parts = []
        for r in range(heads_per_group):
            h = g * heads_per_group + r
            lane = DT_LANE0 + h
            decay = jnp.exp(jnp.where(causal, cs[:, lane:lane + 1] - cs_t[lane:lane + 1, :], -jnp.inf))
            parts.append(_dot(cb * decay, xs[:, h * p:(h + 1) * p]))
        y_diag = jnp.concatenate(parts, axis=1)
        state_ref[g] = prev_t * chunk_decay_e[:, gcols] + _dot_tn(b_g, xs[:, gcols] * state_decay_e[:, gcols])
        y = (y_diag + y_off) + x_c[:, gcols] * dsk_ref[:, gcols]
        y = y * _silu(z[:, gcols])
        ms = jnp.mean(y * y, axis=-1, keepdims=True)
        o_ref[:, gcols] = (y * lax.rsqrt(ms + NORM_EPS) * nw_ref[:, gcols]).astype(o_ref.dtype)


def _ssd_branch(p_main, p_dt, conv_w, conv_b, dt_bias, a_log, d_skip, norm_w, batch, seq):
    m = p_main.shape[0]
    length = SSM_CHUNK
    steps = seq // length
    inner = BRANCH_WIDTH
    conv_dim = conv_w.shape[1]

    def pad_heads(vec):
        return jnp.pad(vec, (DT_LANE0, 0)).reshape(1, LANES)

    expand = (jnp.arange(LANES)[:, None] == (DT_LANE0 + jnp.arange(inner) // SSM_HEADDIM)[None, :]).astype(BF16)
    d_skip_e = jnp.repeat(d_skip, SSM_HEADDIM).reshape(1, inner)

    def rows(width, col0):
        return pl.BlockSpec((length, width), lambda b, t: (b * steps + t, col0 // width))

    def whole(shape):
        return pl.BlockSpec(shape, lambda b, t: (0,) * len(shape))

    return pl.pallas_call(
        _ssd_kernel,
        grid=(batch, steps),
        in_specs=[rows(inner, COL_SSM_Z), rows(inner, COL_SSM_X), rows(inner, COL_SSM_BC),
                  rows(LANES, 0),
                  whole((SSM_CONV, conv_dim)), whole((1, conv_dim)),
                  whole((1, LANES)), whole((1, LANES)), whole((1, inner)), whole((1, inner)),
                  whole((LANES, inner))],
        out_specs=pl.BlockSpec((length, inner), lambda b, t: (b * steps + t, 0)),
        out_shape=jax.ShapeDtypeStruct((m, inner), BF16),
        scratch_shapes=[pltpu.VMEM((SSM_TAIL, conv_dim), F32),
                        pltpu.VMEM((SSM_GROUPS, SSM_STATE, inner // SSM_GROUPS), F32)],
        compiler_params=_params(("parallel", "arbitrary")),
        name="ssd_branch",
    )(p_main, p_main, p_main, p_dt, conv_w, conv_b.reshape(1, conv_dim), pad_heads(dt_bias), pad_heads(a_log),
      d_skip_e, norm_w.reshape(1, inner), expand)


def _swa_kernel(sink_ref, q_ref, kc_ref, kp_ref, vc_ref, vp_ref, qw_ref, kw_ref, seg_ref, o_ref):
    t = ATT_BLOCK
    hd = ATT_HEAD_DIM
    group = ATT_Q_HEADS // ATT_KV_HEADS
    pairs = group // 2
    blk = pl.program_id(1)
    q = q_ref[...]
    keys = jnp.concatenate([kp_ref[...], kc_ref[...]], axis=0)
    vals = jnp.concatenate([vp_ref[...], vc_ref[...]], axis=0)
    lane = lax.broadcasted_iota(jnp.int32, (2 * t, LANES), 1)
    low = lane < hd

    ssq = _dot_exact_rhs(q * q, seg_ref[...])
    q_scale = lax.rsqrt(ssq * (1.0 / hd) + NORM_EPS)

    k_sq = keys * keys
    k_ssq = jnp.where(low, jnp.sum(jnp.where(low, k_sq, 0.0), axis=-1, keepdims=True),
                      jnp.sum(jnp.where(low, 0.0, k_sq), axis=-1, keepdims=True))
    k_fold = keys * lax.rsqrt(k_ssq * (1.0 / hd) + NORM_EPS) * (kw_ref[...] * qw_ref[...] * (hd ** -0.5))

    qi = lax.broadcasted_iota(jnp.int32, (t, 2 * t), 0)
    kj = lax.broadcasted_iota(jnp.int32, (t, 2 * t), 1)
    first_key = jnp.where(blk > 0, 0, t)
    bias = jnp.where((kj > qi) & (kj <= qi + t) & (kj >= first_key), 0.0, -jnp.inf)

    out_pairs = [None] * (ATT_Q_HEADS // 2)
    for c in range(ATT_KV_HEADS):
        own_low = c % 2 == 0
        k_own = jnp.where(low if own_low else ~low, k_fold, 0.0)
        v_own = jnp.where(low if own_low else ~low, vals, 0.0)
        k_other = pltpu.roll(k_own, hd, axis=1)
        v_other = pltpu.roll(v_own, hd, axis=1)
        for parity in range(2):
            k_sel = (k_own if (parity == 0) == own_low else k_other).astype(BF16)
            v_sel = (v_own if (parity == 0) == own_low else v_other).astype(BF16)
            pair0 = c * pairs
            q_stack = jnp.concatenate([q[:, (pair0 + pp) * LANES:(pair0 + pp + 1) * LANES]
                                       for pp in range(pairs)], axis=0).astype(BF16)
            s_all = lax.dot_general(q_stack, k_sel, (((1,), (1,)), ((), ())), preferred_element_type=F32)
            e_parts, inv_parts = [], []
            for pp in range(pairs):
                h = c * group + 2 * pp + parity
                s = s_all[pp * t:(pp + 1) * t] * q_scale[:, h:h + 1] + bias
                sink = sink_ref[h]
                mx = jnp.maximum(jnp.max(s, axis=-1, keepdims=True), sink)
                e = jnp.exp(s - mx)
                e_parts.append(e)
                inv_parts.append(1.0 / (jnp.sum(e, axis=-1, keepdims=True) + jnp.exp(sink - mx)))
            pv = jnp.dot(jnp.concatenate(e_parts, axis=0).astype(BF16), v_sel, preferred_element_type=F32)
            for pp in range(pairs):
                half = pv[pp * t:(pp + 1) * t] * inv_parts[pp]
                idx = pair0 + pp
                out_pairs[idx] = half if out_pairs[idx] is None else out_pairs[idx] + half
    for idx, pair in enumerate(out_pairs):
        o_ref[:, idx * LANES:(idx + 1) * LANES] = pair.astype(o_ref.dtype)


def _swa_branch(p, q_norm_w, k_norm_w, sinks, batch, seq):
    m = p.shape[0]
    t = ATT_BLOCK
    steps = seq // t
    qw = ATT_Q_HEADS * ATT_HEAD_DIM
    kvw = ATT_KV_HEADS * ATT_HEAD_DIM

    def cur(width, col0):
        return pl.BlockSpec((t, width), lambda b, i: (b * steps + i, col0 // width))

    def prev(width, col0):
        return pl.BlockSpec((t, width), lambda b, i: (b * steps + jnp.maximum(i - 1, 0), col0 // width))

    def both_halves(w):
        return jnp.concatenate([w, w]).reshape(1, LANES)

    seg = (jnp.arange(qw)[:, None] // ATT_HEAD_DIM == jnp.arange(LANES)[None, :]).astype(BF16)

    return pl.pallas_call(
        _swa_kernel,
        grid=(batch, steps),
        in_specs=[pl.BlockSpec(memory_space=pltpu.SMEM),
                  cur(qw, COL_ATT_Q), cur(kvw, COL_ATT_K), prev(kvw, COL_ATT_K),
                  cur(kvw, COL_ATT_V), prev(kvw, COL_ATT_V),
                  pl.BlockSpec((1, LANES), lambda b, i: (0, 0)),
                  pl.BlockSpec((1, LANES), lambda b, i: (0, 0)),
                  pl.BlockSpec((qw, LANES), lambda b, i: (0, 0))],
        out_specs=pl.BlockSpec((t, qw), lambda b, i: (b * steps + i, 0)),
        out_shape=jax.ShapeDtypeStruct((m, qw), BF16),
        compiler_params=_params(("parallel", "parallel")),
        name="swa_branch",
    )(sinks, p, p, p, p, p, both_halves(q_norm_w), both_halves(k_norm_w), seg)


def _ffn(x, layer, norm_w, w_up, w_down):
    h = _rmsnorm(x, norm_w[layer])
    act, w_down_padded = _up_swiglu(h, w_up, w_down, layer)
    return _down_residual(act, w_down_padded, x)


def _mixer(x, layer, batch, seq, mix_norm, w_in_t, gm_ln_w, gm_ln_b, gm_w_s, gm_b_s, hg_lb_table, hg_norm,
           ssm_conv_w, ssm_conv_b, ssm_dt_bias, ssm_a_log, ssm_d, ssm_norm,
           att_q_norm, att_k_norm, att_sinks, w_branch, w_out):
    h = _rmsnorm(x, mix_norm)
    p_main = _in_projection(h, w_in_t, layer, ROW0_MAIN, D_IN_MAIN, 512, "mixer_in_projection_main")
    nb, kb, d = w_branch.shape[1:]
    p_gates, w_branch_b, w_out_b = _in_projection(
        h, w_in_t, layer, ROW0_GATES, D_IN_GATES, 512, "mixer_in_projection_gates", BF16,
        cast_weights=(w_branch.reshape(-1, nb * kb, d), w_out))
    p_att = _in_projection(h, w_in_t, layer, ROW0_ATT, D_IN_ATT, 256, "mixer_in_projection_att",
                           single_buffer_h=False)
    p_dt = _in_projection(h, w_in_t, layer, ROW0_DT, LANES, LANES, "mixer_in_projection_dt",
                          single_buffer_h=False)
    branches = (
        _gmlp_branch(p_main, gm_ln_w, gm_ln_b, gm_w_s, gm_b_s),
        _hgrn2_branch(p_main, hg_lb_table, hg_norm, layer, batch, seq),
        _ssd_branch(p_main, p_dt, ssm_conv_w, ssm_conv_b, ssm_dt_bias, ssm_a_log, ssm_d, ssm_norm, batch, seq),
        _swa_branch(p_att, att_q_norm, att_k_norm, att_sinks, batch, seq),
    )
    merged = _merge(branches, w_branch_b.reshape(nb, kb, d), p_gates)
    return _out_residual(merged, w_out_b, x)


def kernel(x, ffn1_norm, ffn1_up, ffn1_down, mix_norm, w_in, gm_ln_w, gm_ln_b, gm_w_s, gm_b_s, hg_lb_table, hg_norm, ssm_conv_w, ssm_conv_b, ssm_dt_bias, ssm_a_log, ssm_d, ssm_norm, att_q_norm, att_k_norm, att_sinks, w_branch, w_out, ffn2_norm, ffn2_up, ffn2_down):
    batch, seq, d = x.shape
    depth = ffn1_norm.shape[0]
    xf = x.reshape(batch * seq, d)
    w_in_t = jnp.transpose(w_in, (0, 2, 1))
    for l in range(depth):
        xf = _ffn(xf, l, ffn1_norm, ffn1_up, ffn1_down)
        xf = _mixer(xf, l, batch, seq, mix_norm[l], w_in_t, gm_ln_w[l], gm_ln_b[l], gm_w_s[l], gm_b_s[l],
                    hg_lb_table, hg_norm[l], ssm_conv_w[l], ssm_conv_b[l], ssm_dt_bias[l], ssm_a_log[l],
                    ssm_d[l], ssm_norm[l], att_q_norm[l], att_k_norm[l], att_sinks[l], w_branch, w_out)
        xf = _ffn(xf, l, ffn2_norm, ffn2_up, ffn2_down)
    return xf.reshape(batch, seq, d)
```

```python
import functools

import jax
import jax.numpy as jnp
from jax import lax
from jax.experimental import pallas as pl
from jax.experimental.pallas import tpu as pltpu

F32 = jnp.float32
BF16 = jnp.bfloat16

NORM_EPS = 1e-6
LN_EPS = 1e-5

V7X_VMEM_BYTES = 64 * 1024 * 1024
VMEM_LIMIT_BYTES = V7X_VMEM_BYTES - 8 * 1024 * 1024
LANES = 128
SUBLANES = 8

BRANCH_WIDTH = 1024
N_BRANCH = 4
GM_GROUPS = 8
GM_CHUNK = 128
HG_HEADS = 8
HG_DK = 128
HG_CHUNK = 64
HG_SUB = 8
HG_HEADS_PER_STEP = 8
SSM_HEADS = 16
SSM_HEADDIM = 64
SSM_GROUPS = 4
SSM_STATE = 128
SSM_CONV = 4
SSM_CHUNK = 128
SSM_TAIL = 8
ATT_HEAD_DIM = 64
ATT_Q_HEADS = 16
ATT_KV_HEADS = 2
ATT_BLOCK = 128

ROW0_MAIN = 0
D_IN_MAIN = 9216
COL_GM = 0
COL_HG_Q = 2048
COL_HG_F = 3072
COL_HG_I = 4096
COL_HG_G = 5120
COL_SSM_Z = 6144
COL_SSM_X = 7168
COL_SSM_BC = 8192
DT_LANE0 = LANES - SSM_HEADS
ROW0_DT = D_IN_MAIN - DT_LANE0
ROW0_ATT = D_IN_MAIN + SSM_HEADS
D_IN_ATT = (ATT_Q_HEADS + 2 * ATT_KV_HEADS) * ATT_HEAD_DIM
COL_ATT_Q = 0
COL_ATT_K = ATT_Q_HEADS * ATT_HEAD_DIM
COL_ATT_V = COL_ATT_K + ATT_KV_HEADS * ATT_HEAD_DIM
ROW0_GATES = ROW0_ATT + D_IN_ATT
D_IN_GATES = N_BRANCH * 4096

D_FF = 11008
D_FF_PADDED = 11264


def _params(semantics):
    return pltpu.CompilerParams(dimension_semantics=semantics, vmem_limit_bytes=VMEM_LIMIT_BYTES)


def _silu(x):
    return x * jax.nn.sigmoid(x)


def _dot(a, b):
    return jnp.dot(a.astype(BF16), b.astype(BF16), preferred_element_type=F32)


def _dot_nt(a, b):
    return lax.dot_general(a.astype(BF16), b.astype(BF16), (((1,), (1,)), ((), ())), preferred_element_type=F32)


def _dot_tn(a, b):
    return lax.dot_general(a.astype(BF16), b.astype(BF16), (((0,), (0,)), ((), ())), preferred_element_type=F32)


def _split3(x):
    hi = x.astype(BF16)
    r1 = x - hi.astype(F32)
    mid = r1.astype(BF16)
    lo = (r1 - mid.astype(F32)).astype(BF16)
    return hi, mid, lo


def _dot_exact_lhs(mask, x):
    m = mask.astype(BF16)
    hi, mid, lo = _split3(x)
    return (jnp.dot(m, hi, preferred_element_type=F32) + jnp.dot(m, mid, preferred_element_type=F32)
            + jnp.dot(m, lo, preferred_element_type=F32))


def _dot_exact_rhs(x, mask):
    m = mask.astype(BF16)
    hi, mid, lo = _split3(x)
    return (jnp.dot(hi, m, preferred_element_type=F32) + jnp.dot(mid, m, preferred_element_type=F32)
            + jnp.dot(lo, m, preferred_element_type=F32))


def _gelu_exact(x):
    return 0.5 * x * (1.0 + lax.erf(x * (2.0 ** -0.5)))


def _rmsnorm_kernel(x_ref, w_ref, o_ref):
    x = x_ref[...]
    ms = jnp.mean(x * x, axis=-1, keepdims=True)
    o_ref[...] = (x * lax.rsqrt(ms + NORM_EPS) * w_ref[...]).astype(o_ref.dtype)


def _rmsnorm(x, w):
    m, d = x.shape
    bm = 512
    return pl.pallas_call(
        _rmsnorm_kernel,
        grid=(m // bm,),
        in_specs=[pl.BlockSpec((bm, d), lambda i: (i, 0)),
                  pl.BlockSpec((1, d), lambda i: (0, 0))],
        out_specs=pl.BlockSpec((bm, d), lambda i: (i, 0)),
        out_shape=jax.ShapeDtypeStruct((m, d), BF16),
        compiler_params=_params(("parallel",)),
        name="rmsnorm",
    )(x, w.reshape(1, d))


def _up_swiglu_kernel(x_ref, wa_ref, wb_ref, wd_ref, o_ref, wdo_ref, *, real_blocks, wd_real_blocks):
    j = pl.program_id(1)
    t = pl.program_id(0) * pl.num_programs(1) + j

    def cast_down_weight_block():
        wdo_ref[...] = jnp.where(t < wd_real_blocks, wd_ref[...], 0.0).astype(wdo_ref.dtype)

    @pl.when(j < real_blocks)
    def _():
        x = x_ref[...]
        a = jnp.dot(x, wa_ref[...].astype(BF16), preferred_element_type=F32)
        b = jnp.dot(x, wb_ref[...].astype(BF16), preferred_element_type=F32)
        o_ref[...] = (_silu(a) * b).astype(o_ref.dtype)
        cast_down_weight_block()

    @pl.when(j >= real_blocks)
    def _():
        o_ref[...] = jnp.zeros_like(o_ref)
        cast_down_weight_block()


def _up_swiglu(h, w_up, w_down, layer):
    m, k = h.shape
    n_out = w_down.shape[2]
    bm, bn = 2048, 256
    real_blocks = D_FF // bn
    last = real_blocks - 1
    nj = D_FF_PADDED // bn
    steps = (m // bm) * nj
    wd_rows = D_FF_PADDED // steps
    assert wd_rows * steps == D_FF_PADDED and D_FF % wd_rows == 0 and wd_rows % (2 * SUBLANES) == 0
    wd_real_blocks = D_FF // wd_rows

    return pl.pallas_call(
        functools.partial(_up_swiglu_kernel, real_blocks=real_blocks, wd_real_blocks=wd_real_blocks),
        grid=(m // bm, nj),
        in_specs=[pl.BlockSpec((bm, k), lambda i, j: (i, 0), pipeline_mode=pl.Buffered(1)),
                  pl.BlockSpec((None, k, bn), lambda i, j: (layer, 0, jnp.minimum(j, last))),
                  pl.BlockSpec((None, k, bn), lambda i, j: (layer, 0, real_blocks + jnp.minimum(j, last))),
                  pl.BlockSpec((None, wd_rows, n_out),
                               lambda i, j: (layer, jnp.minimum(i * nj + j, wd_real_blocks - 1), 0))],
        out_specs=[pl.BlockSpec((bm, bn), lambda i, j: (i, j)),
                   pl.BlockSpec((wd_rows, n_out), lambda i, j: (i * nj + j, 0))],
        out_shape=[jax.ShapeDtypeStruct((m, D_FF_PADDED), BF16),
                   jax.ShapeDtypeStruct((D_FF_PADDED, n_out), BF16)],
        compiler_params=_params(("arbitrary", "arbitrary")),
        name="ffn_up_swiglu",
    )(h, w_up, w_up, w_down)


def _down_residual_kernel(a_ref, w_ref, res_ref, o_ref, acc_ref, *, nk):
    k = pl.program_id(2)

    def partial_product():
        return jnp.dot(a_ref[...], w_ref[...], preferred_element_type=F32)

    @pl.when(k == 0)
    def _():
        acc_ref[...] = partial_product()

    @pl.when((k > 0) & (k < nk - 1))
    def _():
        acc_ref[...] += partial_product()

    @pl.when(k == nk - 1)
    def _():
        o_ref[...] = res_ref[...] + 0.5 * (acc_ref[...] + partial_product())


def _down_residual(act, w, res):
    m, k = act.shape
    n = w.shape[1]
    bm, bn, nk = 1024, 1024, 4
    bk = k // nk
    return pl.pallas_call(
        functools.partial(_down_residual_kernel, nk=nk),
        grid=(m // bm, n // bn, nk),
        in_specs=[pl.BlockSpec((bm, bk), lambda i, j, kk: (i, kk)),
                  pl.BlockSpec((bk, bn), lambda i, j, kk: (kk, j)),
                  pl.BlockSpec((bm, bn), lambda i, j, kk: (i, j))],
        out_specs=pl.BlockSpec((bm, bn), lambda i, j, kk: (i, j)),
        out_shape=jax.ShapeDtypeStruct((m, n), F32),
        scratch_shapes=[pltpu.VMEM((bm, bn), F32)],
        compiler_params=_params(("parallel", "parallel", "arbitrary")),
        name="ffn_down_residual",
    )(act, w, res)


def _nt_matmul_kernel(x_ref, wt_ref, *refs, n_casts):
    cast_in, o_ref, cast_out = refs[:n_casts], refs[n_casts], refs[n_casts + 1:]
    w = wt_ref[...].astype(BF16)
    o_ref[...] = lax.dot_general(x_ref[...], w, (((1,), (1,)), ((), ())),
                                 preferred_element_type=F32).astype(o_ref.dtype)
    for src, dst in zip(cast_in, cast_out):
        dst[...] = src[...].astype(dst.dtype)


def _in_projection(h, w_in_t, layer, row0, n, bn, name, out_dtype=F32, cast_weights=(), bm=2048,
                   single_buffer_h=True):
    m, k = h.shape
    x_mode = dict(pipeline_mode=pl.Buffered(1)) if single_buffer_h else {}
    nj = n // bn
    if row0 % bn == 0:
        w_spec = pl.BlockSpec((None, bn, k), lambda i, j: (layer, row0 // bn + j, 0))
    else:
        w_spec = pl.BlockSpec((None, pl.Element(bn), pl.Element(k)),
                              lambda i, j: (layer, pl.multiple_of(row0 + j * bn, SUBLANES), 0))
    steps = (m // bm) * nj
    cast_specs, cast_out_specs, cast_shapes = [], [], []
    for w in cast_weights:
        _, r, c = w.shape
        rows = r // steps
        assert rows * steps == r and rows % (2 * SUBLANES) == 0
        cast_specs.append(pl.BlockSpec((None, rows, c), lambda i, j: (layer, i * nj + j, 0)))
        cast_out_specs.append(pl.BlockSpec((rows, c), lambda i, j: (i * nj + j, 0)))
        cast_shapes.append(jax.ShapeDtypeStruct((r, c), BF16))
    out = pl.pallas_call(
        functools.partial(_nt_matmul_kernel, n_casts=len(cast_weights)),
        grid=(m // bm, nj),
        in_specs=[pl.BlockSpec((bm, k), lambda i, j: (i, 0), **x_mode), w_spec] + cast_specs,
        out_specs=[pl.BlockSpec((bm, bn), lambda i, j: (i, j))] + cast_out_specs,
        out_shape=[jax.ShapeDtypeStruct((m, n), out_dtype)] + cast_shapes,
        compiler_params=_params(("arbitrary", "arbitrary")),
        name=name,
    )(h, w_in_t, *cast_weights)
    return out if cast_weights else out[0]


def _out_residual_kernel(x_ref, w_ref, res_ref, o_ref):
    o_ref[...] = res_ref[...] + jnp.dot(x_ref[...], w_ref[...], preferred_element_type=F32)


def _out_residual(merged, w, res):
    m, k = merged.shape
    n = w.shape[1]
    bm, bn = 1024, 1024
    return pl.pallas_call(
        _out_residual_kernel,
        grid=(m // bm, n // bn),
        in_specs=[pl.BlockSpec((bm, k), lambda i, j: (i, 0)),
                  pl.BlockSpec((k, bn), lambda i, j: (0, j)),
                  pl.BlockSpec((bm, bn), lambda i, j: (i, j))],
        out_specs=pl.BlockSpec((bm, bn), lambda i, j: (i, j)),
        out_shape=jax.ShapeDtypeStruct((m, n), F32),
        compiler_params=_params(("parallel", "arbitrary")),
        name="mixer_out_residual",
    )(merged, w, res)


def _merge_kernel(b0_ref, b1_ref, b2_ref, b3_ref, w_ref, g0_ref, g1_ref, g2_ref, g3_ref, o_ref):
    branches = (b0_ref, b1_ref, b2_ref, b3_ref)
    gates = (g0_ref, g1_ref, g2_ref, g3_ref)
    merged = None
    for b in range(N_BRANCH):
        y = jnp.dot(branches[b][...], w_ref[b], preferred_element_type=F32)
        term = jax.nn.sigmoid(gates[b][...].astype(F32)) * y
        merged = term if merged is None else merged + term
    o_ref[...] = merged.astype(o_ref.dtype)


def _merge(branches, w_branch, p_gates):
    m = p_gates.shape[0]
    nb, kb, n = w_branch.shape
    bm, bn = 512, 1024
    gate_blocks = n // bn

    def gate_spec(b):
        return pl.BlockSpec((bm, bn), lambda j, i: (i, b * gate_blocks + j))

    return pl.pallas_call(
        _merge_kernel,
        grid=(n // bn, m // bm),
        in_specs=[pl.BlockSpec((bm, kb), lambda j, i: (i, 0)) for _ in range(nb)]
        + [pl.BlockSpec((nb, kb, bn), lambda j, i: (0, 0, j))]
        + [gate_spec(b) for b in range(nb)],
        out_specs=pl.BlockSpec((bm, bn), lambda j, i: (i, j)),
        out_shape=jax.ShapeDtypeStruct((m, n), BF16),
        compiler_params=_params(("parallel", "arbitrary")),
        name="branch_merge",
    )(*branches, w_branch, p_gates, p_gates, p_gates, p_gates)


def _gmlp_kernel(p_ref, lnw_ref, lnb_ref, ws_ref, bst_ref, o_ref, *, chunks):
    t = GM_CHUNK
    width = BRANCH_WIDTH
    gw = width // GM_GROUPS
    row = lax.broadcasted_iota(jnp.int32, (t, t), 0)
    col = lax.broadcasted_iota(jnp.int32, (t, t), 1)
    causal = row >= col
    mixers = [jnp.where(causal, ws_ref[g], 0.0).astype(BF16) for g in range(GM_GROUPS)]
    for c in range(chunks):
        rows = slice(c * t, (c + 1) * t)
        act = _gelu_exact(p_ref[rows, :])
        u = act[:, :width]
        v = act[:, width:]
        mu = jnp.mean(v, axis=-1, keepdims=True)
        var = jnp.mean(jnp.square(v - mu), axis=-1, keepdims=True)
        vn = (v - mu) * lax.rsqrt(var + LN_EPS) * lnw_ref[...] + lnb_ref[...]
        for g in range(GM_GROUPS):
            cols = slice(g * gw, (g + 1) * gw)
            mixed = _dot(mixers[g], vn[:, cols]) + bst_ref[:, g:g + 1]
            o_ref[rows, cols] = (u[:, cols] * mixed).astype(o_ref.dtype)


def _gmlp_branch(p, ln_w, ln_b, w_s, b_s):
    m = p.shape[0]
    tb = 512
    width = BRANCH_WIDTH
    return pl.pallas_call(
        functools.partial(_gmlp_kernel, chunks=tb // GM_CHUNK),
        grid=(m // tb,),
        in_specs=[pl.BlockSpec((tb, 2 * width), lambda i: (i, COL_GM // (2 * width))),
                  pl.BlockSpec((1, width), lambda i: (0, 0)),
                  pl.BlockSpec((1, width), lambda i: (0, 0)),
                  pl.BlockSpec((GM_GROUPS, GM_CHUNK, GM_CHUNK), lambda i: (0, 0, 0)),
                  pl.BlockSpec((GM_CHUNK, GM_GROUPS), lambda i: (0, 0))],
        out_specs=pl.BlockSpec((tb, width), lambda i: (i, 0)),
        out_shape=jax.ShapeDtypeStruct((m, width), BF16),
        compiler_params=_params(("parallel",)),
        name="gmlp_branch",
    )(p, ln_w.reshape(1, width), ln_b.reshape(1, width), w_s, b_s.T)


def _rows_from(x, index_of_block, block_rows):
    blocks = x.shape[0] // block_rows
    parts = [jnp.broadcast_to(x[index_of_block(b):index_of_block(b) + 1], (block_rows, x.shape[1]))
             for b in range(blocks)]
    return jnp.concatenate(parts, axis=0)


def _hgrn2_intra_scores(q, k, g, seg_ref):
    c_len, dk = q.shape
    row = lax.broadcasted_iota(jnp.int32, (c_len, c_len), 0)
    col = lax.broadcasted_iota(jnp.int32, (c_len, c_len), 1)
    pos = lax.broadcasted_iota(jnp.int32, (c_len, dk), 0)
    sub = pos & (HG_SUB - 1)

    pieces = []
    for s in range(HG_SUB):
        g_s = _rows_from(g, lambda b: b * HG_SUB + s, HG_SUB)
        k_s = _rows_from(k, lambda b: b * HG_SUB + s, HG_SUB)
        decay = jnp.exp(jnp.where(sub >= s, g - g_s, -jnp.inf))
        pieces.append((q * k_s * decay).astype(BF16))
    diag = jnp.dot(jnp.concatenate(pieces, axis=1), seg_ref[...], preferred_element_type=F32)
    scores = jnp.where((row ^ col) < HG_SUB, diag, 0.0)

    width = HG_SUB
    while width < c_len:
        right = (pos & width) != 0
        g_ref = _rows_from(g, lambda b: b * 2 * width + width - 1, 2 * width)
        q_s = q * jnp.exp(jnp.where(right, g - g_ref, -jnp.inf))
        k_s = k * jnp.exp(jnp.where(right, -jnp.inf, g_ref - g))
        cross = _dot_nt(q_s, k_s)
        scores = scores + jnp.where((row ^ col) < 2 * width, cross, 0.0)
        width *= 2
    return scores


def _hgrn2_kernel(q_ref, f_ref, i_ref, g_ref, lbt_ref, nw_ref, seg_ref, o_ref, state_ref, *, layer, chunks):
    c_len = HG_CHUNK

    @pl.when(pl.program_id(2) == 0)
    def _():
        state_ref[...] = jnp.zeros_like(state_ref)

    table = lbt_ref[...]
    e = jnp.exp(table - jnp.max(table, axis=0, keepdims=True))
    probs = e / jnp.sum(e, axis=0, keepdims=True)
    lb_all = jnp.sum(probs[:layer + 1], axis=0, keepdims=True) - probs[0:1]

    row = lax.broadcasted_iota(jnp.int32, (c_len, c_len), 0)
    col = lax.broadcasted_iota(jnp.int32, (c_len, c_len), 1)
    cumsum_mat = row >= col

    def chunk_body(c, carry):
        r0 = pl.multiple_of(c * c_len, c_len)
        rows = pl.ds(r0, c_len)
        for h in range(HG_HEADS_PER_STEP):
            cols = slice(h * HG_DK, (h + 1) * HG_DK)
            lb = lb_all[:, cols]
            q = _silu(q_ref[rows, cols])
            f = lb + (1.0 - lb) * jax.nn.sigmoid(f_ref[rows, cols])
            log_f = jnp.log(f)
            k = 1.0 - f
            v = i_ref[rows, cols]
            g = _dot_exact_lhs(cumsum_mat, log_f)
            state_t = state_ref[h]
            o = _dot_nt(q * jnp.exp(g), state_t)
            o = o + _dot(_hgrn2_intra_scores(q, k, g, seg_ref), v)
            g_last = g[c_len - 1:c_len]
            k_s = k * jnp.exp(g_last - g)
            state_ref[h] = state_t * jnp.exp(g_last) + _dot_tn(v, k_s)
            ms = jnp.mean(o * o, axis=-1, keepdims=True)
            y = o * lax.rsqrt(ms + NORM_EPS) * nw_ref[:, cols]
            o_ref[rows, cols] = (y * _silu(g_ref[rows, cols])).astype(o_ref.dtype)
        return carry

    lax.fori_loop(0, chunks, chunk_body, 0, unroll=2)


def _hgrn2_branch(p, lb_table, norm_w, layer, batch, seq):
    m = p.shape[0]
    ts = 512
    steps = seq // ts
    depth = lb_table.shape[0]
    wblk = HG_HEADS_PER_STEP * HG_DK

    def in_spec(col0):
        return pl.BlockSpec((ts, wblk), lambda b, h, t: (b * steps + t, col0 // wblk + h))

    seg = (jnp.arange(HG_SUB * HG_DK)[:, None] // HG_DK == jnp.arange(HG_CHUNK)[None, :] % HG_SUB).astype(BF16)

    return pl.pallas_call(
        functools.partial(_hgrn2_kernel, layer=layer, chunks=ts // HG_CHUNK),
        grid=(batch, HG_HEADS // HG_HEADS_PER_STEP, steps),
        in_specs=[in_spec(COL_HG_Q), in_spec(COL_HG_F), in_spec(COL_HG_I), in_spec(COL_HG_G),
                  pl.BlockSpec((depth, wblk), lambda b, h, t: (0, h)),
                  pl.BlockSpec((1, wblk), lambda b, h, t: (0, h)),
                  pl.BlockSpec((HG_SUB * HG_DK, HG_CHUNK), lambda b, h, t: (0, 0))],
        out_specs=pl.BlockSpec((ts, wblk), lambda b, h, t: (b * steps + t, h)),
        out_shape=jax.ShapeDtypeStruct((m, BRANCH_WIDTH), BF16),
        scratch_shapes=[pltpu.VMEM((HG_HEADS_PER_STEP, HG_DK, HG_DK), F32)],
        compiler_params=_params(("parallel", "parallel", "arbitrary")),
        name="hgrn2_branch",
    )(p, p, p, p, lb_table, norm_w.reshape(1, BRANCH_WIDTH), seg)


def _ssd_kernel(z_ref, x_ref, bc_ref, dt_ref, cw_ref, cb_ref, dtb_ref, alog_ref, dsk_ref, nw_ref,
                expand_ref, o_ref, tail_ref, state_ref):
    length = SSM_CHUNK
    inner = BRANCH_WIDTH
    gw = inner // SSM_GROUPS
    n = SSM_STATE

    @pl.when(pl.program_id(1) == 0)
    def _():
        tail_ref[...] = jnp.zeros_like(tail_ref)
        state_ref[...] = jnp.zeros_like(state_ref)

    xbc = jnp.concatenate([x_ref[...], bc_ref[...]], axis=1)
    prev_rows = tail_ref[...]
    head_sub = lax.broadcasted_iota(jnp.int32, (SSM_TAIL, xbc.shape[1]), 0)
    conv = cb_ref[...] + xbc * cw_ref[SSM_CONV - 1:SSM_CONV, :]
    for shift in range(1, SSM_CONV):
        rolled = pltpu.roll(xbc, shift, axis=0)
        head = jnp.where(head_sub < shift, pltpu.roll(prev_rows, shift, axis=0), rolled[:SSM_TAIL])
        shifted = jnp.concatenate([head, rolled[SSM_TAIL:]], axis=0)
        conv = conv + shifted * cw_ref[SSM_CONV - 1 - shift:SSM_CONV - shift, :]
    tail_ref[...] = xbc[length - SSM_TAIL:length]
    xbc_act = _silu(conv)
    x_c = xbc_act[:, :inner]
    b_m = xbc_act[:, inner:inner + SSM_GROUPS * n]
    c_m = xbc_act[:, inner + SSM_GROUPS * n:]

    dt = jax.nn.softplus(dt_ref[...] + dtb_ref[...])
    a = dt * (-jnp.exp(alog_ref[...]))
    row = lax.broadcasted_iota(jnp.int32, (length, length), 0)
    col = lax.broadcasted_iota(jnp.int32, (length, length), 1)
    causal = row >= col
    cs = _dot_exact_lhs(causal, a)
    cs_t = cs.T
    expand = expand_ref[...]
    dt_e = _dot_exact_rhs(dt, expand)
    cs_e = _dot_exact_rhs(cs, expand)
    cs_last_e = cs_e[length - 1:length]
    xs = x_c * dt_e
    in_decay_e = jnp.exp(cs_e)
    state_decay_e = jnp.exp(cs_last_e - cs_e)
    chunk_decay_e = jnp.exp(cs_last_e)
    z = z_ref[...]

    heads_per_group = SSM_HEADS // SSM_GROUPS
    p = SSM_HEADDIM
    for g in range(SSM_GROUPS):
        gcols = slice(g * gw, (g + 1) * gw)
        b_g = b_m[:, g * n:(g + 1) * n]
        c_g = c_m[:, g * n:(g + 1) * n]
        cb = _dot_nt(c_g, b_g)
        prev_t = state_ref[g]
        y_off = _dot(c_g, prev_t) * in_decay_e[:, gcols]
        parts = []
        for r in range(heads_per_group):
            h = g * heads_per_group + r
            lane = DT_LANE0 + h
            decay = jnp.exp(jnp.where(causal, cs[:, lane:lane + 1] - cs_t[lane:lane + 1, :], -jnp.inf))
            parts.append(_dot(cb * decay, xs[:, h * p:(h + 1) * p]))
        y_diag = jnp.concatenate(parts, axis=1)
        state_ref[g] = prev_t * chunk_decay_e[:, gcols] + _dot_tn(b_g, xs[:, gcols] * state_decay_e[:, gcols])
        y = (y_diag + y_off) + x_c[:, gcols] * dsk_ref[:, gcols]
        y = y * _silu(z[:, gcols])
        ms = jnp.mean(y * y, axis=-1, keepdims=True)
        o_ref[:, gcols] = (y * lax.rsqrt(ms + NORM_EPS) * nw_ref[:, gcols]).astype(o_ref.dtype)


def _ssd_branch(p_main, p_dt, conv_w, conv_b, dt_bias, a_log, d_skip, norm_w, batch, seq):
    m = p_main.shape[0]
    length = SSM_CHUNK
    steps = seq // length
    inner = BRANCH_WIDTH
    conv_dim = conv_w.shape[1]

    def pad_heads(vec):
        return jnp.pad(vec, (DT_LANE0, 0)).reshape(1, LANES)

    expand = (jnp.arange(LANES)[:, None] == (DT_LANE0 + jnp.arange(inner) // SSM_HEADDIM)[None, :]).astype(BF16)
    d_skip_e = jnp.repeat(d_skip, SSM_HEADDIM).reshape(1, inner)

    def rows(width, col0):
        return pl.BlockSpec((length, width), lambda b, t: (b * steps + t, col0 // width))

    def whole(shape):
        return pl.BlockSpec(shape, lambda b, t: (0,) * len(shape))

    return pl.pallas_call(
        _ssd_kernel,
        grid=(batch, steps),
        in_specs=[rows(inner, COL_SSM_Z), rows(inner, COL_SSM_X), rows(inner, COL_SSM_BC),
                  rows(LANES, 0),
                  whole((SSM_CONV, conv_dim)), whole((1, conv_dim)),
                  whole((1, LANES)), whole((1, LANES)), whole((1, inner)), whole((1, inner)),
                  whole((LANES, inner))],
        out_specs=pl.BlockSpec((length, inner), lambda b, t: (b * steps + t, 0)),
        out_shape=jax.ShapeDtypeStruct((m, inner), BF16),
        scratch_shapes=[pltpu.VMEM((SSM_TAIL, conv_dim), F32),
                        pltpu.VMEM((SSM_GROUPS, SSM_STATE, inner // SSM_GROUPS), F32)],
        compiler_params=_params(("parallel", "arbitrary")),
        name="ssd_branch",
    )(p_main, p_main, p_main, p_dt, conv_w, conv_b.reshape(1, conv_dim), pad_heads(dt_bias), pad_heads(a_log),
      d_skip_e, norm_w.reshape(1, inner), expand)


def _swa_kernel(sink_ref, q_ref, kc_ref, kp_ref, vc_ref, vp_ref, qw_ref, kw_ref, seg_ref, o_ref):
    t = ATT_BLOCK
    hd = ATT_HEAD_DIM
    group = ATT_Q_HEADS // ATT_KV_HEADS
    pairs = group // 2
    blk = pl.program_id(1)
    q = q_ref[...]
    keys = jnp.concatenate([kp_ref[...], kc_ref[...]], axis=0)
    vals = jnp.concatenate([vp_ref[...], vc_ref[...]], axis=0)
    lane = lax.broadcasted_iota(jnp.int32, (2 * t, LANES), 1)
    low = lane < hd

    ssq = _dot_exact_rhs(q * q, seg_ref[...])
    q_scale = lax.rsqrt(ssq * (1.0 / hd) + NORM_EPS)

    k_sq = keys * keys
    k_ssq = jnp.where(low, jnp.sum(jnp.where(low, k_sq, 0.0), axis=-1, keepdims=True),
                      jnp.sum(jnp.where(low, 0.0, k_sq), axis=-1, keepdims=True))
    k_fold = keys * lax.rsqrt(k_ssq * (1.0 / hd) + NORM_EPS) * (kw_ref[...] * qw_ref[...] * (hd ** -0.5))

    qi = lax.broadcasted_iota(jnp.int32, (t, 2 * t), 0)
    kj = lax.broadcasted_iota(jnp.int32, (t, 2 * t), 1)
    first_key = jnp.where(blk > 0, 0, t)
    bias = jnp.where((kj > qi) & (kj <= qi + t) & (kj >= first_key), 0.0, -jnp.inf)

    out_pairs = [None] * (ATT_Q_HEADS // 2)
    for c in range(ATT_KV_HEADS):
        own_low = c % 2 == 0
        k_own = jnp.where(low if own_low else ~low, k_fold, 0.0)
        v_own = jnp.where(low if own_low else ~low, vals, 0.0)
        k_other = pltpu.roll(k_own, hd, axis=1)
        v_other = pltpu.roll(v_own, hd, axis=1)
        for parity in range(2):
            k_sel = (k_own if (parity == 0) == own_low else k_other).astype(BF16)
            v_sel = (v_own if (parity == 0) == own_low else v_other).astype(BF16)
            pair0 = c * pairs
            q_stack = jnp.concatenate([q[:, (pair0 + pp) * LANES:(pair0 + pp + 1) * LANES]
                                       for pp in range(pairs)], axis=0).astype(BF16)
            s_all = lax.dot_general(q_stack, k_sel, (((1,), (1,)), ((), ())), preferred_element_type=F32)
            e_parts, inv_parts = [], []
            for pp in range(pairs):
                h = c * group + 2 * pp + parity
                s = s_all[pp * t:(pp + 1) * t] * q_scale[:, h:h + 1] + bias
                sink = sink_ref[h]
                mx = jnp.maximum(jnp.max(s, axis=-1, keepdims=True), sink)
                e = jnp.exp(s - mx)
                e_parts.append(e)
                inv_parts.append(1.0 / (jnp.sum(e, axis=-1, keepdims=True) + jnp.exp(sink - mx)))
            pv = jnp.dot(jnp.concatenate(e_parts, axis=0).astype(BF16), v_sel, preferred_element_type=F32)
            for pp in range(pairs):
                half = pv[pp * t:(pp + 1) * t] * inv_parts[pp]
                idx = pair0 + pp
                out_pairs[idx] = half if out_pairs[idx] is None else out_pairs[idx] + half
    for idx, pair in enumerate(out_pairs):
        o_ref[:, idx * LANES:(idx + 1) * LANES] = pair.astype(o_ref.dtype)


def _swa_branch(p, q_norm_w, k_norm_w, sinks, batch, seq):
    m = p.shape[0]
    t = ATT_BLOCK
    steps = seq // t
    qw = ATT_Q_HEADS * ATT_HEAD_DIM
    kvw = ATT_KV_HEADS * ATT_HEAD_DIM

    def cur(width, col0):
        return pl.BlockSpec((t, width), lambda b, i: (b * steps + i, col0 // width))

    def prev(width, col0):
        return pl.BlockSpec((t, width), lambda b, i: (b * steps + jnp.maximum(i - 1, 0), col0 // width))

    def both_halves(w):
        return jnp.concatenate([w, w]).reshape(1, LANES)

    seg = (jnp.arange(qw)[:, None] // ATT_HEAD_DIM == jnp.arange(LANES)[None, :]).astype(BF16)

    return pl.pallas_call(
        _swa_kernel,
        grid=(batch, steps),
        in_specs=[pl.BlockSpec(memory_space=pltpu.SMEM),
                  cur(qw, COL_ATT_Q), cur(kvw, COL_ATT_K), prev(kvw, COL_ATT_K),
                  cur(kvw, COL_ATT_V), prev(kvw, COL_ATT_V),
                  pl.BlockSpec((1, LANES), lambda b, i: (0, 0)),
                  pl.BlockSpec((1, LANES), lambda b, i: (0, 0)),
                  pl.BlockSpec((qw, LANES), lambda b, i: (0, 0))],
        out_specs=pl.BlockSpec((t, qw), lambda b, i: (b * steps + i, 0)),
        out_shape=jax.ShapeDtypeStruct((m, qw), BF16),
        compiler_params=_params(("parallel", "parallel")),
        name="swa_branch",
    )(sinks, p, p, p, p, p, both_halves(q_norm_w), both_halves(k_norm_w), seg)


def _ffn(x, layer, norm_w, w_up, w_down):
    h = _rmsnorm(x, norm_w[layer])
    act, w_down_padded = _up_swiglu(h, w_up, w_down, layer)
    return _down_residual(act, w_down_padded, x)


def _mixer(x, layer, batch, seq, mix_norm, w_in_t, gm_ln_w, gm_ln_b, gm_w_s, gm_b_s, hg_lb_table, hg_norm,
           ssm_conv_w, ssm_conv_b, ssm_dt_bias, ssm_a_log, ssm_d, ssm_norm,
           att_q_norm, att_k_norm, att_sinks, w_branch, w_out):
    h = _rmsnorm(x, mix_norm)
    p_main = _in_projection(h, w_in_t, layer, ROW0_MAIN, D_IN_MAIN, 512, "mixer_in_projection_main")
    nb, kb, d = w_branch.shape[1:]
    p_gates, w_branch_b, w_out_b = _in_projection(
        h, w_in_t, layer, ROW0_GATES, D_IN_GATES, 512, "mixer_in_projection_gates", BF16,
        cast_weights=(w_branch.reshape(-1, nb * kb, d), w_out))
    p_att = _in_projection(h, w_in_t, layer, ROW0_ATT, D_IN_ATT, 256, "mixer_in_projection_att",
                           single_buffer_h=False)
    p_dt = _in_projection(h, w_in_t, layer, ROW0_DT, LANES, LANES, "mixer_in_projection_dt",
                          single_buffer_h=False)
    branches = (
        _gmlp_branch(p_main, gm_ln_w, gm_ln_b, gm_w_s, gm_b_s),
        _hgrn2_branch(p_main, hg_lb_table, hg_norm, layer, batch, seq),
        _ssd_branch(p_main, p_dt, ssm_conv_w, ssm_conv_b, ssm_dt_bias, ssm_a_log, ssm_d, ssm_norm, batch, seq),
        _swa_branch(p_att, att_q_norm, att_k_norm, att_sinks, batch, seq),
    )
    merged = _merge(branches, w_branch_b.reshape(nb, kb, d), p_gates)
    return _out_residual(merged, w_out_b, x)


def kernel(x, ffn1_norm, ffn1_up, ffn1_down, mix_norm, w_in, gm_ln_w, gm_ln_b, gm_w_s, gm_b_s, hg_lb_table, hg_norm, ssm_conv_w, ssm_conv_b, ssm_dt_bias, ssm_a_log, ssm_d, ssm_norm, att_q_norm, att_k_norm, att_sinks, w_branch, w_out, ffn2_norm, ffn2_up, ffn2_down):
    batch, seq, d = x.shape
    depth = ffn1_norm.shape[0]
    xf = x.reshape(batch * seq, d)
    w_in_t = jnp.transpose(w_in, (0, 2, 1))
    for l in range(depth):
        xf = _ffn(xf, l, ffn1_norm, ffn1_up, ffn1_down)
        xf = _mixer(xf, l, batch, seq, mix_norm[l], w_in_t, gm_ln_w[l], gm_ln_b[l], gm_w_s[l], gm_b_s[l],
                    hg_lb_table, hg_norm[l], ssm_conv_w[l], ssm_conv_b[l], ssm_dt_bias[l], ssm_a_log[l],
                    ssm_d[l], ssm_norm[l], att_q_norm[l], att_k_norm[l], att_sinks[l], w_branch, w_out)
        xf = _ffn(xf, l, ffn2_norm, ffn2_up, ffn2_down)
    return xf.reshape(batch, seq, d)
```
